```python
import math
import jax, jax.numpy as jnp
from jax import lax
import numpy as np

D_MODEL = 1024
BATCH = 32
SEQ = 256
DEPTH = 4
DEC_BATCH = 4
DEC_SEQ = 1024
PAST_LEN = 256

GRID_W = 64
N_HEADS = 8
N_KV = 2
HEAD_DIM = 64
GQA = N_HEADS // N_KV
WINDOW = 128
BLOCK = 128
ATT_SCALE = HEAD_DIM ** -0.5
ROPE_BASE = 10000.0
ROPE_PAIRS = HEAD_DIM // 4
DN_HEADS = 4
DK = 128
DV = 128
DN_CONV = 5
CHUNK = 64
N_DIR = 2
ATT_Q = N_HEADS * HEAD_DIM
ATT_KV = N_KV * HEAD_DIM
DN_QK = DN_HEADS * DK
DN_V = DN_HEADS * DV
DN_CONV_CH = 2 * DN_QK + DN_V
MIX_WIDTH = ATT_Q + DN_V
IN_DIM = ATT_Q + 2 * ATT_KV + DN_CONV_CH + DN_V + 2 * N_DIR * DN_HEADS
SPLIT_POINTS = (ATT_Q, ATT_Q + ATT_KV, ATT_Q + 2 * ATT_KV, ATT_Q + 2 * ATT_KV + DN_CONV_CH,
                ATT_Q + 2 * ATT_KV + DN_CONV_CH + DN_V,
                ATT_Q + 2 * ATT_KV + DN_CONV_CH + DN_V + N_DIR * DN_HEADS)
N_EXPERTS = 16
N_GROUPS = 4
EXPERTS_PER_GROUP = N_EXPERTS // N_GROUPS
TOP_K = 2
D_FF = 512
EPS = 1e-6

kernel_name = "hybrid_swa_gdn_grouped_moe_diffusion_step"


def rms_norm(x, gain):
    xf = x.astype(jnp.float32)
    y = xf * lax.rsqrt(jnp.mean(xf * xf, axis=-1, keepdims=True) + EPS) * gain.astype(jnp.float32)
    return y.astype(x.dtype)


def l2_normalize(x):
    xf = x.astype(jnp.float32)
    return xf * lax.rsqrt(jnp.sum(xf * xf, axis=-1, keepdims=True) + EPS)


def modulation(cond, w_ada_l, b_ada_l):
    m = jnp.einsum('nd,de->ne', cond, w_ada_l.astype(jnp.float32)) + b_ada_l.astype(jnp.float32)
    return jnp.split(m[:, None, :], 6, axis=-1)


def adaln(x, gain, shift, scale):
    return (rms_norm(x, gain).astype(jnp.float32) * (1.0 + scale) + shift).astype(x.dtype)


def residual(x, gate, out):
    return x + (gate * out.astype(jnp.float32)).astype(x.dtype)


def axial_rope_tables(n_tokens):
    rows = n_tokens // GRID_W
    pos = jnp.arange(rows * GRID_W)
    row = (pos // GRID_W).astype(jnp.float32)
    col = (pos % GRID_W).astype(jnp.float32)
    inv = ROPE_BASE ** (-jnp.arange(ROPE_PAIRS, dtype=jnp.float32) / ROPE_PAIRS)
    ang = jnp.stack([row[:, None] * inv, col[:, None] * inv], axis=1)
    return jnp.cos(ang), jnp.sin(ang)


def apply_rope(x, cos, sin):
    B, L, H, _ = x.shape
    xr = x.astype(jnp.float32).reshape(B, L, H, 2, 2, ROPE_PAIRS)
    x0, x1 = xr[..., 0, :], xr[..., 1, :]
    c = cos[None, :, None]
    s = sin[None, :, None]
    out = jnp.stack([x0 * c - x1 * s, x0 * s + x1 * c], axis=-2)
    return out.reshape(B, L, H, HEAD_DIM).astype(x.dtype)


def centred_conv(x, w):
    L = x.shape[1]
    pad = (DN_CONV - 1) // 2
    xp = jnp.pad(x, ((0, 0), (pad, pad), (0, 0)))
    out = xp[:, 0:L] * w[0]
    for i in range(1, DN_CONV):
        out = out + xp[:, i:i + L] * w[i]
    return out


def mixer_inputs(h, w_in, conv_w, a_log, dt_bias, q_gain, k_gain):
    B, L, _ = h.shape
    p = jnp.einsum('bld,de->ble', h, w_in)
    qa, ka, va, qkvd, z, a, bt = jnp.split(p, SPLIT_POINTS, axis=-1)
    qa = rms_norm(qa.reshape(B, L, N_HEADS, HEAD_DIM), q_gain)
    ka = rms_norm(ka.reshape(B, L, N_KV, HEAD_DIM), k_gain)
    va = va.reshape(B, L, N_KV, HEAD_DIM)
    qkvd = jax.nn.silu(centred_conv(qkvd, conv_w))
    qd, kd, vd = jnp.split(qkvd, (DN_QK, 2 * DN_QK), axis=-1)
    qd = l2_normalize(qd.reshape(B, L, DN_HEADS, DK)) * (DK ** -0.5)
    kd = l2_normalize(kd.reshape(B, L, DN_HEADS, DK))
    vd = vd.reshape(B, L, DN_HEADS, DV).astype(jnp.float32)
    a = a.reshape(B, L, N_DIR, DN_HEADS).astype(jnp.float32)
    bt = bt.reshape(B, L, N_DIR, DN_HEADS).astype(jnp.float32)
    g = -jnp.exp(a_log.astype(jnp.float32)) * jax.nn.softplus(a + dt_bias.astype(jnp.float32))
    beta = jax.nn.sigmoid(bt)
    return qa, ka, va, qd, kd, vd, g, beta, z


def gated_delta_chunked(q, k, v, g, beta, s0):
    B, L, H, _ = q.shape
    n = L // CHUNK

    def blocks(t):
        t = t.astype(jnp.float32).reshape((B, n, CHUNK, H) + t.shape[3:])
        return jnp.moveaxis(t, 3, 1)

    q, k, v, g, beta = blocks(q), blocks(k), blocks(v), blocks(g), blocks(beta)
    gc = jnp.cumsum(g, axis=-1)
    causal = jnp.tril(jnp.ones((CHUNK, CHUNK), dtype=bool))
    strict = jnp.tril(jnp.ones((CHUNK, CHUNK), dtype=bool), -1)
    decay = jnp.exp(jnp.where(causal, gc[..., :, None] - gc[..., None, :], -jnp.inf))
    kb = k * beta[..., None]
    a_mat = jnp.where(strict, jnp.einsum('bhnik,bhnjk->bhnij', kb, k) * decay, 0.0)
    rhs = jnp.concatenate([v * beta[..., None], kb * jnp.exp(gc)[..., None]], axis=-1)
    sol = lax.linalg.triangular_solve(a_mat + jnp.eye(CHUNK, dtype=jnp.float32), rhs,
                                      left_side=True, lower=True, unit_diagonal=True)
    u0, w = sol[..., :DV], sol[..., DV:]
    qk = jnp.where(causal, jnp.einsum('bhnik,bhnjk->bhnij', q, k) * decay, 0.0)
    q_dec = q * jnp.exp(gc)[..., None]
    k_dec = k * jnp.exp(gc[..., -1:] - gc)[..., None]
    g_tot = jnp.exp(gc[..., -1])

    def step(S, xs):
        u0_c, w_c, qk_c, qd_c, kd_c, gt_c = xs
        u = u0_c - jnp.einsum('bhck,bhkv->bhcv', w_c, S)
        o = jnp.einsum('bhck,bhkv->bhcv', qd_c, S) + jnp.einsum('bhij,bhjv->bhiv', qk_c, u)
        S = S * gt_c[..., None, None] + jnp.einsum('bhck,bhcv->bhkv', kd_c, u)
        return S, o

    xs = (jnp.moveaxis(u0, 2, 0), jnp.moveaxis(w, 2, 0), jnp.moveaxis(qk, 2, 0),
          jnp.moveaxis(q_dec, 2, 0), jnp.moveaxis(k_dec, 2, 0), jnp.moveaxis(g_tot, 2, 0))
    s_final, o = lax.scan(step, s0.astype(jnp.float32), xs)
    o = jnp.moveaxis(jnp.moveaxis(o, 0, 2), 1, 3).reshape(B, L, H, DV)
    return o, s_final


def gdn_bidirectional(q, k, v, g, beta, s0):
    o_f, s_f = gated_delta_chunked(q, k, v, g[:, :, 0], beta[:, :, 0], s0[:, 0])
    flip = lambda t: jnp.flip(t, axis=1)
    o_b, s_b = gated_delta_chunked(flip(q), flip(k), flip(v), flip(g[:, :, 1]), flip(beta[:, :, 1]), s0[:, 1])
    return o_f + flip(o_b), jnp.stack([s_f, s_b], axis=1)


def gated_head_norm(o, z, gain):
    B, L = o.shape[:2]
    y = rms_norm(o, gain) * jax.nn.silu(z.reshape(B, L, DN_HEADS, DV).astype(jnp.float32))
    return y.reshape(B, L, DN_V).astype(z.dtype)


def sink_logits(sink, shape):
    return jnp.broadcast_to(sink.astype(jnp.float32).reshape(N_KV, GQA)[:, :, None, None],
                            shape[:-1] + (1,))


def context_attention(q, k, v, sink):
    B, L = q.shape[:2]
    qg = q.reshape(B, L, N_KV, GQA, HEAD_DIM)
    s = jnp.einsum('bqkgd,bskd->bkgqs', qg, k).astype(jnp.float32) * ATT_SCALE
    p = jax.nn.softmax(jnp.concatenate([s, sink_logits(sink, s.shape)], axis=-1), axis=-1)[..., :L]
    o = jnp.einsum('bkgqs,bskd->bqkgd', p.astype(v.dtype), v)
    return o.reshape(B, L, ATT_Q)


def banded_attention(q, k, v, k_ctx, v_ctx, sink):
    B, L = q.shape[:2]
    nb = L // BLOCK
    n_ctx = k_ctx.shape[1]
    qb = q.reshape(B, nb, BLOCK, N_KV, GQA, HEAD_DIM)

    def windows(t):
        tp = jnp.pad(t, ((0, 0), (BLOCK, BLOCK), (0, 0), (0, 0))).reshape(B, nb + 2, BLOCK, N_KV, HEAD_DIM)
        return jnp.concatenate([tp[:, :-2], tp[:, 1:-1], tp[:, 2:]], axis=2)

    kw, vw = windows(k), windows(v)
    s_loc = jnp.einsum('bnqkgd,bnskd->bnkgqs', qb, kw).astype(jnp.float32) * ATT_SCALE
    qi = jnp.arange(BLOCK)[:, None]
    kj = jnp.arange(3 * BLOCK)[None, :]
    key_pos = jnp.arange(nb)[:, None, None] * BLOCK - BLOCK + kj
    valid = (jnp.abs(kj - BLOCK - qi) <= WINDOW) & (key_pos >= 0) & (key_pos < L)
    s_loc = jnp.where(valid[None, :, None, None], s_loc, -jnp.inf)
    s_ctx = jnp.einsum('bnqkgd,bskd->bnkgqs', qb, k_ctx).astype(jnp.float32) * ATT_SCALE
    s = jnp.concatenate([s_loc, s_ctx, sink_logits(sink, s_loc.shape)], axis=-1)
    p = jax.nn.softmax(s, axis=-1).astype(v.dtype)
    n_loc = 3 * BLOCK
    o = (jnp.einsum('bnkgqs,bnskd->bnqkgd', p[..., :n_loc], vw)
         + jnp.einsum('bnkgqs,bskd->bnqkgd', p[..., n_loc:n_loc + n_ctx], v_ctx))
    return o.reshape(B, L, ATT_Q)


def context_mixer(h, w_in, conv_w, a_log, dt_bias, q_gain, k_gain, sink, o_gain, w_out):
    qa, ka, va, qd, kd, vd, g, beta, z = mixer_inputs(h, w_in, conv_w, a_log, dt_bias, q_gain, k_gain)
    att = context_attention(qa, ka, va, sink)
    s0 = jnp.zeros((h.shape[0], N_DIR, DN_HEADS, DK, DV), jnp.float32)
    o_dn, s_ctx = gdn_bidirectional(qd, kd, vd, g, beta, s0)
    dn = gated_head_norm(o_dn, z, o_gain)
    out = jnp.einsum('blm,md->bld', jnp.concatenate([att, dn.astype(att.dtype)], axis=-1), w_out)
    return out, ka, va, s_ctx


def latent_mixer(h, k_ctx, v_ctx, s0, cos, sin, w_in, conv_w, a_log, dt_bias, q_gain, k_gain, sink, o_gain, w_out):
    qa, ka, va, qd, kd, vd, g, beta, z = mixer_inputs(h, w_in, conv_w, a_log, dt_bias, q_gain, k_gain)
    qa = apply_rope(qa, cos, sin)
    ka = apply_rope(ka, cos, sin)
    att = banded_attention(qa, ka, va, k_ctx.astype(qa.dtype), v_ctx.astype(va.dtype), sink)
    o_dn, _ = gdn_bidirectional(qd, kd, vd, g, beta, s0)
    dn = gated_head_norm(o_dn, z, o_gain)
    return jnp.einsum('blm,md->bld', jnp.concatenate([att, dn.astype(att.dtype)], axis=-1), w_out)


def routed_moe(h, router_w, router_bias, w_gate, w_up, w_down):
    B, L, D = h.shape
    t = h.reshape(B * L, D)
    scores = jax.nn.sigmoid(jnp.einsum('td,de->te', t, router_w).astype(jnp.float32))
    biased = (scores + router_bias.astype(jnp.float32)).reshape(-1, N_GROUPS, EXPERTS_PER_GROUP)
    group_score = lax.top_k(biased, TOP_K)[0].sum(-1)
    _, g_idx = lax.top_k(group_score, 1)
    in_group = jnp.take_along_axis(biased, g_idx[:, :, None], axis=1)[:, 0]
    _, e_local = lax.top_k(in_group, TOP_K)
    e_idx = g_idx * EXPERTS_PER_GROUP + e_local
    gate = jnp.take_along_axis(scores, e_idx, axis=1)
    gate = gate / jnp.sum(gate, axis=-1, keepdims=True)
    flat_e = e_idx.reshape(-1)
    order = jnp.argsort(flat_e)
    tok = order // TOP_K
    sizes = jnp.bincount(flat_e, length=N_EXPERTS).astype(jnp.int32)
    xs = t[tok].astype(w_gate.dtype)
    hid = jax.nn.silu(lax.ragged_dot(xs, w_gate, sizes)) * lax.ragged_dot(xs, w_up, sizes)
    y = lax.ragged_dot(hid, w_down, sizes)
    y = y * gate.reshape(-1)[order][:, None].astype(y.dtype)
    return jnp.zeros((B * L, D), y.dtype).at[tok].add(y).reshape(B, L, D)


def setup_inputs(seed: int = 0) -> dict:
    key = jax.random.key(seed)
    ks = jax.random.split(key, 26)
    f32 = jnp.float32

    def nrm(k, shape, s):
        return jax.random.normal(k, shape, f32) * s

    def gain(k, shape):
        return 1.0 + 0.02 * jax.random.normal(k, shape, f32)

    dt = jnp.exp(jax.random.uniform(ks[14], (DEPTH, N_DIR, DN_HEADS), f32, math.log(1e-3), math.log(1e-1)))
    return {
        "x_prompt": nrm(ks[0], (BATCH, SEQ, D_MODEL), 1.0),
        "x_sample": nrm(ks[1], (DEC_BATCH, DEC_SEQ, D_MODEL), 1.0),
        "cache_k": nrm(ks[2], (DEC_BATCH, DEPTH, PAST_LEN, N_KV, HEAD_DIM), 1.0),
        "cache_v": nrm(ks[3], (DEC_BATCH, DEPTH, PAST_LEN, N_KV, HEAD_DIM), 1.0),
        "state_dn": nrm(ks[4], (DEC_BATCH, DEPTH, N_DIR, DN_HEADS, DK, DV), DK ** -0.5),
        "c": nrm(ks[5], (DEC_BATCH, D_MODEL), 1.0),
        "c_ctx": nrm(ks[6], (D_MODEL,), 1.0),
        "w_ada": nrm(ks[7], (DEPTH, D_MODEL, 6 * D_MODEL), 0.5 * D_MODEL ** -0.5),
        "b_ada": nrm(ks[8], (DEPTH, 6 * D_MODEL), 0.02),
        "norm_attn": gain(ks[9], (DEPTH, D_MODEL)),
        "norm_ffn": gain(ks[10], (DEPTH, D_MODEL)),
        "w_in": nrm(ks[11], (DEPTH, D_MODEL, IN_DIM), D_MODEL ** -0.5),
        "conv_w": nrm(ks[12], (DEPTH, DN_CONV, DN_CONV_CH), DN_CONV ** -0.5),
        "a_log": jnp.log(jax.random.uniform(ks[13], (DEPTH, N_DIR, DN_HEADS), f32, 1.0, 16.0)),
        "dt_bias": dt + jnp.log(-jnp.expm1(-dt)),
        "q_norm": gain(ks[15], (DEPTH, HEAD_DIM)),
        "k_norm": gain(ks[16], (DEPTH, HEAD_DIM)),
        "sink": nrm(ks[17], (DEPTH, N_HEADS), 0.5),
        "o_norm": gain(ks[18], (DEPTH, DV)),
        "w_out": nrm(ks[19], (DEPTH, MIX_WIDTH, D_MODEL), MIX_WIDTH ** -0.5),
        "router_w": nrm(ks[20], (D_MODEL, N_EXPERTS), D_MODEL ** -0.5),
        "router_bias": nrm(ks[21], (N_EXPERTS,), 0.01),
        "w_gate": nrm(ks[22], (DEPTH, N_EXPERTS, D_MODEL, D_FF), D_MODEL ** -0.5),
        "w_up": nrm(ks[23], (DEPTH, N_EXPERTS, D_MODEL, D_FF), D_MODEL ** -0.5),
        "w_down": nrm(ks[24], (DEPTH, N_EXPERTS, D_FF, D_MODEL), D_FF ** -0.5),
    }


def reference(x_prompt, x_sample, cache_k, cache_v, state_dn, c, c_ctx, w_ada, b_ada, norm_attn, norm_ffn,
              w_in, conv_w, a_log, dt_bias, q_norm, k_norm, sink, o_norm, w_out, router_w, router_bias,
              w_gate, w_up, w_down):
    cond_ctx = jax.nn.silu(c_ctx.astype(jnp.float32))[None, :]
    cond_lat = jax.nn.silu(c.astype(jnp.float32))
    cos, sin = axial_rope_tables(x_sample.shape[1])
    xp, xs = x_prompt, x_sample
    k_list, v_list, s_list = [], [], []
    for l in range(DEPTH):
        mix_w = (w_in[l], conv_w[l], a_log[l], dt_bias[l], q_norm[l], k_norm[l], sink[l], o_norm[l], w_out[l])
        moe_w = (router_w, router_bias, w_gate[l], w_up[l], w_down[l])
        sh1, sc1, gt1, sh2, sc2, gt2 = modulation(cond_ctx, w_ada[l], b_ada[l])
        mix, k_c, v_c, s_c = context_mixer(adaln(xp, norm_attn[l], sh1, sc1), *mix_w)
        xp = residual(xp, gt1, mix)
        xp = residual(xp, gt2, routed_moe(adaln(xp, norm_ffn[l], sh2, sc2), *moe_w))
        k_list.append(k_c)
        v_list.append(v_c)
        s_list.append(s_c)
        sh1, sc1, gt1, sh2, sc2, gt2 = modulation(cond_lat, w_ada[l], b_ada[l])
        mix = latent_mixer(adaln(xs, norm_attn[l], sh1, sc1), cache_k[:, l], cache_v[:, l], state_dn[:, l],
                           cos, sin, *mix_w)
        xs = residual(xs, gt1, mix)
        xs = residual(xs, gt2, routed_moe(adaln(xs, norm_ffn[l], sh2, sc2), *moe_w))
    new_k = jnp.stack(k_list, axis=1)
    new_v = jnp.stack(v_list, axis=1)
    new_state = jnp.stack(s_list, axis=1)
    return (xp, xs, new_k, new_v, new_state)
```

```python
import functools

import jax
import jax.numpy as jnp
from jax import lax
from jax.experimental import pallas as pl
from jax.experimental.pallas import tpu as pltpu

D_MODEL = 1024
BATCH = 32
SEQ = 256
DEPTH = 4
DEC_BATCH = 4
DEC_SEQ = 1024
PAST_LEN = 256
GRID_W = 64
N_HEADS = 8
N_KV = 2
HEAD_DIM = 64
WINDOW = 128
BLOCK = 128
ATT_SCALE = HEAD_DIM ** -0.5
ROPE_BASE = 10000.0
ROPE_PAIRS = HEAD_DIM // 4
DN_HEADS = 4
DK = 128
DV = 128
DN_CONV = 5
CHUNK = 64
N_DIR = 2
ATT_Q = N_HEADS * HEAD_DIM
ATT_KV = N_KV * HEAD_DIM
DN_QK = DN_HEADS * DK
DN_V = DN_HEADS * DV
DN_CONV_CH = 2 * DN_QK + DN_V
N_GATE_COLS = 2 * N_DIR * DN_HEADS
MAIN_COLS = ATT_Q + 2 * ATT_KV + DN_CONV_CH + DN_V
N_EXPERTS = 16
N_GROUPS = 4
EXPERTS_PER_GROUP = 4
TOP_K = 2
D_FF = 512
EPS = 1e-6

N_CTX_TOK = BATCH * SEQ
N_LAT_TOK = DEC_BATCH * DEC_SEQ
N_TOK = N_CTX_TOK + N_LAT_TOK
N_COND = 8

LANES = 128
TOK_TILE = 256
EXP_TILE = 256
N_EXP_TILES = (N_TOK * TOP_K) // EXP_TILE + N_EXPERTS
N_SORTED = N_EXP_TILES * EXP_TILE
VMEM_LIMIT = 56 * 1024 * 1024

F32 = jnp.float32
BF16 = jnp.bfloat16


def _cparams(*sem):
    return pltpu.CompilerParams(dimension_semantics=sem, vmem_limit_bytes=VMEM_LIMIT)


def _dot(a, b):
    return jnp.dot(a.astype(BF16), b.astype(BF16), preferred_element_type=F32)


def _dot_nt(a, b):
    return lax.dot_general(a.astype(BF16), b.astype(BF16), (((1,), (1,)), ((), ())), preferred_element_type=F32)


def _dot_tn(a, b):
    return lax.dot_general(a.astype(BF16), b.astype(BF16), (((0,), (0,)), ((), ())), preferred_element_type=F32)


def _dot_f32(a, b):
    return jnp.dot(a, b, preferred_element_type=F32, precision=lax.Precision.HIGHEST)


def _dot_split(mat01, x):
    hi = x.astype(BF16)
    lo = (x - hi.astype(F32)).astype(BF16)
    return (jnp.dot(mat01, hi, preferred_element_type=F32) + jnp.dot(mat01, lo, preferred_element_type=F32))


def _sigmoid(x):
    return 1.0 / (1.0 + jnp.exp(-x))


def _silu(x):
    return x * _sigmoid(x)


def _softplus(x):
    return jnp.maximum(x, 0.0) + jnp.log(1.0 + jnp.exp(-jnp.abs(x)))


MOD_TN = 1536


def _mod_kernel(cond_ref, w_ref, b_ref, o_ref):
    c = _silu(cond_ref[...])
    o_ref[...] = _dot(c, w_ref[...]) + b_ref[...]


def _modulation_call(cond, w_ada, b_ada):
    n_col = 6 * D_MODEL
    return pl.pallas_call(
        _mod_kernel,
        grid=(DEPTH, n_col // MOD_TN),
        in_specs=[
            pl.BlockSpec((N_COND, D_MODEL), lambda l, j: (0, 0)),
            pl.BlockSpec((None, D_MODEL, MOD_TN), lambda l, j: (l, 0, j)),
            pl.BlockSpec((None, 1, MOD_TN), lambda l, j: (l, 0, j)),
        ],
        out_specs=pl.BlockSpec((None, N_COND, MOD_TN), lambda l, j: (l, 0, j)),
        out_shape=jax.ShapeDtypeStruct((DEPTH, N_COND, n_col), F32),
        compiler_params=_cparams("arbitrary", "arbitrary"),
        name="modulation",
    )(cond, w_ada, b_ada.reshape(DEPTH, 1, n_col))


def _mod_row(i):
    n_ctx = N_CTX_TOK // TOK_TILE
    per_seq = DEC_SEQ // TOK_TILE
    return jnp.where(i < n_ctx, 0, 1 + (jnp.maximum(i - n_ctx, 0)) // per_seq)


def _adaln(x, gain, shift, scale):
    ms = jnp.mean(x * x, axis=-1, keepdims=True)
    return (x * lax.rsqrt(ms + EPS) * gain) * (1.0 + scale) + shift


def _seg_rms(x, bd, gain):
    x2 = x * x
    hi = x2.astype(BF16)
    lo = (x2 - hi.astype(F32)).astype(BF16)
    ms = jnp.dot(hi, bd, preferred_element_type=F32) + jnp.dot(lo, bd, preferred_element_type=F32)
    return x * lax.rsqrt(ms + EPS) * gain


def _rope(x, cos, sin_signed):
    w = x.shape[-1]
    lane = lax.broadcasted_iota(jnp.int32, x.shape, 1)
    first = (lane % (2 * ROPE_PAIRS)) < ROPE_PAIRS
    partner = jnp.where(first, pltpu.roll(x, w - ROPE_PAIRS, axis=1), pltpu.roll(x, ROPE_PAIRS, axis=1))
    return x * cos + partner * sin_signed


def _inproj_kernel(x_ref, mod_ref, gain_ref, w_ref, wg_ref, bd_ref, qg_ref, kg_ref, cos_ref, sin_ref,
                   q_ref, kv_ref, qkvd_ref, z_ref, ab_ref):
    i = pl.program_id(0)
    is_lat = i >= N_CTX_TOK // TOK_TILE
    shift = mod_ref[:, 0:D_MODEL]
    scale = mod_ref[:, D_MODEL:2 * D_MODEL]
    h = _adaln(x_ref[...], gain_ref[...], shift, scale).astype(BF16)

    qa = jnp.dot(h, w_ref[:, 0:ATT_Q], preferred_element_type=F32)
    qn = _seg_rms(qa, bd_ref[...], qg_ref[...])
    ka = jnp.dot(h, w_ref[:, ATT_Q:ATT_Q + ATT_KV], preferred_element_type=F32)
    kn = _seg_rms(ka, bd_ref[0:ATT_KV, 0:ATT_KV], kg_ref[...])

    @pl.when(is_lat)
    def _():
        q_ref[...] = (_rope(qn, cos_ref[...], sin_ref[...]) * ATT_SCALE).astype(BF16)
        kv_ref[:, 0:ATT_KV] = _rope(kn, cos_ref[:, 0:ATT_KV], sin_ref[:, 0:ATT_KV])

    @pl.when(jnp.logical_not(is_lat))
    def _():
        q_ref[...] = (qn * ATT_SCALE).astype(BF16)
        kv_ref[:, 0:ATT_KV] = kn

    c0 = ATT_Q + ATT_KV
    kv_ref[:, ATT_KV:2 * ATT_KV] = jnp.dot(h, w_ref[:, c0:c0 + ATT_KV], preferred_element_type=F32)
    c0 += ATT_KV
    for j in range(DN_CONV_CH // 512):
        qkvd_ref[:, j * 512:(j + 1) * 512] = jnp.dot(
            h, w_ref[:, c0 + j * 512:c0 + (j + 1) * 512], preferred_element_type=F32).astype(BF16)
    c0 += DN_CONV_CH
    z_ref[...] = jnp.dot(h, w_ref[:, c0:c0 + DN_V], preferred_element_type=F32).astype(BF16)
    ab_ref[...] = jnp.dot(h, wg_ref[...], preferred_element_type=F32)[:, 0:N_GATE_COLS]


def _inproj_call(l, x, mod, gain, w_main, w_gates, bd, qg, kg, cos, sin):
    n_ctx = N_CTX_TOK // TOK_TILE
    per_seq = DEC_SEQ // TOK_TILE

    def pos_map(i):
        return (jnp.maximum(i - n_ctx, 0) % per_seq, 0)

    tok = lambda w: pl.BlockSpec((TOK_TILE, w), lambda i: (i, 0))
    full = lambda a: pl.BlockSpec(a.shape, lambda i: (0,) * a.ndim)
    return pl.pallas_call(
        _inproj_kernel,
        grid=(N_TOK // TOK_TILE,),
        in_specs=[
            tok(D_MODEL),
            pl.BlockSpec((None, None, 1, 6 * D_MODEL), lambda i: (l, _mod_row(i), 0, 0)),
            pl.BlockSpec((None, 1, D_MODEL), lambda i: (l, 0, 0)),
            pl.BlockSpec((None, D_MODEL, MAIN_COLS), lambda i: (l, 0, 0)),
            pl.BlockSpec((None, D_MODEL, LANES), lambda i: (l, 0, 0)),
            full(bd),
            pl.BlockSpec((None, 1, ATT_Q), lambda i: (l, 0, 0)),
            pl.BlockSpec((None, 1, ATT_KV), lambda i: (l, 0, 0)),
            pl.BlockSpec((TOK_TILE, ATT_Q), pos_map),
            pl.BlockSpec((TOK_TILE, ATT_Q), pos_map),
        ],
        out_specs=[tok(ATT_Q), tok(2 * ATT_KV), tok(DN_CONV_CH), tok(DN_V), tok(N_GATE_COLS)],
        out_shape=[
            jax.ShapeDtypeStruct((N_TOK, ATT_Q), BF16),
            jax.ShapeDtypeStruct((N_TOK, 2 * ATT_KV), F32),
            jax.ShapeDtypeStruct((N_TOK, DN_CONV_CH), BF16),
            jax.ShapeDtypeStruct((N_TOK, DN_V), BF16),
            jax.ShapeDtypeStruct((N_TOK, N_GATE_COLS), F32),
        ],
        compiler_params=_cparams("arbitrary"),
        name="inproj",
    )(x, mod, gain, w_main, w_gates, bd, qg, kg, cos, sin)


def _dup_half(x, g):
    lane = lax.broadcasted_iota(jnp.int32, x.shape, 1)
    lo = lane < HEAD_DIM
    xr = pltpu.roll(x, HEAD_DIM, axis=1)
    return jnp.where(lo, x, xr) if g == 0 else jnp.where(lo, xr, x)


def _attend(sink_ref, q_ref, o_ref, key_sets):
    rows = q_ref.shape[0]
    lane = lax.broadcasted_iota(jnp.int32, (rows, LANES), 1)
    lo = lane < HEAD_DIM
    for g in range(N_KV):
        ks = [(_dup_half(k, g).astype(BF16), _dup_half(v, g).astype(BF16), valid) for k, v, valid in key_sets]
        for jj in range(2):
            j = g * 2 + jj
            qp = q_ref[:, j * LANES:(j + 1) * LANES]
            outs = []
            for e in range(2):
                qm = jnp.where(lo if e == 0 else jnp.logical_not(lo), qp, jnp.zeros_like(qp))
                sk = sink_ref[2 * j + e]
                scores = []
                m = jnp.full((rows, 1), sk, F32)
                for k, _, valid in ks:
                    s = _dot_nt(qm, k)
                    if valid is not None:
                        s = jnp.where(valid, s, -jnp.inf)
                    scores.append(s)
                    m = jnp.maximum(m, jnp.max(s, axis=-1, keepdims=True))
                den = jnp.exp(sk - m)
                acc = jnp.zeros((rows, LANES), F32)
                for s, (_, v, _) in zip(scores, ks):
                    p = jnp.exp(s - m)
                    den = den + jnp.sum(p, axis=-1, keepdims=True)
                    acc = acc + _dot(p, v)
                outs.append(acc / den)
            o_ref[:, j * LANES:(j + 1) * LANES] = jnp.where(lo, outs[0], outs[1]).astype(o_ref.dtype)


def _attn_ctx_kernel(sink_ref, q_ref, kv_ref, o_ref):
    _attend(sink_ref, q_ref, o_ref, [(kv_ref[:, 0:ATT_KV], kv_ref[:, ATT_KV:2 * ATT_KV], None)])


def _attn_lat_kernel(sink_ref, q_ref, kv_ref, ck_ref, cv_ref, o_ref):
    n = pl.program_id(1)
    span = 3 * BLOCK
    start = pl.multiple_of(jnp.clip((n - 1) * BLOCK, 0, DEC_SEQ - span), BLOCK)
    kw = kv_ref[pl.ds(start, span), 0:ATT_KV]
    vw = kv_ref[pl.ds(start, span), ATT_KV:2 * ATT_KV]
    qpos = n * BLOCK + lax.broadcasted_iota(jnp.int32, (BLOCK, span), 0)
    kpos = start + lax.broadcasted_iota(jnp.int32, (BLOCK, span), 1)
    valid = jnp.abs(kpos - qpos) <= WINDOW
    _attend(sink_ref, q_ref, o_ref, [(kw, vw, valid), (ck_ref[...], cv_ref[...], None)])


def _attn_ctx_call(sink_l, q, kv):
    return pl.pallas_call(
        _attn_ctx_kernel,
        grid=(BATCH,),
        in_specs=[
            pl.BlockSpec(memory_space=pltpu.SMEM),
            pl.BlockSpec((SEQ, ATT_Q), lambda b: (b, 0)),
            pl.BlockSpec((SEQ, 2 * ATT_KV), lambda b: (b, 0)),
        ],
        out_specs=pl.BlockSpec((SEQ, ATT_Q), lambda b: (b, 0)),
        out_shape=jax.ShapeDtypeStruct((N_CTX_TOK, ATT_Q), BF16),
        compiler_params=_cparams("arbitrary"),
        name="attn_ctx",
    )(sink_l, q, kv)


def _attn_lat_call(l, sink_l, q, kv, cache_k, cache_v):
    nb = DEC_SEQ // BLOCK
    q0 = N_CTX_TOK // BLOCK
    s0 = N_CTX_TOK // DEC_SEQ
    return pl.pallas_call(
        _attn_lat_kernel,
        grid=(DEC_BATCH, nb),
        in_specs=[
            pl.BlockSpec(memory_space=pltpu.SMEM),
            pl.BlockSpec((BLOCK, ATT_Q), lambda b, n: (q0 + b * nb + n, 0)),
            pl.BlockSpec((DEC_SEQ, 2 * ATT_KV), lambda b, n: (s0 + b, 0)),
            pl.BlockSpec((None, None, PAST_LEN, ATT_KV), lambda b, n: (b, l, 0, 0)),
            pl.BlockSpec((None, None, PAST_LEN, ATT_KV), lambda b, n: (b, l, 0, 0)),
        ],
        out_specs=pl.BlockSpec((BLOCK, ATT_Q), lambda b, n: (b * nb + n, 0)),
        out_shape=jax.ShapeDtypeStruct((N_LAT_TOK, ATT_Q), BF16),
        compiler_params=_cparams("arbitrary", "arbitrary"),
        name="attn_lat",
    )(sink_l, q, kv, cache_k, cache_v)


def _unit_tri_inverse(a, row, col):
    eye = (row == col).astype(F32)
    a8 = jnp.where((row // 8) == (col // 8), a, 0.0)
    a8_2 = _dot_f32(a8, a8)
    a8_4 = _dot_f32(a8_2, a8_2)
    t = eye - a8
    t = t + _dot_f32(t, a8_2)
    t = t + _dot_f32(t, a8_4)
    b = 8
    while b < CHUNK:
        off = jnp.where(((row // (2 * b)) == (col // (2 * b))) & ((row // b) != (col // b)), a, 0.0)
        t = t - _dot_f32(_dot_f32(t, off), t)
        b *= 2
    return t


def _gdn_kernel(*refs, seq_len, has_s0, write_state):
    qkvd_ref, z_ref, ab_ref, convw_ref, alog_ref, dtb_ref, ogain_ref = refs[:7]
    pos = 7
    s0_ref = None
    if has_s0:
        s0_ref = refs[pos]
        pos += 1
    dn_ref = refs[pos]
    pos += 1
    sfin_ref = None
    if write_state:
        sfin_ref = refs[pos]
        pos += 1
    qn_s, kn_s, vn_s, gb_s, o_s, st_s = refs[pos:]
    L = seq_len
    n_chunks = L // CHUNK

    trow = lax.broadcasted_iota(jnp.int32, (L, LANES), 0)
    for cb in range(DN_CONV_CH // LANES):
        cs = slice(cb * LANES, (cb + 1) * LANES)
        x = qkvd_ref[:, cs].astype(F32)
        acc = None
        for i in range(DN_CONV):
            sh = i - (DN_CONV - 1) // 2
            if sh == 0:
                term = x
            else:
                xr = pltpu.roll(x, (-sh) % L, axis=0)
                ok = (trow + sh >= 0) & (trow + sh < L)
                term = jnp.where(ok, xr, 0.0)
            term = term * convw_ref[i:i + 1, cs]
            acc = term if acc is None else acc + term
        y = _silu(acc)
        kind, head = divmod(cb, DN_HEADS)
        hs = slice(head * DK, (head + 1) * DK)
        if kind == 0:
            qn_s[:, hs] = y * lax.rsqrt(jnp.sum(y * y, axis=-1, keepdims=True) + EPS) * (DK ** -0.5)
        elif kind == 1:
            kn_s[:, hs] = y * lax.rsqrt(jnp.sum(y * y, axis=-1, keepdims=True) + EPS)
        else:
            vn_s[:, hs] = y

    ab = ab_ref[...]
    glane = lax.broadcasted_iota(jnp.int32, ab.shape, 1)
    g = -jnp.exp(alog_ref[...]) * _softplus(ab + dtb_ref[...])
    gb_s[...] = jnp.where(glane < N_DIR * DN_HEADS, g, _sigmoid(ab))

    for d in range(N_DIR):
        for h in range(DN_HEADS):
            if has_s0:
                st_s[d * DN_HEADS + h] = s0_ref[d, h]
            else:
                st_s[d * DN_HEADS + h] = jnp.zeros((DK, DV), F32)

    row = lax.broadcasted_iota(jnp.int32, (CHUNK, CHUNK), 0)
    col = lax.broadcasted_iota(jnp.int32, (CHUNK, CHUNK), 1)
    before_incl = [col <= row, col >= row]
    before_strict = [col < row, col > row]
    tri01 = [m.astype(BF16) for m in before_incl]
    after_strict01 = [m.astype(F32) for m in before_strict]

    def chunk_step(n, carry):
        for d in range(N_DIR):
            c = n if d == 0 else n_chunks - 1 - n
            r0 = pl.multiple_of(c * CHUNK, CHUNK)
            rs = pl.ds(r0, CHUNK)
            gbc = gb_s[rs, :]
            gc_all = _dot_split(tri01[d], gbc)
            last = CHUNK - 1 if d == 0 else 0
            for h in range(DN_HEADS):
                cidx = d * DN_HEADS + h
                hs = slice(h * DK, (h + 1) * DK)
                q = qn_s[rs, hs]
                k = kn_s[rs, hs]
                v = vn_s[rs, hs]
                gcol = gbc[:, cidx:cidx + 1]
                beta = gbc[:, N_DIR * DN_HEADS + cidx:N_DIR * DN_HEADS + cidx + 1]
                gc = gc_all[:, cidx:cidx + 1]
                gc_last = gc_all[last:last + 1, cidx:cidx + 1]
                gdiff = _dot_split(tri01[d], gcol * after_strict01[d])
                decay = jnp.where(before_incl[d], jnp.exp(gdiff), 0.0)
                kb = k * beta
                a_mat = jnp.where(before_strict[d], _dot_nt(kb, k) * decay, 0.0)
                t_inv = _unit_tri_inverse(a_mat, row, col)
                e_gc = jnp.exp(gc)
                u0 = _dot_f32(t_inv, v * beta)
                w = _dot_f32(t_inv, kb * e_gc)
                qk = jnp.where(before_incl[d], _dot_nt(q, k) * decay, 0.0)
                q_dec = q * e_gc
                k_dec = k * jnp.exp(gc_last - gc)
                s_prev = st_s[cidx]
                u = u0 - _dot(w, s_prev)
                o = _dot(q_dec, s_prev) + _dot(qk, u)
                st_s[cidx] = s_prev * jnp.exp(gc_last) + _dot_tn(k_dec, u)
                o_s[d, rs, hs] = o
        return carry

    lax.fori_loop(0, n_chunks, chunk_step, 0)

    for h in range(DN_HEADS):
        hs = slice(h * DV, (h + 1) * DV)
        o = o_s[0, :, hs] + o_s[1, :, hs]
        y = o * lax.rsqrt(jnp.mean(o * o, axis=-1, keepdims=True) + EPS) * ogain_ref[...]
        dn_ref[:, hs] = (y * _silu(z_ref[:, hs].astype(F32))).astype(dn_ref.dtype)
    if write_state:
        for d in range(N_DIR):
            for h in range(DN_HEADS):
                sfin_ref[d, h] = st_s[d * DN_HEADS + h]


def _gdn_call(l, qkvd, z, ab, conv_w, alog16, dtb16, o_gain, state_dn, *, latent):
    L = DEC_SEQ if latent else SEQ
    n_seq = DEC_BATCH if latent else BATCH
    base = N_CTX_TOK // L if latent else 0
    seq = lambda w: pl.BlockSpec((L, w), lambda b: (base + b, 0))
    in_specs = [
        seq(DN_CONV_CH), seq(DN_V), seq(N_GATE_COLS),
        pl.BlockSpec((None, DN_CONV, DN_CONV_CH), lambda b: (l, 0, 0)),
        pl.BlockSpec((None, 1, N_GATE_COLS), lambda b: (l, 0, 0)),
        pl.BlockSpec((None, 1, N_GATE_COLS), lambda b: (l, 0, 0)),
        pl.BlockSpec((None, 1, DV), lambda b: (l, 0, 0)),
    ]
    args = [qkvd, z, ab, conv_w, alog16, dtb16, o_gain]
    out_specs = [pl.BlockSpec((L, DN_V), lambda b: (b, 0))]
    out_shape = [jax.ShapeDtypeStruct((n_seq * L, DN_V), BF16)]
    if latent:
        in_specs.append(pl.BlockSpec((None, None, N_DIR, DN_HEADS, DK, DV), lambda b: (b, l, 0, 0, 0, 0)))
        args.append(state_dn)
    else:
        out_specs.append(pl.BlockSpec((None, N_DIR, DN_HEADS, DK, DV), lambda b: (b, 0, 0, 0, 0)))
        out_shape.append(jax.ShapeDtypeStruct((n_seq, N_DIR, DN_HEADS, DK, DV), F32))
    return pl.pallas_call(
        functools.partial(_gdn_kernel, seq_len=L, has_s0=latent, write_state=not latent),
        grid=(n_seq,),
        in_specs=in_specs,
        out_specs=out_specs,
        out_shape=out_shape,
        scratch_shapes=[
            pltpu.VMEM((L, DN_QK), F32), pltpu.VMEM((L, DN_QK), F32), pltpu.VMEM((L, DN_V), F32),
            pltpu.VMEM((L, N_GATE_COLS), F32), pltpu.VMEM((N_DIR, L, DN_V), F32),
            pltpu.VMEM((N_DIR * DN_HEADS, DK, DV), F32),
        ],
        compiler_params=_cparams("arbitrary"),
        name="gdn_lat" if latent else "gdn_ctx",
    )(*args)


def _top2_of4(vals):
    m1 = jnp.maximum(jnp.maximum(vals[0], vals[1]), jnp.maximum(vals[2], vals[3]))
    i1 = jnp.where(vals[0] == m1, 0, jnp.where(vals[1] == m1, 1, jnp.where(vals[2] == m1, 2, 3)))
    rest = [jnp.where(i1 == j, -jnp.inf, vals[j]) for j in range(4)]
    m2 = jnp.maximum(jnp.maximum(rest[0], rest[1]), jnp.maximum(rest[2], rest[3]))
    i2 = jnp.where(rest[0] == m2, 0, jnp.where(rest[1] == m2, 1, jnp.where(rest[2] == m2, 2, 3)))
    return m1, i1, m2, i2


def _outproj_kernel(x_ref, att_ref, dn_ref, mod_ref, gain_ref, wo_ref, rw_ref, rb_ref, lt_ref,
                    x1_ref, h2_ref, ri_ref, rg_ref, cnt_ref, carry_s):
    i = pl.program_id(0)

    @pl.when(i == 0)
    def _():
        carry_s[...] = jnp.zeros_like(carry_s)

    gate1 = mod_ref[:, 2 * D_MODEL:3 * D_MODEL]
    shift2 = mod_ref[:, 3 * D_MODEL:4 * D_MODEL]
    scale2 = mod_ref[:, 4 * D_MODEL:5 * D_MODEL]
    mix = (jnp.dot(att_ref[...], wo_ref[0:ATT_Q, :], preferred_element_type=F32)
           + jnp.dot(dn_ref[...], wo_ref[ATT_Q:ATT_Q + DN_V, :], preferred_element_type=F32))
    x1 = x_ref[...] + gate1 * mix
    x1_ref[...] = x1
    h2 = _adaln(x1, gain_ref[...], shift2, scale2)
    h2_ref[...] = h2

    scores = _sigmoid(_dot_f32(h2, rw_ref[...]))
    biased = scores + rb_ref[...]
    cols = [biased[:, e:e + 1] for e in range(N_EXPERTS)]
    per_group = [_top2_of4(cols[g * 4:(g + 1) * 4]) for g in range(N_GROUPS)]
    gs = [pg[0] + pg[2] for pg in per_group]
    _, gsel, _, _ = _top2_of4(gs)
    pick = lambda k: jnp.where(gsel == 0, per_group[0][k], jnp.where(gsel == 1, per_group[1][k],
                               jnp.where(gsel == 2, per_group[2][k], per_group[3][k])))
    e1 = gsel * EXPERTS_PER_GROUP + pick(1)
    e2 = gsel * EXPERTS_PER_GROUP + pick(3)
    elane = lax.broadcasted_iota(jnp.int32, scores.shape, 1)
    oh1 = elane == e1
    oh2 = elane == e2
    s1 = jnp.sum(jnp.where(oh1, scores, 0.0), axis=-1, keepdims=True)
    s2 = jnp.sum(jnp.where(oh2, scores, 0.0), axis=-1, keepdims=True)
    tot = s1 + s2
    glane = lax.broadcasted_iota(jnp.int32, rg_ref.shape, 1)
    rg_ref[...] = jnp.where(glane == 0, s1 / tot, s2 / tot)

    sel = (oh1 | oh2).astype(BF16)
    before = jnp.dot(lt_ref[...], sel, preferred_element_type=F32) + carry_s[...]
    r1 = jnp.sum(jnp.where(oh1, before, 0.0), axis=-1, keepdims=True)
    r2 = jnp.sum(jnp.where(oh2, before, 0.0), axis=-1, keepdims=True)
    carry_s[...] = carry_s[...] + jnp.sum(sel.astype(F32), axis=0, keepdims=True)
    cnt_ref[...] = carry_s[...].astype(jnp.int32)
    ilane = lax.broadcasted_iota(jnp.int32, ri_ref.shape, 1)
    ri_ref[...] = jnp.where(ilane == 0, e1, jnp.where(ilane == 1, e2, jnp.where(
        ilane == 2, r1.astype(jnp.int32), r2.astype(jnp.int32))))


def _outproj_call(l, x, att, dn, mod, gain, w_out, router_w, router_b, lt):
    tok = lambda w: pl.BlockSpec((TOK_TILE, w), lambda i: (i, 0))
    full = lambda a: pl.BlockSpec(a.shape, lambda i: (0,) * a.ndim)
    return pl.pallas_call(
        _outproj_kernel,
        grid=(N_TOK // TOK_TILE,),
        in_specs=[
            tok(D_MODEL), tok(ATT_Q), tok(DN_V),
            pl.BlockSpec((None, None, 1, 6 * D_MODEL), lambda i: (l, _mod_row(i), 0, 0)),
            pl.BlockSpec((None, 1, D_MODEL), lambda i: (l, 0, 0)),
            pl.BlockSpec((None, ATT_Q + DN_V, D_MODEL), lambda i: (l, 0, 0)),
            full(router_w), full(router_b), full(lt),
        ],
        out_specs=[tok(D_MODEL), tok(D_MODEL), tok(4), tok(2), pl.BlockSpec((1, N_EXPERTS), lambda i: (0, 0))],
        out_shape=[
            jax.ShapeDtypeStruct((N_TOK, D_MODEL), F32),
            jax.ShapeDtypeStruct((N_TOK, D_MODEL), F32),
            jax.ShapeDtypeStruct((N_TOK, 4), jnp.int32),
            jax.ShapeDtypeStruct((N_TOK, 2), F32),
            jax.ShapeDtypeStruct((1, N_EXPERTS), jnp.int32),
        ],
        scratch_shapes=[pltpu.VMEM((1, N_EXPERTS), F32)],
        compiler_params=_cparams("arbitrary"),
        name="outproj_router",
    )(x, att, dn, mod, gain, w_out, router_w, router_b, lt)


def _dispatch_kernel(pos_ref, h2_ref, buf_in_ref, sorted_ref, sem):
    del buf_in_ref

    def row_copy(r, k):
        return pltpu.make_async_copy(h2_ref.at[pl.ds(r, 1), :], sorted_ref.at[pl.ds(pos_ref[0, k * TOK_TILE + r], 1), :], sem)

    def start(r, c):
        row_copy(r, 0).start()
        row_copy(r, 1).start()
        return c

    def wait(r, c):
        row_copy(r, 0).wait()
        row_copy(r, 1).wait()
        return c

    lax.fori_loop(0, TOK_TILE, start, 0)
    lax.fori_loop(0, TOK_TILE, wait, 0)


def _dispatch_call(pos_tiles, h2, zeros_sorted):
    return pl.pallas_call(
        _dispatch_kernel,
        grid=(N_TOK // TOK_TILE,),
        in_specs=[
            pl.BlockSpec((None, 1, TOP_K * TOK_TILE), lambda i: (i, 0, 0), memory_space=pltpu.SMEM),
            pl.BlockSpec((TOK_TILE, D_MODEL), lambda i: (i, 0)),
            pl.BlockSpec(memory_space=pl.ANY),
        ],
        out_specs=pl.BlockSpec(memory_space=pl.ANY),
        out_shape=jax.ShapeDtypeStruct((N_SORTED, D_MODEL), F32),
        scratch_shapes=[pltpu.SemaphoreType.DMA(())],
        input_output_aliases={2: 0},
        compiler_params=_cparams("arbitrary"),
        name="moe_dispatch",
    )(pos_tiles, h2, zeros_sorted)


def _experts_kernel(te_ref, nu_ref, x_ref, wg_ref, wu_ref, wd_ref, y_ref):
    used = pl.program_id(0) < nu_ref[0]

    @pl.when(used)
    def _():
        x = x_ref[...].astype(BF16)
        hid = _silu(jnp.dot(x, wg_ref[...], preferred_element_type=F32)) * jnp.dot(x, wu_ref[...], preferred_element_type=F32)
        y_ref[...] = jnp.dot(hid.astype(BF16), wd_ref[...], preferred_element_type=F32)

    @pl.when(jnp.logical_not(used))
    def _():
        y_ref[...] = jnp.zeros_like(y_ref)


def _experts_call(l, tile_expert, n_used, xs, w_gate, w_up, w_down):
    row = lambda j, te, nu: (jnp.minimum(j, nu[0] - 1), 0)
    grid_spec = pltpu.PrefetchScalarGridSpec(
        num_scalar_prefetch=2,
        grid=(N_EXP_TILES,),
        in_specs=[
            pl.BlockSpec((EXP_TILE, D_MODEL), row),
            pl.BlockSpec((None, None, D_MODEL, D_FF), lambda j, te, nu: (l, te[j], 0, 0)),
            pl.BlockSpec((None, None, D_MODEL, D_FF), lambda j, te, nu: (l, te[j], 0, 0)),
            pl.BlockSpec((None, None, D_FF, D_MODEL), lambda j, te, nu: (l, te[j], 0, 0)),
        ],
        out_specs=pl.BlockSpec((EXP_TILE, D_MODEL), lambda j, te, nu: (j, 0)),
    )
    return pl.pallas_call(
        _experts_kernel,
        grid_spec=grid_spec,
        out_shape=jax.ShapeDtypeStruct((N_SORTED, D_MODEL), F32),
        compiler_params=_cparams("arbitrary"),
        name="moe_experts",
    )(tile_expert, n_used, xs, w_gate, w_up, w_down)


def _combine_kernel(pos_ref, x1_ref, rg_ref, mod_ref, y_ref, out_ref, buf_s, sem):
    def row_copy(r, k):
        return pltpu.make_async_copy(y_ref.at[pl.ds(pos_ref[0, k * TOK_TILE + r], 1), :], buf_s.at[k, pl.ds(r, 1), :], sem)

    def start(r, c):
        row_copy(r, 0).start()
        row_copy(r, 1).start()
        return c

    def wait(r, c):
        row_copy(r, 0).wait()
        row_copy(r, 1).wait()
        return c

    lax.fori_loop(0, TOK_TILE, start, 0)
    lax.fori_loop(0, TOK_TILE, wait, 0)
    gate2 = mod_ref[:, 5 * D_MODEL:6 * D_MODEL]
    moe = buf_s[0] * rg_ref[:, 0:1] + buf_s[1] * rg_ref[:, 1:2]
    out_ref[...] = x1_ref[...] + gate2 * moe


def _combine_call(l, pos_tiles, x1, rg, mod, y_sorted):
    tok = lambda w: pl.BlockSpec((TOK_TILE, w), lambda i: (i, 0))
    return pl.pallas_call(
        _combine_kernel,
        grid=(N_TOK // TOK_TILE,),
        in_specs=[
            pl.BlockSpec((None, 1, TOP_K * TOK_TILE), lambda i: (i, 0, 0), memory_space=pltpu.SMEM),
            tok(D_MODEL), tok(2),
            pl.BlockSpec((None, None, 1, 6 * D_MODEL), lambda i: (l, _mod_row(i), 0, 0)),
            pl.BlockSpec(memory_space=pl.ANY),
        ],
        out_specs=tok(D_MODEL),
        out_shape=jax.ShapeDtypeStruct((N_TOK, D_MODEL), F32),
        scratch_shapes=[pltpu.VMEM((TOP_K, TOK_TILE, D_MODEL), F32), pltpu.SemaphoreType.DMA(())],
        compiler_params=_cparams("arbitrary"),
        name="moe_combine",
    )(pos_tiles, x1, rg, mod, y_sorted)


def _rope_tables():
    pos = jnp.arange(DEC_SEQ)
    r = (pos // GRID_W).astype(F32)
    c = (pos % GRID_W).astype(F32)
    inv = ROPE_BASE ** (-jnp.arange(ROPE_PAIRS, dtype=F32) / ROPE_PAIRS)
    ar, ac = r[:, None] * inv, c[:, None] * inv
    cos = jnp.concatenate([jnp.cos(ar), jnp.cos(ar), jnp.cos(ac), jnp.cos(ac)], axis=-1)
    sin = jnp.concatenate([-jnp.sin(ar), jnp.sin(ar), -jnp.sin(ac), jnp.sin(ac)], axis=-1)
    return jnp.tile(cos, (1, N_HEADS)), jnp.tile(sin, (1, N_HEADS))


def _routing_layout(ri, counts):
    counts = counts.reshape(N_EXPERTS)
    padded = ((counts + EXP_TILE - 1) // EXP_TILE) * EXP_TILE
    ends = jnp.cumsum(padded)
    offs = ends - padded
    pos = offs[ri[:, 0:2]] + ri[:, 2:4]
    n_used = (ends[-1] // EXP_TILE).astype(jnp.int32)
    tile_start = jnp.arange(N_EXP_TILES, dtype=jnp.int32) * EXP_TILE
    tile_expert = jnp.minimum(jnp.searchsorted(ends, tile_start, side="right"), N_EXPERTS - 1).astype(jnp.int32)
    last = tile_expert[jnp.maximum(n_used - 1, 0)]
    tile_expert = jnp.where(tile_start < ends[-1], tile_expert, last)
    pos_tiles = pos.reshape(N_TOK // TOK_TILE, TOK_TILE, TOP_K).transpose(0, 2, 1).reshape(
        N_TOK // TOK_TILE, 1, TOP_K * TOK_TILE).astype(jnp.int32)
    return pos_tiles, tile_expert, n_used.reshape(1)


def kernel(x_prompt, x_sample, cache_k, cache_v, state_dn, c, c_ctx, w_ada, b_ada, norm_attn, norm_ffn,
           w_in, conv_w, a_log, dt_bias, q_norm, k_norm, sink, o_norm, w_out, router_w, router_bias,
           w_gate, w_up, w_down):
    x = jnp.concatenate([x_prompt.reshape(N_CTX_TOK, D_MODEL), x_sample.reshape(N_LAT_TOK, D_MODEL)], axis=0)
    cond = jnp.concatenate([c_ctx[None, :], c, jnp.zeros((N_COND - 1 - DEC_BATCH, D_MODEL), F32)], axis=0)
    mod = _modulation_call(cond, w_ada, b_ada).reshape(DEPTH, N_COND, 1, 6 * D_MODEL)

    w_main = w_in[:, :, :MAIN_COLS].astype(BF16)
    w_gates = jnp.pad(w_in[:, :, MAIN_COLS:], ((0, 0), (0, 0), (0, LANES - N_GATE_COLS))).astype(BF16)
    w_out_b = w_out.astype(BF16)
    wg_b, wu_b, wd_b = w_gate.astype(BF16), w_up.astype(BF16), w_down.astype(BF16)
    seg = jnp.arange(ATT_Q) // HEAD_DIM
    bd = jnp.where(seg[:, None] == seg[None, :], 1.0 / HEAD_DIM, 0.0).astype(BF16)
    qg = jnp.tile(q_norm, (1, N_HEADS)).reshape(DEPTH, 1, ATT_Q)
    kg = jnp.tile(k_norm, (1, N_KV)).reshape(DEPTH, 1, ATT_KV)
    cos, sin = _rope_tables()
    gain1 = norm_attn.reshape(DEPTH, 1, D_MODEL)
    gain2 = norm_ffn.reshape(DEPTH, 1, D_MODEL)
    pad8 = lambda a: jnp.pad(a.reshape(DEPTH, 1, N_DIR * DN_HEADS), ((0, 0), (0, 0), (0, N_GATE_COLS - N_DIR * DN_HEADS)))
    alog16, dtb16 = pad8(a_log), pad8(dt_bias)
    o_gain = o_norm.reshape(DEPTH, 1, DV)
    ck = cache_k.reshape(DEC_BATCH, DEPTH, PAST_LEN, ATT_KV)
    cv = cache_v.reshape(DEC_BATCH, DEPTH, PAST_LEN, ATT_KV)
    rb = router_bias.reshape(1, N_EXPERTS)
    tri = jnp.arange(TOK_TILE)
    lt = (tri[None, :] < tri[:, None]).astype(BF16)
    zeros_sorted = jnp.zeros((N_SORTED, D_MODEL), F32)

    k_list, v_list, s_list = [], [], []
    for l in range(DEPTH):
        q, kv, qkvd, z, ab = _inproj_call(l, x, mod, gain1, w_main, w_gates, bd, qg, kg, cos, sin)
        k_list.append(kv[:N_CTX_TOK, 0:ATT_KV].reshape(BATCH, SEQ, N_KV, HEAD_DIM))
        v_list.append(kv[:N_CTX_TOK, ATT_KV:].reshape(BATCH, SEQ, N_KV, HEAD_DIM))
        att_c = _attn_ctx_call(sink[l], q, kv)
        att_l = _attn_lat_call(l, sink[l], q, kv, ck, cv)
        dn_c, s_c = _gdn_call(l, qkvd, z, ab, conv_w, alog16, dtb16, o_gain, state_dn, latent=False)
        (dn_l,) = _gdn_call(l, qkvd, z, ab, conv_w, alog16, dtb16, o_gain, state_dn, latent=True)
        s_list.append(s_c)
        att = jnp.concatenate([att_c, att_l], axis=0)
        dn = jnp.concatenate([dn_c, dn_l], axis=0)
        x1, h2, ri, rg, counts = _outproj_call(l, x, att, dn, mod, gain2, w_out_b, router_w, rb, lt)
        pos_tiles, tile_expert, n_used = _routing_layout(ri, counts)
        xs_sorted = _dispatch_call(pos_tiles, h2, zeros_sorted)
        y_sorted = _experts_call(l, tile_expert, n_used, xs_sorted, wg_b, wu_b, wd_b)
        x = _combine_call(l, pos_tiles, x1, rg, mod, y_sorted)

    y_prompt = x[:N_CTX_TOK].reshape(BATCH, SEQ, D_MODEL)
    y_sample = x[N_CTX_TOK:].reshape(DEC_BATCH, DEC_SEQ, D_MODEL)
    return (y_prompt, y_sample, jnp.stack(k_list, axis=1), jnp.stack(v_list, axis=1), jnp.stack(s_list, axis=1))
```

```python
import functools

import jax
import jax.numpy as jnp
from jax import lax
from jax.experimental import pallas as pl
from jax.experimental.pallas import tpu as pltpu

D_MODEL = 1024
BATCH = 32
SEQ = 256
DEPTH = 4
DEC_BATCH = 4
DEC_SEQ = 1024
PAST_LEN = 256
GRID_W = 64
N_HEADS = 8
N_KV = 2
HEAD_DIM = 64
WINDOW = 128
BLOCK = 128
ATT_SCALE = HEAD_DIM ** -0.5
ROPE_BASE = 10000.0
ROPE_PAIRS = HEAD_DIM // 4
DN_HEADS = 4
DK = 128
DV = 128
DN_CONV = 5
CHUNK = 64
N_DIR = 2
ATT_Q = N_HEADS * HEAD_DIM
ATT_KV = N_KV * HEAD_DIM
DN_QK = DN_HEADS * DK
DN_V = DN_HEADS * DV
DN_CONV_CH = 2 * DN_QK + DN_V
N_GATE_COLS = 2 * N_DIR * DN_HEADS
MAIN_COLS = ATT_Q + 2 * ATT_KV + DN_CONV_CH + DN_V
N_EXPERTS = 16
N_GROUPS = 4
EXPERTS_PER_GROUP = 4
TOP_K = 2
D_FF = 512
EPS = 1e-6

N_CTX_TOK = BATCH * SEQ
N_LAT_TOK = DEC_BATCH * DEC_SEQ
N_TOK = N_CTX_TOK + N_LAT_TOK
N_COND = 8

LANES = 128
TOK_TILE = 256
EXP_TILE = 256
N_EXP_TILES = (N_TOK * TOP_K) // EXP_TILE + N_EXPERTS
N_SORTED = N_EXP_TILES * EXP_TILE
VMEM_LIMIT = 56 * 1024 * 1024

F32 = jnp.float32
BF16 = jnp.bfloat16


def _cparams(*sem):
    return pltpu.CompilerParams(dimension_semantics=sem, vmem_limit_bytes=VMEM_LIMIT)


def _dot(a, b):
    return jnp.dot(a.astype(BF16), b.astype(BF16), preferred_element_type=F32)


def _dot_nt(a, b):
    return lax.dot_general(a.astype(BF16), b.astype(BF16), (((1,), (1,)), ((), ())), preferred_element_type=F32)


def _dot_tn(a, b):
    return lax.dot_general(a.astype(BF16), b.astype(BF16), (((0,), (0,)), ((), ())), preferred_element_type=F32)


def _dot_f32(a, b):
    return jnp.dot(a, b, preferred_element_type=F32, precision=lax.Precision.HIGHEST)


def _dot_split(mat01, x):
    hi = x.astype(BF16)
    lo = (x - hi.astype(F32)).astype(BF16)
    return (jnp.dot(mat01, hi, preferred_element_type=F32) + jnp.dot(mat01, lo, preferred_element_type=F32))


def _sigmoid(x):
    return 1.0 / (1.0 + jnp.exp(-x))


def _silu(x):
    return x * _sigmoid(x)


def _softplus(x):
    return jnp.maximum(x, 0.0) + jnp.log(1.0 + jnp.exp(-jnp.abs(x)))


MOD_TN = 1536


def _mod_kernel(cond_ref, w_ref, b_ref, o_ref):
    c = _silu(cond_ref[...])
    o_ref[...] = _dot(c, w_ref[...]) + b_ref[...]


def _modulation_call(cond, w_ada, b_ada):
    n_col = 6 * D_MODEL
    return pl.pallas_call(
        _mod_kernel,
        grid=(DEPTH, n_col // MOD_TN),
        in_specs=[
            pl.BlockSpec((N_COND, D_MODEL), lambda l, j: (0, 0)),
            pl.BlockSpec((None, D_MODEL, MOD_TN), lambda l, j: (l, 0, j)),
            pl.BlockSpec((None, 1, MOD_TN), lambda l, j: (l, 0, j)),
        ],
        out_specs=pl.BlockSpec((None, N_COND, MOD_TN), lambda l, j: (l, 0, j)),
        out_shape=jax.ShapeDtypeStruct((DEPTH, N_COND, n_col), F32),
        compiler_params=_cparams("arbitrary", "arbitrary"),
        name="modulation",
    )(cond, w_ada, b_ada.reshape(DEPTH, 1, n_col))


def _mod_row(i):
    n_ctx = N_CTX_TOK // TOK_TILE
    per_seq = DEC_SEQ // TOK_TILE
    return jnp.where(i < n_ctx, 0, 1 + (jnp.maximum(i - n_ctx, 0)) // per_seq)


def _adaln(x, gain, shift, scale):
    ms = jnp.mean(x * x, axis=-1, keepdims=True)
    return (x * lax.rsqrt(ms + EPS) * gain) * (1.0 + scale) + shift


def _seg_rms(x, bd, gain):
    x2 = x * x
    hi = x2.astype(BF16)
    lo = (x2 - hi.astype(F32)).astype(BF16)
    ms = jnp.dot(hi, bd, preferred_element_type=F32) + jnp.dot(lo, bd, preferred_element_type=F32)
    return x * lax.rsqrt(ms + EPS) * gain


def _rope(x, cos, sin_signed):
    w = x.shape[-1]
    lane = lax.broadcasted_iota(jnp.int32, x.shape, 1)
    first = (lane % (2 * ROPE_PAIRS)) < ROPE_PAIRS
    partner = jnp.where(first, pltpu.roll(x, w - ROPE_PAIRS, axis=1), pltpu.roll(x, ROPE_PAIRS, axis=1))
    return x * cos + partner * sin_signed


def _inproj_kernel(x_ref, mod_ref, gain_ref, w_ref, wg_ref, bd_ref, qg_ref, kg_ref, cos_ref, sin_ref,
                   q_ref, kv_ref, qkvd_ref, z_ref, ab_ref):
    i = pl.program_id(0)
    is_lat = i >= N_CTX_TOK // TOK_TILE
    shift = mod_ref[:, 0:D_MODEL]
    scale = mod_ref[:, D_MODEL:2 * D_MODEL]
    h = _adaln(x_ref[...], gain_ref[...], shift, scale).astype(BF16)

    qa = jnp.dot(h, w_ref[:, 0:ATT_Q], preferred_element_type=F32)
    qn = _seg_rms(qa, bd_ref[...], qg_ref[...])
    ka = jnp.dot(h, w_ref[:, ATT_Q:ATT_Q + ATT_KV], preferred_element_type=F32)
    kn = _seg_rms(ka, bd_ref[0:ATT_KV, 0:ATT_KV], kg_ref[...])

    @pl.when(is_lat)
    def _():
        q_ref[...] = (_rope(qn, cos_ref[...], sin_ref[...]) * ATT_SCALE).astype(BF16)
        kv_ref[:, 0:ATT_KV] = _rope(kn, cos_ref[:, 0:ATT_KV], sin_ref[:, 0:ATT_KV])

    @pl.when(jnp.logical_not(is_lat))
    def _():
        q_ref[...] = (qn * ATT_SCALE).astype(BF16)
        kv_ref[:, 0:ATT_KV] = kn

    c0 = ATT_Q + ATT_KV
    kv_ref[:, ATT_KV:2 * ATT_KV] = jnp.dot(h, w_ref[:, c0:c0 + ATT_KV], preferred_element_type=F32)
    c0 += ATT_KV
    for j in range(DN_CONV_CH // 512):
        qkvd_ref[:, j * 512:(j + 1) * 512] = jnp.dot(
            h, w_ref[:, c0 + j * 512:c0 + (j + 1) * 512], preferred_element_type=F32).astype(BF16)
    c0 += DN_CONV_CH
    z_ref[...] = jnp.dot(h, w_ref[:, c0:c0 + DN_V], preferred_element_type=F32).astype(BF16)
    ab_ref[...] = jnp.dot(h, wg_ref[...], preferred_element_type=F32)[:, 0:N_GATE_COLS]


def _inproj_call(l, x, mod, gain, w_main, w_gates, bd, qg, kg, cos, sin):
    n_ctx = N_CTX_TOK // TOK_TILE
    per_seq = DEC_SEQ // TOK_TILE

    def pos_map(i):
        return (jnp.maximum(i - n_ctx, 0) % per_seq, 0)

    tok = lambda w: pl.BlockSpec((TOK_TILE, w), lambda i: (i, 0))
    full = lambda a: pl.BlockSpec(a.shape, lambda i: (0,) * a.ndim)
    return pl.pallas_call(
        _inproj_kernel,
        grid=(N_TOK // TOK_TILE,),
        in_specs=[
            tok(D_MODEL),
            pl.BlockSpec((None, None, 1, 6 * D_MODEL), lambda i: (l, _mod_row(i), 0, 0)),
            pl.BlockSpec((None, 1, D_MODEL), lambda i: (l, 0, 0)),
            pl.BlockSpec((None, D_MODEL, MAIN_COLS), lambda i: (l, 0, 0)),
            pl.BlockSpec((None, D_MODEL, LANES), lambda i: (l, 0, 0)),
            full(bd),
            pl.BlockSpec((None, 1, ATT_Q), lambda i: (l, 0, 0)),
            pl.BlockSpec((None, 1, ATT_KV), lambda i: (l, 0, 0)),
            pl.BlockSpec((TOK_TILE, ATT_Q), pos_map),
            pl.BlockSpec((TOK_TILE, ATT_Q), pos_map),
        ],
        out_specs=[tok(ATT_Q), tok(2 * ATT_KV), tok(DN_CONV_CH), tok(DN_V), tok(N_GATE_COLS)],
        out_shape=[
            jax.ShapeDtypeStruct((N_TOK, ATT_Q), BF16),
            jax.ShapeDtypeStruct((N_TOK, 2 * ATT_KV), F32),
            jax.ShapeDtypeStruct((N_TOK, DN_CONV_CH), BF16),
            jax.ShapeDtypeStruct((N_TOK, DN_V), BF16),
            jax.ShapeDtypeStruct((N_TOK, N_GATE_COLS), F32),
        ],
        compiler_params=_cparams("arbitrary"),
        name="inproj",
    )(x, mod, gain, w_main, w_gates, bd, qg, kg, cos, sin)


def _dup_half(x, g):
    lane = lax.broadcasted_iota(jnp.int32, x.shape, 1)
    lo = lane < HEAD_DIM
    xr = pltpu.roll(x, HEAD_DIM, axis=1)
    return jnp.where(lo, x, xr) if g == 0 else jnp.where(lo, xr, x)


def _attend(sink_ref, q_ref, o_ref, key_sets):
    rows = q_ref.shape[0]
    lane = lax.broadcasted_iota(jnp.int32, (rows, LANES), 1)
    lo = lane < HEAD_DIM
    for g in range(N_KV):
        ks = [(_dup_half(k, g).astype(BF16), _dup_half(v, g).astype(BF16), valid) for k, v, valid in key_sets]
        for jj in range(2):
            j = g * 2 + jj
            qp = q_ref[:, j * LANES:(j + 1) * LANES]
            outs = []
            for e in range(2):
                qm = jnp.where(lo if e == 0 else jnp.logical_not(lo), qp, jnp.zeros_like(qp))
                sk = sink_ref[2 * j + e]
                scores = []
                m = jnp.full((rows, 1), sk, F32)
                for k, _, valid in ks:
                    s = _dot_nt(qm, k)
                    if valid is not None:
                        s = jnp.where(valid, s, -jnp.inf)
                    scores.append(s)
                    m = jnp.maximum(m, jnp.max(s, axis=-1, keepdims=True))
                den = jnp.exp(sk - m)
                acc = jnp.zeros((rows, LANES), F32)
                for s, (_, v, _) in zip(scores, ks):
                    p = jnp.exp(s - m)
                    den = den + jnp.sum(p, axis=-1, keepdims=True)
                    acc = acc + _dot(p, v)
                outs.append(acc / den)
            o_ref[:, j * LANES:(j + 1) * LANES] = jnp.where(lo, outs[0], outs[1]).astype(o_ref.dtype)


def _attn_ctx_kernel(sink_ref, q_ref, kv_ref, o_ref):
    _attend(sink_ref, q_ref, o_ref, [(kv_ref[:, 0:ATT_KV], kv_ref[:, ATT_KV:2 * ATT_KV], None)])


def _attn_lat_kernel(sink_ref, q_ref, kv_ref, ck_ref, cv_ref, o_ref):
    n = pl.program_id(1)
    span = 3 * BLOCK
    start = pl.multiple_of(jnp.clip((n - 1) * BLOCK, 0, DEC_SEQ - span), BLOCK)
    kw = kv_ref[pl.ds(start, span), 0:ATT_KV]
    vw = kv_ref[pl.ds(start, span), ATT_KV:2 * ATT_KV]
    qpos = n * BLOCK + lax.broadcasted_iota(jnp.int32, (BLOCK, span), 0)
    kpos = start + lax.broadcasted_iota(jnp.int32, (BLOCK, span), 1)
    valid = jnp.abs(kpos - qpos) <= WINDOW
    _attend(sink_ref, q_ref, o_ref, [(kw, vw, valid), (ck_ref[...], cv_ref[...], None)])


def _attn_ctx_call(sink_l, q, kv):
    return pl.pallas_call(
        _attn_ctx_kernel,
        grid=(BATCH,),
        in_specs=[
            pl.BlockSpec(memory_space=pltpu.SMEM),
            pl.BlockSpec((SEQ, ATT_Q), lambda b: (b, 0)),
            pl.BlockSpec((SEQ, 2 * ATT_KV), lambda b: (b, 0)),
        ],
        out_specs=pl.BlockSpec((SEQ, ATT_Q), lambda b: (b, 0)),
        out_shape=jax.ShapeDtypeStruct((N_CTX_TOK, ATT_Q), BF16),
        compiler_params=_cparams("arbitrary"),
        name="attn_ctx",
    )(sink_l, q, kv)


def _attn_lat_call(l, sink_l, q, kv, cache_k, cache_v):
    nb = DEC_SEQ // BLOCK
    q0 = N_CTX_TOK // BLOCK
    s0 = N_CTX_TOK // DEC_SEQ
    return pl.pallas_call(
        _attn_lat_kernel,
        grid=(DEC_BATCH, nb),
        in_specs=[
            pl.BlockSpec(memory_space=pltpu.SMEM),
            pl.BlockSpec((BLOCK, ATT_Q), lambda b, n: (q0 + b * nb + n, 0)),
            pl.BlockSpec((DEC_SEQ, 2 * ATT_KV), lambda b, n: (s0 + b, 0)),
            pl.BlockSpec((None, None, PAST_LEN, ATT_KV), lambda b, n: (b, l, 0, 0)),
            pl.BlockSpec((None, None, PAST_LEN, ATT_KV), lambda b, n: (b, l, 0, 0)),
        ],
        out_specs=pl.BlockSpec((BLOCK, ATT_Q), lambda b, n: (b * nb + n, 0)),
        out_shape=jax.ShapeDtypeStruct((N_LAT_TOK, ATT_Q), BF16),
        compiler_params=_cparams("arbitrary", "arbitrary"),
        name="attn_lat",
    )(sink_l, q, kv, cache_k, cache_v)


INV_PASSES = 1
GDN_UNROLL = 4


def _split_bf16(x):
    hi = x.astype(BF16)
    return hi, (x - hi.astype(F32)).astype(BF16)


def _block_diag(x, width):
    n = x.shape[1] // width
    blk = lax.broadcasted_iota(jnp.int32, x.shape, 1) // width
    zero = jnp.zeros_like(x)
    return jnp.concatenate([jnp.where(blk == h, x, zero) for h in range(n)], axis=0)


def _mm_heads(ts, xs, width):
    if INV_PASSES == 1:
        bds = [_block_diag(x.astype(BF16), width) for x in xs]
        return [jnp.dot(t.astype(BF16), bd, preferred_element_type=F32) for t, bd in zip(ts, bds)]
    tsp = [_split_bf16(t) for t in ts]
    xsp = [_split_bf16(x) for x in xs]
    bd_hi = [_block_diag(x_hi, width) for x_hi, _ in xsp]
    bd_lo = [_block_diag(x_lo, width) for _, x_lo in xsp]
    return [jnp.dot(t_hi, bh, preferred_element_type=F32) + jnp.dot(t_lo, bh, preferred_element_type=F32)
            + jnp.dot(t_hi, bl, preferred_element_type=F32) for (t_hi, t_lo), bh, bl in zip(tsp, bd_hi, bd_lo)]


def _unit_tri_inverse(mats, row, col):
    eye = (row == col).astype(F32)
    a8 = [jnp.where((row // 8) == (col // 8), a, 0.0) for a in mats]
    a8_2 = _mm_heads(a8, a8, CHUNK)
    a8_4 = _mm_heads(a8_2, a8_2, CHUNK)
    ts = [eye - a for a in a8]
    ts = [t + p for t, p in zip(ts, _mm_heads(ts, a8_2, CHUNK))]
    ts = [t + p for t, p in zip(ts, _mm_heads(ts, a8_4, CHUNK))]
    b = 8
    while b < CHUNK:
        level = ((row // (2 * b)) == (col // (2 * b))) & ((row // b) != (col // b))
        off = [jnp.where(level, a, 0.0) for a in mats]
        ts = [t - p for t, p in zip(ts, _mm_heads(_mm_heads(ts, off, CHUNK), ts, CHUNK))]
        b *= 2
    return ts


def _bcast_cols(x, first, width):
    rows = x.shape[0]
    if width == LANES:
        return jnp.concatenate([jnp.broadcast_to(x[:, first + h:first + h + 1], (rows, width)) for h in range(DN_HEADS)], axis=1)
    blk = lax.broadcasted_iota(jnp.int32, (rows, DN_HEADS * width), 1) // width
    out = jnp.broadcast_to(x[:, first:first + 1], (rows, DN_HEADS * width))
    for h in range(1, DN_HEADS):
        out = jnp.where(blk == h, jnp.broadcast_to(x[:, first + h:first + h + 1], (rows, DN_HEADS * width)), out)
    return out


def _gdn_kernel(*refs, seq_len, has_s0, write_state):
    qkvd_ref, z_ref, ab_ref, convw_ref, alog_ref, dtb_ref, ogain_ref = refs[:7]
    pos = 7
    s0_ref = None
    if has_s0:
        s0_ref = refs[pos]
        pos += 1
    dn_ref = refs[pos]
    pos += 1
    sfin_ref = None
    if write_state:
        sfin_ref = refs[pos]
        pos += 1
    qn_s, kn_s, vn_s, gb_s, u0_s, w_s, qd_s, kd_s, qk_s, gt_s, o_s, st_s = refs[pos:]
    L = seq_len
    n_chunks = L // CHUNK
    n_pairs = DN_HEADS // 2
    pair_w = 2 * DK

    trow = lax.broadcasted_iota(jnp.int32, (L, LANES), 0)
    for cb in range(DN_CONV_CH // LANES):
        cs = slice(cb * LANES, (cb + 1) * LANES)
        x = qkvd_ref[:, cs].astype(F32)
        acc = None
        for i in range(DN_CONV):
            sh = i - (DN_CONV - 1) // 2
            if sh == 0:
                term = x
            else:
                xr = pltpu.roll(x, (-sh) % L, axis=0)
                ok = (trow + sh >= 0) & (trow + sh < L)
                term = jnp.where(ok, xr, 0.0)
            term = term * convw_ref[i:i + 1, cs]
            acc = term if acc is None else acc + term
        y = _silu(acc)
        kind, head = divmod(cb, DN_HEADS)
        hs = slice(head * DK, (head + 1) * DK)
        if kind == 0:
            qn_s[:, hs] = y * lax.rsqrt(jnp.sum(y * y, axis=-1, keepdims=True) + EPS) * (DK ** -0.5)
        elif kind == 1:
            kn_s[:, hs] = y * lax.rsqrt(jnp.sum(y * y, axis=-1, keepdims=True) + EPS)
        else:
            vn_s[:, hs] = y

    ab = ab_ref[...]
    glane = lax.broadcasted_iota(jnp.int32, ab.shape, 1)
    g = -jnp.exp(alog_ref[...]) * _softplus(ab + dtb_ref[...])
    gb_s[...] = jnp.where(glane < N_DIR * DN_HEADS, g, _sigmoid(ab))

    for d in range(N_DIR):
        for p in range(n_pairs):
            idx = d * n_pairs + p
            st_s[idx] = jnp.zeros((pair_w, pair_w), F32)
            if has_s0:
                st_s[idx, 0:DK, 0:DV] = s0_ref[d, 2 * p]
                st_s[idx, DK:2 * DK, DV:2 * DV] = s0_ref[d, 2 * p + 1]

    row = lax.broadcasted_iota(jnp.int32, (CHUNK, DN_HEADS * CHUNK), 0)
    col = lax.broadcasted_iota(jnp.int32, (CHUNK, DN_HEADS * CHUNK), 1) % CHUNK
    before_incl = [col <= row, col >= row]
    before_strict = [col < row, col > row]
    after_strict01 = [m.astype(F32) for m in before_strict]
    r64 = lax.broadcasted_iota(jnp.int32, (CHUNK, CHUNK), 0)
    c64 = lax.broadcasted_iota(jnp.int32, (CHUNK, CHUNK), 1)
    tri01 = [(c64 <= r64).astype(BF16), (c64 >= r64).astype(BF16)]

    def prep_step(n, carry):
        a_mats, v_betas, kb_egcs, slots = [], [], [], []
        for j in range(GDN_UNROLL):
            c = n * GDN_UNROLL + j
            rs = pl.ds(pl.multiple_of(c * CHUNK, CHUNK), CHUNK)
            gbc = gb_s[rs, :]
            q = qn_s[rs, :]
            k = kn_s[rs, :]
            v = vn_s[rs, :]
            k_bd = _block_diag(k.astype(BF16), DK)
            for d in range(N_DIR):
                first = d * DN_HEADS
                gc_all = _dot_split(tri01[d], gbc)
                last = CHUNK - 1 if d == 0 else 0
                gc_last = gc_all[last:last + 1, :]
                gt_s[d, pl.ds(c, 1), :] = jnp.exp(gc_last)
                beta = _bcast_cols(gbc, N_DIR * DN_HEADS + first, DK)
                e_gc = _bcast_cols(jnp.exp(gc_all), first, DK)
                e_kd = _bcast_cols(jnp.exp(gc_last - gc_all), first, DK)
                kb = k * beta
                gdiff = _dot_split(tri01[d], _bcast_cols(gbc, first, CHUNK) * after_strict01[d])
                decay = jnp.where(before_incl[d], jnp.exp(gdiff), 0.0)
                kk_qk = _dot_nt(jnp.concatenate([kb, q], axis=0), k_bd)
                a_mats.append(jnp.where(before_strict[d], kk_qk[0:CHUNK] * decay, 0.0))
                qk_s[d, rs, :] = jnp.where(before_incl[d], kk_qk[CHUNK:2 * CHUNK] * decay, 0.0).astype(BF16)
                qd_s[d, rs, :] = (q * e_gc).astype(BF16)
                kd_s[d, rs, :] = (k * e_kd).astype(BF16)
                v_betas.append(v * beta)
                kb_egcs.append(kb * e_gc)
                slots.append((d, rs))
        t_invs = _unit_tri_inverse(a_mats, row, col)
        for (d, rs), u0, w in zip(slots, _mm_heads(t_invs, v_betas, DV), _mm_heads(t_invs, kb_egcs, DK)):
            u0_s[d, rs, :] = u0
            w_s[d, rs, :] = w.astype(BF16)
        return carry

    lax.fori_loop(0, n_chunks // GDN_UNROLL, prep_step, 0)

    plane = lax.broadcasted_iota(jnp.int32, (1, pair_w), 1)
    srow = lax.broadcasted_iota(jnp.int32, (pair_w, pair_w), 0) // DK
    scol = lax.broadcasted_iota(jnp.int32, (pair_w, pair_w), 1) // DV
    same_head = srow == scol

    def scan_step(n, carry):
        probs = []
        for d in range(N_DIR):
            c = n if d == 0 else n_chunks - 1 - n
            rs = pl.ds(pl.multiple_of(c * CHUNK, CHUNK), CHUNK)
            gt = gt_s[d, pl.ds(c, 1), :]
            for p in range(n_pairs):
                c0 = d * DN_HEADS + 2 * p
                g_tot = jnp.where(plane < DV, gt[:, c0:c0 + 1], gt[:, c0 + 1:c0 + 2])
                probs.append((d, rs, p, d * n_pairs + p, slice(p * pair_w, (p + 1) * pair_w), g_tot))
        s_prev = [st_s[idx] for _, _, _, idx, _, _ in probs]
        s_b = [s.astype(BF16) for s in s_prev]
        ws = [jnp.dot(w_s[d, rs, ps], sb, preferred_element_type=F32) for (d, rs, _, _, ps, _), sb in zip(probs, s_b)]
        u_b = [(u0_s[d, rs, ps] - w).astype(BF16) for (d, rs, _, _, ps, _), w in zip(probs, ws)]
        upd = [lax.dot_general(kd_s[d, rs, ps], u, (((0,), (0,)), ((), ())), preferred_element_type=F32)
               for (d, rs, _, _, ps, _), u in zip(probs, u_b)]
        for (_, _, _, idx, _, g_tot), s, up in zip(probs, s_prev, upd):
            st_s[idx] = s * g_tot + jnp.where(same_head, up, 0.0)
        for (d, rs, p, _, ps, _), sb, u in zip(probs, s_b, u_b):
            o_s[d, rs, ps] = (jnp.dot(qd_s[d, rs, ps], sb, preferred_element_type=F32)
                              + jnp.dot(qk_s[d, rs, p * 2 * CHUNK:(p + 1) * 2 * CHUNK], _block_diag(u, DV),
                                        preferred_element_type=F32))
        return carry

    lax.fori_loop(0, n_chunks, scan_step, 0)

    for h in range(DN_HEADS):
        hs = slice(h * DV, (h + 1) * DV)
        o = o_s[0, :, hs] + o_s[1, :, hs]
        y = o * lax.rsqrt(jnp.mean(o * o, axis=-1, keepdims=True) + EPS) * ogain_ref[...]
        dn_ref[:, hs] = (y * _silu(z_ref[:, hs].astype(F32))).astype(dn_ref.dtype)
    if write_state:
        for d in range(N_DIR):
            for p in range(n_pairs):
                idx = d * n_pairs + p
                sfin_ref[d, 2 * p] = st_s[idx, 0:DK, 0:DV]
                sfin_ref[d, 2 * p + 1] = st_s[idx, DK:2 * DK, DV:2 * DV]


def _gdn_call(l, qkvd, z, ab, conv_w, alog16, dtb16, o_gain, state_dn, *, latent):
    L = DEC_SEQ if latent else SEQ
    n_seq = DEC_BATCH if latent else BATCH
    base = N_CTX_TOK // L if latent else 0
    seq = lambda w: pl.BlockSpec((L, w), lambda b: (base + b, 0))
    in_specs = [
        seq(DN_CONV_CH), seq(DN_V), seq(N_GATE_COLS),
        pl.BlockSpec((None, DN_CONV, DN_CONV_CH), lambda b: (l, 0, 0)),
        pl.BlockSpec((None, 1, N_GATE_COLS), lambda b: (l, 0, 0)),
        pl.BlockSpec((None, 1, N_GATE_COLS), lambda b: (l, 0, 0)),
        pl.BlockSpec((None, 1, DV), lambda b: (l, 0, 0)),
    ]
    args = [qkvd, z, ab, conv_w, alog16, dtb16, o_gain]
    out_specs = [pl.BlockSpec((L, DN_V), lambda b: (b, 0))]
    out_shape = [jax.ShapeDtypeStruct((n_seq * L, DN_V), BF16)]
    if latent:
        in_specs.append(pl.BlockSpec((None, None, N_DIR, DN_HEADS, DK, DV), lambda b: (b, l, 0, 0, 0, 0)))
        args.append(state_dn)
    else:
        out_specs.append(pl.BlockSpec((None, N_DIR, DN_HEADS, DK, DV), lambda b: (b, 0, 0, 0, 0)))
        out_shape.append(jax.ShapeDtypeStruct((n_seq, N_DIR, DN_HEADS, DK, DV), F32))
    n_chunks = L // CHUNK
    return pl.pallas_call(
        functools.partial(_gdn_kernel, seq_len=L, has_s0=latent, write_state=not latent),
        grid=(n_seq,),
        in_specs=in_specs,
        out_specs=out_specs,
        out_shape=out_shape,
        scratch_shapes=[
            pltpu.VMEM((L, DN_QK), F32), pltpu.VMEM((L, DN_QK), F32), pltpu.VMEM((L, DN_V), F32),
            pltpu.VMEM((L, N_GATE_COLS), F32),
            pltpu.VMEM((N_DIR, L, DN_V), F32),
            pltpu.VMEM((N_DIR, L, DN_QK), BF16),
            pltpu.VMEM((N_DIR, L, DN_QK), BF16),
            pltpu.VMEM((N_DIR, L, DN_QK), BF16),
            pltpu.VMEM((N_DIR, L, DN_HEADS * CHUNK), BF16),
            pltpu.VMEM((N_DIR, n_chunks, N_GATE_COLS), F32),
            pltpu.VMEM((N_DIR, L, DN_V), F32),
            pltpu.VMEM((N_DIR * (DN_HEADS // 2), 2 * DK, 2 * DV), F32),
        ],
        compiler_params=_cparams("arbitrary"),
        name="gdn_lat" if latent else "gdn_ctx",
    )(*args)


def _top2_of4(vals):
    m1 = jnp.maximum(jnp.maximum(vals[0], vals[1]), jnp.maximum(vals[2], vals[3]))
    i1 = jnp.where(vals[0] == m1, 0, jnp.where(vals[1] == m1, 1, jnp.where(vals[2] == m1, 2, 3)))
    rest = [jnp.where(i1 == j, -jnp.inf, vals[j]) for j in range(4)]
    m2 = jnp.maximum(jnp.maximum(rest[0], rest[1]), jnp.maximum(rest[2], rest[3]))
    i2 = jnp.where(rest[0] == m2, 0, jnp.where(rest[1] == m2, 1, jnp.where(rest[2] == m2, 2, 3)))
    return m1, i1, m2, i2


def _outproj_kernel(x_ref, att_ref, dn_ref, mod_ref, gain_ref, wo_ref, rw_ref, rb_ref, lt_ref,
                    x1_ref, h2_ref, ri_ref, rg_ref, cnt_ref, carry_s):
    i = pl.program_id(0)

    @pl.when(i == 0)
    def _():
        carry_s[...] = jnp.zeros_like(carry_s)

    gate1 = mod_ref[:, 2 * D_MODEL:3 * D_MODEL]
    shift2 = mod_ref[:, 3 * D_MODEL:4 * D_MODEL]
    scale2 = mod_ref[:, 4 * D_MODEL:5 * D_MODEL]
    mix = (jnp.dot(att_ref[...], wo_ref[0:ATT_Q, :], preferred_element_type=F32)
           + jnp.dot(dn_ref[...], wo_ref[ATT_Q:ATT_Q + DN_V, :], preferred_element_type=F32))
    x1 = x_ref[...] + gate1 * mix
    x1_ref[...] = x1
    h2 = _adaln(x1, gain_ref[...], shift2, scale2)
    h2_ref[...] = h2

    scores = _sigmoid(_dot_f32(h2, rw_ref[...]))
    biased = scores + rb_ref[...]
    cols = [biased[:, e:e + 1] for e in range(N_EXPERTS)]
    per_group = [_top2_of4(cols[g * 4:(g + 1) * 4]) for g in range(N_GROUPS)]
    gs = [pg[0] + pg[2] for pg in per_group]
    _, gsel, _, _ = _top2_of4(gs)
    pick = lambda k: jnp.where(gsel == 0, per_group[0][k], jnp.where(gsel == 1, per_group[1][k],
                               jnp.where(gsel == 2, per_group[2][k], per_group[3][k])))
    e1 = gsel * EXPERTS_PER_GROUP + pick(1)
    e2 = gsel * EXPERTS_PER_GROUP + pick(3)
    elane = lax.broadcasted_iota(jnp.int32, scores.shape, 1)
    oh1 = elane == e1
    oh2 = elane == e2
    s1 = jnp.sum(jnp.where(oh1, scores, 0.0), axis=-1, keepdims=True)
    s2 = jnp.sum(jnp.where(oh2, scores, 0.0), axis=-1, keepdims=True)
    tot = s1 + s2
    glane = lax.broadcasted_iota(jnp.int32, rg_ref.shape, 1)
    rg_ref[...] = jnp.where(glane == 0, s1 / tot, s2 / tot)

    sel = (oh1 | oh2).astype(BF16)
    before = jnp.dot(lt_ref[...], sel, preferred_element_type=F32) + carry_s[...]
    r1 = jnp.sum(jnp.where(oh1, before, 0.0), axis=-1, keepdims=True)
    r2 = jnp.sum(jnp.where(oh2, before, 0.0), axis=-1, keepdims=True)
    carry_s[...] = carry_s[...] + jnp.sum(sel.astype(F32), axis=0, keepdims=True)
    cnt_ref[...] = carry_s[...].astype(jnp.int32)
    ilane = lax.broadcasted_iota(jnp.int32, ri_ref.shape, 1)
    ri_ref[...] = jnp.where(ilane == 0, e1, jnp.where(ilane == 1, e2, jnp.where(
        ilane == 2, r1.astype(jnp.int32), r2.astype(jnp.int32))))


def _outproj_call(l, x, att, dn, mod, gain, w_out, router_w, router_b, lt):
    tok = lambda w: pl.BlockSpec((TOK_TILE, w), lambda i: (i, 0))
    full = lambda a: pl.BlockSpec(a.shape, lambda i: (0,) * a.ndim)
    return pl.pallas_call(
        _outproj_kernel,
        grid=(N_TOK // TOK_TILE,),
        in_specs=[
            tok(D_MODEL), tok(ATT_Q), tok(DN_V),
            pl.BlockSpec((None, None, 1, 6 * D_MODEL), lambda i: (l, _mod_row(i), 0, 0)),
            pl.BlockSpec((None, 1, D_MODEL), lambda i: (l, 0, 0)),
            pl.BlockSpec((None, ATT_Q + DN_V, D_MODEL), lambda i: (l, 0, 0)),
            full(router_w), full(router_b), full(lt),
        ],
        out_specs=[tok(D_MODEL), tok(D_MODEL), tok(4), tok(2), pl.BlockSpec((1, N_EXPERTS), lambda i: (0, 0))],
        out_shape=[
            jax.ShapeDtypeStruct((N_TOK, D_MODEL), F32),
            jax.ShapeDtypeStruct((N_TOK, D_MODEL), F32),
            jax.ShapeDtypeStruct((N_TOK, 4), jnp.int32),
            jax.ShapeDtypeStruct((N_TOK, 2), F32),
            jax.ShapeDtypeStruct((1, N_EXPERTS), jnp.int32),
        ],
        scratch_shapes=[pltpu.VMEM((1, N_EXPERTS), F32)],
        compiler_params=_cparams("arbitrary"),
        name="outproj_router",
    )(x, att, dn, mod, gain, w_out, router_w, router_b, lt)


def _dispatch_kernel(pos_ref, h2_ref, buf_in_ref, sorted_ref, sem):
    del buf_in_ref

    def row_copy(r, k):
        return pltpu.make_async_copy(h2_ref.at[pl.ds(r, 1), :], sorted_ref.at[pl.ds(pos_ref[0, k * TOK_TILE + r], 1), :], sem)

    def start(r, c):
        row_copy(r, 0).start()
        row_copy(r, 1).start()
        return c

    def wait(r, c):
        row_copy(r, 0).wait()
        row_copy(r, 1).wait()
        return c

    lax.fori_loop(0, TOK_TILE, start, 0)
    lax.fori_loop(0, TOK_TILE, wait, 0)


def _dispatch_call(pos_tiles, h2, zeros_sorted):
    return pl.pallas_call(
        _dispatch_kernel,
        grid=(N_TOK // TOK_TILE,),
        in_specs=[
            pl.BlockSpec((None, 1, TOP_K * TOK_TILE), lambda i: (i, 0, 0), memory_space=pltpu.SMEM),
            pl.BlockSpec((TOK_TILE, D_MODEL), lambda i: (i, 0)),
            pl.BlockSpec(memory_space=pl.ANY),
        ],
        out_specs=pl.BlockSpec(memory_space=pl.ANY),
        out_shape=jax.ShapeDtypeStruct((N_SORTED, D_MODEL), F32),
        scratch_shapes=[pltpu.SemaphoreType.DMA(())],
        input_output_aliases={2: 0},
        compiler_params=_cparams("arbitrary"),
        name="moe_dispatch",
    )(pos_tiles, h2, zeros_sorted)


def _experts_kernel(te_ref, nu_ref, x_ref, wg_ref, wu_ref, wd_ref, y_ref):
    used = pl.program_id(0) < nu_ref[0]

    @pl.when(used)
    def _():
        x = x_ref[...].astype(BF16)
        hid = _silu(jnp.dot(x, wg_ref[...], preferred_element_type=F32)) * jnp.dot(x, wu_ref[...], preferred_element_type=F32)
        y_ref[...] = jnp.dot(hid.astype(BF16), wd_ref[...], preferred_element_type=F32)

    @pl.when(jnp.logical_not(used))
    def _():
        y_ref[...] = jnp.zeros_like(y_ref)


def _experts_call(l, tile_expert, n_used, xs, w_gate, w_up, w_down):
    row = lambda j, te, nu: (jnp.minimum(j, nu[0] - 1), 0)
    grid_spec = pltpu.PrefetchScalarGridSpec(
        num_scalar_prefetch=2,
        grid=(N_EXP_TILES,),
        in_specs=[
            pl.BlockSpec((EXP_TILE, D_MODEL), row),
            pl.BlockSpec((None, None, D_MODEL, D_FF), lambda j, te, nu: (l, te[j], 0, 0)),
            pl.BlockSpec((None, None, D_MODEL, D_FF), lambda j, te, nu: (l, te[j], 0, 0)),
            pl.BlockSpec((None, None, D_FF, D_MODEL), lambda j, te, nu: (l, te[j], 0, 0)),
        ],
        out_specs=pl.BlockSpec((EXP_TILE, D_MODEL), lambda j, te, nu: (j, 0)),
    )
    return pl.pallas_call(
        _experts_kernel,
        grid_spec=grid_spec,
        out_shape=jax.ShapeDtypeStruct((N_SORTED, D_MODEL), F32),
        compiler_params=_cparams("arbitrary"),
        name="moe_experts",
    )(tile_expert, n_used, xs, w_gate, w_up, w_down)


def _combine_kernel(pos_ref, x1_ref, rg_ref, mod_ref, y_ref, out_ref, buf_s, sem):
    def row_copy(r, k):
        return pltpu.make_async_copy(y_ref.at[pl.ds(pos_ref[0, k * TOK_TILE + r], 1), :], buf_s.at[k, pl.ds(r, 1), :], sem)

    def start(r, c):
        row_copy(r, 0).start()
        row_copy(r, 1).start()
        return c

    def wait(r, c):
        row_copy(r, 0).wait()
        row_copy(r, 1).wait()
        return c

    lax.fori_loop(0, TOK_TILE, start, 0)
    lax.fori_loop(0, TOK_TILE, wait, 0)
    gate2 = mod_ref[:, 5 * D_MODEL:6 * D_MODEL]
    moe = buf_s[0] * rg_ref[:, 0:1] + buf_s[1] * rg_ref[:, 1:2]
    out_ref[...] = x1_ref[...] + gate2 * moe


def _combine_call(l, pos_tiles, x1, rg, mod, y_sorted):
    tok = lambda w: pl.BlockSpec((TOK_TILE, w), lambda i: (i, 0))
    return pl.pallas_call(
        _combine_kernel,
        grid=(N_TOK // TOK_TILE,),
        in_specs=[
            pl.BlockSpec((None, 1, TOP_K * TOK_TILE), lambda i: (i, 0, 0), memory_space=pltpu.SMEM),
            tok(D_MODEL), tok(2),
            pl.BlockSpec((None, None, 1, 6 * D_MODEL), lambda i: (l, _mod_row(i), 0, 0)),
            pl.BlockSpec(memory_space=pl.ANY),
        ],
        out_specs=tok(D_MODEL),
        out_shape=jax.ShapeDtypeStruct((N_TOK, D_MODEL), F32),
        scratch_shapes=[pltpu.VMEM((TOP_K, TOK_TILE, D_MODEL), F32), pltpu.SemaphoreType.DMA(())],
        compiler_params=_cparams("arbitrary"),
        name="moe_combine",
    )(pos_tiles, x1, rg, mod, y_sorted)


def _rope_tables():
    pos = jnp.arange(DEC_SEQ)
    r = (pos // GRID_W).astype(F32)
    c = (pos % GRID_W).astype(F32)
    inv = ROPE_BASE ** (-jnp.arange(ROPE_PAIRS, dtype=F32) / ROPE_PAIRS)
    ar, ac = r[:, None] * inv, c[:, None] * inv
    cos = jnp.concatenate([jnp.cos(ar), jnp.cos(ar), jnp.cos(ac), jnp.cos(ac)], axis=-1)
    sin = jnp.concatenate([-jnp.sin(ar), jnp.sin(ar), -jnp.sin(ac), jnp.sin(ac)], axis=-1)
    return jnp.tile(cos, (1, N_HEADS)), jnp.tile(sin, (1, N_HEADS))


def _routing_layout(ri, counts):
    counts = counts.reshape(N_EXPERTS)
    padded = ((counts + EXP_TILE - 1) // EXP_TILE) * EXP_TILE
    ends = jnp.cumsum(padded)
    offs = ends - padded
    pos = offs[ri[:, 0:2]] + ri[:, 2:4]
    n_used = (ends[-1] // EXP_TILE).astype(jnp.int32)
    tile_start = jnp.arange(N_EXP_TILES, dtype=jnp.int32) * EXP_TILE
    tile_expert = jnp.minimum(jnp.searchsorted(ends, tile_start, side="right"), N_EXPERTS - 1).astype(jnp.int32)
    last = tile_expert[jnp.maximum(n_used - 1, 0)]
    tile_expert = jnp.where(tile_start < ends[-1], tile_expert, last)
    pos_tiles = pos.reshape(N_TOK // TOK_TILE, TOK_TILE, TOP_K).transpose(0, 2, 1).reshape(
        N_TOK // TOK_TILE, 1, TOP_K * TOK_TILE).astype(jnp.int32)
    return pos_tiles, tile_expert, n_used.reshape(1)


def kernel(x_prompt, x_sample, cache_k, cache_v, state_dn, c, c_ctx, w_ada, b_ada, norm_attn, norm_ffn,
           w_in, conv_w, a_log, dt_bias, q_norm, k_norm, sink, o_norm, w_out, router_w, router_bias,
           w_gate, w_up, w_down):
    x = jnp.concatenate([x_prompt.reshape(N_CTX_TOK, D_MODEL), x_sample.reshape(N_LAT_TOK, D_MODEL)], axis=0)
    cond = jnp.concatenate([c_ctx[None, :], c, jnp.zeros((N_COND - 1 - DEC_BATCH, D_MODEL), F32)], axis=0)
    mod = _modulation_call(cond, w_ada, b_ada).reshape(DEPTH, N_COND, 1, 6 * D_MODEL)

    w_main = w_in[:, :, :MAIN_COLS].astype(BF16)
    w_gates = jnp.pad(w_in[:, :, MAIN_COLS:], ((0, 0), (0, 0), (0, LANES - N_GATE_COLS))).astype(BF16)
    w_out_b = w_out.astype(BF16)
    wg_b, wu_b, wd_b = w_gate.astype(BF16), w_up.astype(BF16), w_down.astype(BF16)
    seg = jnp.arange(ATT_Q) // HEAD_DIM
    bd = jnp.where(seg[:, None] == seg[None, :], 1.0 / HEAD_DIM, 0.0).astype(BF16)
    qg = jnp.tile(q_norm, (1, N_HEADS)).reshape(DEPTH, 1, ATT_Q)
    kg = jnp.tile(k_norm, (1, N_KV)).reshape(DEPTH, 1, ATT_KV)
    cos, sin = _rope_tables()
    gain1 = norm_attn.reshape(DEPTH, 1, D_MODEL)
    gain2 = norm_ffn.reshape(DEPTH, 1, D_MODEL)
    pad8 = lambda a: jnp.pad(a.reshape(DEPTH, 1, N_DIR * DN_HEADS), ((0, 0), (0, 0), (0, N_GATE_COLS - N_DIR * DN_HEADS)))
    alog16, dtb16 = pad8(a_log), pad8(dt_bias)
    o_gain = o_norm.reshape(DEPTH, 1, DV)
    ck = cache_k.reshape(DEC_BATCH, DEPTH, PAST_LEN, ATT_KV)
    cv = cache_v.reshape(DEC_BATCH, DEPTH, PAST_LEN, ATT_KV)
    rb = router_bias.reshape(1, N_EXPERTS)
    tri = jnp.arange(TOK_TILE)
    lt = (tri[None, :] < tri[:, None]).astype(BF16)
    zeros_sorted = jnp.zeros((N_SORTED, D_MODEL), F32)

    k_list, v_list, s_list = [], [], []
    for l in range(DEPTH):
        q, kv, qkvd, z, ab = _inproj_call(l, x, mod, gain1, w_main, w_gates, bd, qg, kg, cos, sin)
        k_list.append(kv[:N_CTX_TOK, 0:ATT_KV].reshape(BATCH, SEQ, N_KV, HEAD_DIM))
        v_list.append(kv[:N_CTX_TOK, ATT_KV:].reshape(BATCH, SEQ, N_KV, HEAD_DIM))
        att_c = _attn_ctx_call(sink[l], q, kv)
        att_l = _attn_lat_call(l, sink[l], q, kv, ck, cv)
        dn_c, s_c = _gdn_call(l, qkvd, z, ab, conv_w, alog16, dtb16, o_gain, state_dn, latent=False)
        (dn_l,) = _gdn_call(l, qkvd, z, ab, conv_w, alog16, dtb16, o_gain, state_dn, latent=True)
        s_list.append(s_c)
        att = jnp.concatenate([att_c, att_l], axis=0)
        dn = jnp.concatenate([dn_c, dn_l], axis=0)
        x1, h2, ri, rg, counts = _outproj_call(l, x, att, dn, mod, gain2, w_out_b, router_w, rb, lt)
        pos_tiles, tile_expert, n_used = _routing_layout(ri, counts)
        xs_sorted = _dispatch_call(pos_tiles, h2, zeros_sorted)
        y_sorted = _experts_call(l, tile_expert, n_used, xs_sorted, wg_b, wu_b, wd_b)
        x = _combine_call(l, pos_tiles, x1, rg, mod, y_sorted)

    y_prompt = x[:N_CTX_TOK].reshape(BATCH, SEQ, D_MODEL)
    y_sample = x[N_CTX_TOK:].reshape(DEC_BATCH, DEC_SEQ, D_MODEL)
    return (y_prompt, y_sample, jnp.stack(k_list, axis=1), jnp.stack(v_list, axis=1), jnp.stack(s_list, axis=1))
```

```python
import functools

import jax
import jax.numpy as jnp
from jax import lax
from jax.experimental import pallas as pl
from jax.experimental.pallas import tpu as pltpu

D_MODEL = 1024
BATCH = 32
SEQ = 256
DEPTH = 4
DEC_BATCH = 4
DEC_SEQ = 1024
PAST_LEN = 256
GRID_W = 64
N_HEADS = 8
N_KV = 2
HEAD_DIM = 64
WINDOW = 128
BLOCK = 128
ATT_SCALE = HEAD_DIM ** -0.5
ROPE_BASE = 10000.0
ROPE_PAIRS = HEAD_DIM // 4
DN_HEADS = 4
DK = 128
DV = 128
DN_CONV = 5
CHUNK = 64
N_DIR = 2
ATT_Q = N_HEADS * HEAD_DIM
ATT_KV = N_KV * HEAD_DIM
DN_QK = DN_HEADS * DK
DN_V = DN_HEADS * DV
DN_CONV_CH = 2 * DN_QK + DN_V
N_GATE_COLS = 2 * N_DIR * DN_HEADS
IN_DIM = ATT_Q + 2 * ATT_KV + DN_CONV_CH + DN_V + N_GATE_COLS
N_EXPERTS = 16
N_GROUPS = 4
EXPERTS_PER_GROUP = 4
TOP_K = 2
D_FF = 512
EPS = 1e-6

N_CTX_TOK = BATCH * SEQ
N_LAT_TOK = DEC_BATCH * DEC_SEQ
N_TOK = N_CTX_TOK + N_LAT_TOK
N_COND = 8

LANES = 128
TOK_TILE = 256
EXP_TILE = 256
N_EXP_TILES = (N_TOK * TOP_K) // EXP_TILE + N_EXPERTS
N_SORTED = N_EXP_TILES * EXP_TILE
VMEM_LIMIT = 56 * 1024 * 1024

F32 = jnp.float32
BF16 = jnp.bfloat16


def _cparams(*sem):
    return pltpu.CompilerParams(dimension_semantics=sem, vmem_limit_bytes=VMEM_LIMIT)


def _dot(a, b):
    return jnp.dot(a.astype(BF16), b.astype(BF16), preferred_element_type=F32)


def _dot_nt(a, b):
    return lax.dot_general(a.astype(BF16), b.astype(BF16), (((1,), (1,)), ((), ())), preferred_element_type=F32)


def _dot_tn(a, b):
    return lax.dot_general(a.astype(BF16), b.astype(BF16), (((0,), (0,)), ((), ())), preferred_element_type=F32)


def _dot_f32(a, b):
    return jnp.dot(a, b, preferred_element_type=F32, precision=lax.Precision.HIGHEST)


def _dot_split(mat01, x):
    hi = x.astype(BF16)
    lo = (x - hi.astype(F32)).astype(BF16)
    return (jnp.dot(mat01, hi, preferred_element_type=F32) + jnp.dot(mat01, lo, preferred_element_type=F32))


def _sigmoid(x):
    return 1.0 / (1.0 + jnp.exp(-x))


def _silu(x):
    return x * _sigmoid(x)


def _softplus(x):
    return jnp.maximum(x, 0.0) + jnp.log(1.0 + jnp.exp(-jnp.abs(x)))


MOD_TN = 1536


def _mod_kernel(cond_ref, w_ref, b_ref, o_ref):
    c = _silu(cond_ref[...])
    o_ref[...] = _dot(c, w_ref[...]) + b_ref[...]


def _modulation_call(cond, w_ada, b_ada):
    n_col = 6 * D_MODEL
    return pl.pallas_call(
        _mod_kernel,
        grid=(DEPTH, n_col // MOD_TN),
        in_specs=[
            pl.BlockSpec((N_COND, D_MODEL), lambda l, j: (0, 0)),
            pl.BlockSpec((None, D_MODEL, MOD_TN), lambda l, j: (l, 0, j)),
            pl.BlockSpec((None, 1, MOD_TN), lambda l, j: (l, 0, j)),
        ],
        out_specs=pl.BlockSpec((None, N_COND, MOD_TN), lambda l, j: (l, 0, j)),
        out_shape=jax.ShapeDtypeStruct((DEPTH, N_COND, n_col), F32),
        compiler_params=_cparams("arbitrary", "arbitrary"),
        name="modulation",
    )(cond, w_ada, b_ada.reshape(DEPTH, 1, n_col))


def _mod_row(i):
    n_ctx = N_CTX_TOK // TOK_TILE
    per_seq = DEC_SEQ // TOK_TILE
    return jnp.where(i < n_ctx, 0, 1 + (jnp.maximum(i - n_ctx, 0)) // per_seq)


def _adaln(x, gain, shift, scale):
    ms = jnp.mean(x * x, axis=-1, keepdims=True)
    return (x * lax.rsqrt(ms + EPS) * gain) * (1.0 + scale) + shift


def _seg_rms(x, bd, gain):
    x2 = x * x
    hi = x2.astype(BF16)
    lo = (x2 - hi.astype(F32)).astype(BF16)
    ms = jnp.dot(hi, bd, preferred_element_type=F32) + jnp.dot(lo, bd, preferred_element_type=F32)
    return x * lax.rsqrt(ms + EPS) * gain


def _rope(x, cos, sin_signed):
    w = x.shape[-1]
    lane = lax.broadcasted_iota(jnp.int32, x.shape, 1)
    first = (lane % (2 * ROPE_PAIRS)) < ROPE_PAIRS
    partner = jnp.where(first, pltpu.roll(x, w - ROPE_PAIRS, axis=1), pltpu.roll(x, ROPE_PAIRS, axis=1))
    return x * cos + partner * sin_signed


def _inproj_kernel(x_ref, mod_ref, gain_ref, w_ref, bd_ref, qg_ref, kg_ref, cos_ref, sin_ref,
                   q_ref, kv_ref, qkvd_ref, z_ref, ab_ref):
    i = pl.program_id(0)
    is_lat = i >= N_CTX_TOK // TOK_TILE
    shift = mod_ref[:, 0:D_MODEL]
    scale = mod_ref[:, D_MODEL:2 * D_MODEL]
    h = _adaln(x_ref[...], gain_ref[...], shift, scale).astype(BF16)

    qa = jnp.dot(h, w_ref[:, 0:ATT_Q], preferred_element_type=F32)
    qn = _seg_rms(qa, bd_ref[...], qg_ref[...])
    ka = jnp.dot(h, w_ref[:, ATT_Q:ATT_Q + ATT_KV], preferred_element_type=F32)
    kn = _seg_rms(ka, bd_ref[0:ATT_KV, 0:ATT_KV], kg_ref[...])

    @pl.when(is_lat)
    def _():
        q_ref[...] = (_rope(qn, cos_ref[...], sin_ref[...]) * ATT_SCALE).astype(BF16)
        kv_ref[:, 0:ATT_KV] = _rope(kn, cos_ref[:, 0:ATT_KV], sin_ref[:, 0:ATT_KV])

    @pl.when(jnp.logical_not(is_lat))
    def _():
        q_ref[...] = (qn * ATT_SCALE).astype(BF16)
        kv_ref[:, 0:ATT_KV] = kn

    c0 = ATT_Q + ATT_KV
    kv_ref[:, ATT_KV:2 * ATT_KV] = jnp.dot(h, w_ref[:, c0:c0 + ATT_KV], preferred_element_type=F32)
    c0 += ATT_KV
    for j in range(DN_CONV_CH // 512):
        qkvd_ref[:, j * 512:(j + 1) * 512] = jnp.dot(
            h, w_ref[:, c0 + j * 512:c0 + (j + 1) * 512], preferred_element_type=F32).astype(BF16)
    c0 += DN_CONV_CH
    z_ref[...] = jnp.dot(h, w_ref[:, c0:c0 + DN_V], preferred_element_type=F32).astype(BF16)
    c0 += DN_V
    ab_ref[...] = jnp.dot(h, w_ref[:, c0:c0 + N_GATE_COLS], preferred_element_type=F32)


def _inproj_call(l, x, mod, gain, w_in, bd, qg, kg, cos, sin):
    n_ctx = N_CTX_TOK // TOK_TILE
    per_seq = DEC_SEQ // TOK_TILE

    def pos_map(i):
        return (jnp.maximum(i - n_ctx, 0) % per_seq, 0)

    tok = lambda w: pl.BlockSpec((TOK_TILE, w), lambda i: (i, 0))
    full = lambda a: pl.BlockSpec(a.shape, lambda i: (0,) * a.ndim)
    return pl.pallas_call(
        _inproj_kernel,
        grid=(N_TOK // TOK_TILE,),
        in_specs=[
            tok(D_MODEL),
            pl.BlockSpec((None, None, 1, 6 * D_MODEL), lambda i: (l, _mod_row(i), 0, 0)),
            pl.BlockSpec((None, 1, D_MODEL), lambda i: (l, 0, 0)),
            pl.BlockSpec((None, D_MODEL, IN_DIM), lambda i: (l, 0, 0)),
            full(bd),
            pl.BlockSpec((None, 1, ATT_Q), lambda i: (l, 0, 0)),
            pl.BlockSpec((None, 1, ATT_KV), lambda i: (l, 0, 0)),
            pl.BlockSpec((TOK_TILE, ATT_Q), pos_map),
            pl.BlockSpec((TOK_TILE, ATT_Q), pos_map),
        ],
        out_specs=[tok(ATT_Q), tok(2 * ATT_KV), tok(DN_CONV_CH), tok(DN_V), tok(N_GATE_COLS)],
        out_shape=[
            jax.ShapeDtypeStruct((N_TOK, ATT_Q), BF16),
            jax.ShapeDtypeStruct((N_TOK, 2 * ATT_KV), F32),
            jax.ShapeDtypeStruct((N_TOK, DN_CONV_CH), BF16),
            jax.ShapeDtypeStruct((N_TOK, DN_V), BF16),
            jax.ShapeDtypeStruct((N_TOK, N_GATE_COLS), F32),
        ],
        compiler_params=_cparams("arbitrary"),
        name="inproj",
    )(x, mod, gain, w_in, bd, qg, kg, cos, sin)


def _dup_half(x, g):
    lane = lax.broadcasted_iota(jnp.int32, x.shape, 1)
    lo = lane < HEAD_DIM
    xr = pltpu.roll(x, HEAD_DIM, axis=1)
    return jnp.where(lo, x, xr) if g == 0 else jnp.where(lo, xr, x)


def _attend(sink_ref, q_ref, o_ref, key_sets):
    rows = q_ref.shape[0]
    lane = lax.broadcasted_iota(jnp.int32, (rows, LANES), 1)
    lo = lane < HEAD_DIM
    for g in range(N_KV):
        ks = [(_dup_half(k, g).astype(BF16), _dup_half(v, g).astype(BF16), valid) for k, v, valid in key_sets]
        for jj in range(2):
            j = g * 2 + jj
            qp = q_ref[:, j * LANES:(j + 1) * LANES]
            outs = []
            for e in range(2):
                qm = jnp.where(lo if e == 0 else jnp.logical_not(lo), qp, jnp.zeros_like(qp))
                sk = sink_ref[2 * j + e]
                scores = []
                m = jnp.full((rows, 1), sk, F32)
                for k, _, valid in ks:
                    s = _dot_nt(qm, k)
                    if valid is not None:
                        s = jnp.where(valid, s, -jnp.inf)
                    scores.append(s)
                    m = jnp.maximum(m, jnp.max(s, axis=-1, keepdims=True))
                den = jnp.exp(sk - m)
                acc = jnp.zeros((rows, LANES), F32)
                for s, (_, v, _) in zip(scores, ks):
                    p = jnp.exp(s - m)
                    den = den + jnp.sum(p, axis=-1, keepdims=True)
                    acc = acc + _dot(p, v)
                outs.append(acc / den)
            o_ref[:, j * LANES:(j + 1) * LANES] = jnp.where(lo, outs[0], outs[1]).astype(o_ref.dtype)


def _attn_ctx_kernel(sink_ref, q_ref, kv_ref, o_ref):
    _attend(sink_ref, q_ref, o_ref, [(kv_ref[:, 0:ATT_KV], kv_ref[:, ATT_KV:2 * ATT_KV], None)])


def _attn_lat_kernel(sink_ref, q_ref, kv_ref, ck_ref, cv_ref, mix_in_ref, o_ref):
    del mix_in_ref
    n = pl.program_id(1)
    span = 3 * BLOCK
    start = pl.multiple_of(jnp.clip((n - 1) * BLOCK, 0, DEC_SEQ - span), BLOCK)
    kw = kv_ref[pl.ds(start, span), 0:ATT_KV]
    vw = kv_ref[pl.ds(start, span), ATT_KV:2 * ATT_KV]
    qpos = n * BLOCK + lax.broadcasted_iota(jnp.int32, (BLOCK, span), 0)
    kpos = start + lax.broadcasted_iota(jnp.int32, (BLOCK, span), 1)
    valid = jnp.abs(kpos - qpos) <= WINDOW
    _attend(sink_ref, q_ref, o_ref, [(kw, vw, valid), (ck_ref[...], cv_ref[...], None)])


def _attn_ctx_call(sink_l, q, kv):
    return pl.pallas_call(
        _attn_ctx_kernel,
        grid=(BATCH,),
        in_specs=[
            pl.BlockSpec(memory_space=pltpu.SMEM),
            pl.BlockSpec((SEQ, ATT_Q), lambda b: (b, 0)),
            pl.BlockSpec((SEQ, 2 * ATT_KV), lambda b: (b, 0)),
        ],
        out_specs=pl.BlockSpec((SEQ, ATT_Q), lambda b: (b, 0)),
        out_shape=jax.ShapeDtypeStruct((N_TOK, ATT_Q + DN_V), BF16),
        compiler_params=_cparams("arbitrary"),
        name="attn_ctx",
    )(sink_l, q, kv)


def _attn_lat_call(l, sink_l, q, kv, cache_k, cache_v, mix):
    nb = DEC_SEQ // BLOCK
    q0 = N_CTX_TOK // BLOCK
    s0 = N_CTX_TOK // DEC_SEQ
    return pl.pallas_call(
        _attn_lat_kernel,
        grid=(DEC_BATCH, nb),
        in_specs=[
            pl.BlockSpec(memory_space=pltpu.SMEM),
            pl.BlockSpec((BLOCK, ATT_Q), lambda b, n: (q0 + b * nb + n, 0)),
            pl.BlockSpec((DEC_SEQ, 2 * ATT_KV), lambda b, n: (s0 + b, 0)),
            pl.BlockSpec((None, None, PAST_LEN, ATT_KV), lambda b, n: (b, l, 0, 0)),
            pl.BlockSpec((None, None, PAST_LEN, ATT_KV), lambda b, n: (b, l, 0, 0)),
            pl.BlockSpec(memory_space=pl.ANY),
        ],
        out_specs=pl.BlockSpec((BLOCK, ATT_Q), lambda b, n: (q0 + b * nb + n, 0)),
        out_shape=jax.ShapeDtypeStruct((N_TOK, ATT_Q + DN_V), BF16),
        input_output_aliases={5: 0},
        compiler_params=_cparams("arbitrary", "arbitrary"),
        name="attn_lat",
    )(sink_l, q, kv, cache_k, cache_v, mix)


INV_PASSES = 1
GDN_UNROLL = 4


def _split_bf16(x):
    hi = x.astype(BF16)
    return hi, (x - hi.astype(F32)).astype(BF16)


def _block_diag(x, width):
    n = x.shape[1] // width
    blk = lax.broadcasted_iota(jnp.int32, x.shape, 1) // width
    zero = jnp.zeros_like(x)
    return jnp.concatenate([jnp.where(blk == h, x, zero) for h in range(n)], axis=0)


def _mm_heads(ts, xs, width):
    if INV_PASSES == 1:
        bds = [_block_diag(x.astype(BF16), width) for x in xs]
        return [jnp.dot(t.astype(BF16), bd, preferred_element_type=F32) for t, bd in zip(ts, bds)]
    tsp = [_split_bf16(t) for t in ts]
    xsp = [_split_bf16(x) for x in xs]
    bd_hi = [_block_diag(x_hi, width) for x_hi, _ in xsp]
    bd_lo = [_block_diag(x_lo, width) for _, x_lo in xsp]
    return [jnp.dot(t_hi, bh, preferred_element_type=F32) + jnp.dot(t_lo, bh, preferred_element_type=F32)
            + jnp.dot(t_hi, bl, preferred_element_type=F32) for (t_hi, t_lo), bh, bl in zip(tsp, bd_hi, bd_lo)]


def _unit_tri_inverse(mats, row, col):
    eye = (row == col).astype(F32)
    a8 = [jnp.where((row // 8) == (col // 8), a, 0.0) for a in mats]
    a8_2 = _mm_heads(a8, a8, CHUNK)
    a8_4 = _mm_heads(a8_2, a8_2, CHUNK)
    ts = [eye - a for a in a8]
    ts = [t + p for t, p in zip(ts, _mm_heads(ts, a8_2, CHUNK))]
    ts = [t + p for t, p in zip(ts, _mm_heads(ts, a8_4, CHUNK))]
    b = 8
    while b < CHUNK:
        level = ((row // (2 * b)) == (col // (2 * b))) & ((row // b) != (col // b))
        off = [jnp.where(level, a, 0.0) for a in mats]
        ts = [t - p for t, p in zip(ts, _mm_heads(_mm_heads(ts, off, CHUNK), ts, CHUNK))]
        b *= 2
    return ts


def _bcast_cols(x, first, width):
    rows = x.shape[0]
    if width == LANES:
        return jnp.concatenate([jnp.broadcast_to(x[:, first + h:first + h + 1], (rows, width)) for h in range(DN_HEADS)], axis=1)
    blk = lax.broadcasted_iota(jnp.int32, (rows, DN_HEADS * width), 1) // width
    out = jnp.broadcast_to(x[:, first:first + 1], (rows, DN_HEADS * width))
    for h in range(1, DN_HEADS):
        out = jnp.where(blk == h, jnp.broadcast_to(x[:, first + h:first + h + 1], (rows, DN_HEADS * width)), out)
    return out


def _gdn_kernel(*refs, seq_len, has_s0, write_state):
    qkvd_ref, z_ref, ab_ref, convw_ref, alog_ref, dtb_ref, ogain_ref = refs[:7]
    pos = 7
    s0_ref = None
    if has_s0:
        s0_ref = refs[pos]
        pos += 1
    pos += 1
    dn_ref = refs[pos]
    pos += 1
    sfin_ref = None
    if write_state:
        sfin_ref = refs[pos]
        pos += 1
    qn_s, kn_s, vn_s, gb_s, u0_s, w_s, qd_s, kd_s, qk_s, gt_s, o_s, st_s = refs[pos:]
    L = seq_len
    n_chunks = L // CHUNK
    n_pairs = DN_HEADS // 2
    pair_w = 2 * DK

    trow = lax.broadcasted_iota(jnp.int32, (L, LANES), 0)
    for cb in range(DN_CONV_CH // LANES):
        cs = slice(cb * LANES, (cb + 1) * LANES)
        x = qkvd_ref[:, cs].astype(F32)
        acc = None
        for i in range(DN_CONV):
            sh = i - (DN_CONV - 1) // 2
            if sh == 0:
                term = x
            else:
                xr = pltpu.roll(x, (-sh) % L, axis=0)
                ok = (trow + sh >= 0) & (trow + sh < L)
                term = jnp.where(ok, xr, 0.0)
            term = term * convw_ref[i:i + 1, cs]
            acc = term if acc is None else acc + term
        y = _silu(acc)
        kind, head = divmod(cb, DN_HEADS)
        hs = slice(head * DK, (head + 1) * DK)
        if kind == 0:
            qn_s[:, hs] = y * lax.rsqrt(jnp.sum(y * y, axis=-1, keepdims=True) + EPS) * (DK ** -0.5)
        elif kind == 1:
            kn_s[:, hs] = y * lax.rsqrt(jnp.sum(y * y, axis=-1, keepdims=True) + EPS)
        else:
            vn_s[:, hs] = y

    ab = ab_ref[...]
    glane = lax.broadcasted_iota(jnp.int32, ab.shape, 1)
    g = -jnp.exp(alog_ref[...]) * _softplus(ab + dtb_ref[...])
    gb_s[...] = jnp.where(glane < N_DIR * DN_HEADS, g, _sigmoid(ab))

    for d in range(N_DIR):
        for p in range(n_pairs):
            idx = d * n_pairs + p
            st_s[idx] = jnp.zeros((pair_w, pair_w), F32)
            if has_s0:
                st_s[idx, 0:DK, 0:DV] = s0_ref[d, 2 * p]
                st_s[idx, DK:2 * DK, DV:2 * DV] = s0_ref[d, 2 * p + 1]

    row = lax.broadcasted_iota(jnp.int32, (CHUNK, DN_HEADS * CHUNK), 0)
    col = lax.broadcasted_iota(jnp.int32, (CHUNK, DN_HEADS * CHUNK), 1) % CHUNK
    before_incl = [col <= row, col >= row]
    before_strict = [col < row, col > row]
    after_strict01 = [m.astype(F32) for m in before_strict]
    r64 = lax.broadcasted_iota(jnp.int32, (CHUNK, CHUNK), 0)
    c64 = lax.broadcasted_iota(jnp.int32, (CHUNK, CHUNK), 1)
    tri01 = [(c64 <= r64).astype(BF16), (c64 >= r64).astype(BF16)]

    def prep_step(n, carry):
        a_mats, v_betas, kb_egcs, slots = [], [], [], []
        for j in range(GDN_UNROLL):
            c = n * GDN_UNROLL + j
            rs = pl.ds(pl.multiple_of(c * CHUNK, CHUNK), CHUNK)
            gbc = gb_s[rs, :]
            q = qn_s[rs, :]
            k = kn_s[rs, :]
            v = vn_s[rs, :]
            k_bd = _block_diag(k.astype(BF16), DK)
            for d in range(N_DIR):
                first = d * DN_HEADS
                gc_all = _dot_split(tri01[d], gbc)
                last = CHUNK - 1 if d == 0 else 0
                gc_last = gc_all[last:last + 1, :]
                gt_s[d, pl.ds(c, 1), :] = jnp.exp(gc_last)
                beta = _bcast_cols(gbc, N_DIR * DN_HEADS + first, DK)
                e_gc = _bcast_cols(jnp.exp(gc_all), first, DK)
                e_kd = _bcast_cols(jnp.exp(gc_last - gc_all), first, DK)
                kb = k * beta
                gdiff = _dot_split(tri01[d], _bcast_cols(gbc, first, CHUNK) * after_strict01[d])
                decay = jnp.where(before_incl[d], jnp.exp(gdiff), 0.0)
                kk_qk = _dot_nt(jnp.concatenate([kb, q], axis=0), k_bd)
                a_mats.append(jnp.where(before_strict[d], kk_qk[0:CHUNK] * decay, 0.0))
                qk_s[d, rs, :] = jnp.where(before_incl[d], kk_qk[CHUNK:2 * CHUNK] * decay, 0.0).astype(BF16)
                qd_s[d, rs, :] = (q * e_gc).astype(BF16)
                kd_s[d, rs, :] = (k * e_kd).astype(BF16)
                v_betas.append(v * beta)
                kb_egcs.append(kb * e_gc)
                slots.append((d, rs))
        t_invs = _unit_tri_inverse(a_mats, row, col)
        for (d, rs), u0, w in zip(slots, _mm_heads(t_invs, v_betas, DV), _mm_heads(t_invs, kb_egcs, DK)):
            u0_s[d, rs, :] = u0
            w_s[d, rs, :] = w.astype(BF16)
        return carry

    lax.fori_loop(0, n_chunks // GDN_UNROLL, prep_step, 0)

    plane = lax.broadcasted_iota(jnp.int32, (1, pair_w), 1)
    srow = lax.broadcasted_iota(jnp.int32, (pair_w, pair_w), 0) // DK
    scol = lax.broadcasted_iota(jnp.int32, (pair_w, pair_w), 1) // DV
    same_head = srow == scol

    def scan_step(n, carry):
        probs = []
        for d in range(N_DIR):
            c = n if d == 0 else n_chunks - 1 - n
            rs = pl.ds(pl.multiple_of(c * CHUNK, CHUNK), CHUNK)
            gt = gt_s[d, pl.ds(c, 1), :]
            for p in range(n_pairs):
                c0 = d * DN_HEADS + 2 * p
                g_tot = jnp.where(plane < DV, gt[:, c0:c0 + 1], gt[:, c0 + 1:c0 + 2])
                probs.append((d, rs, p, d * n_pairs + p, slice(p * pair_w, (p + 1) * pair_w), g_tot))
        s_prev = [st_s[idx] for _, _, _, idx, _, _ in probs]
        s_b = [s.astype(BF16) for s in s_prev]
        ws = [jnp.dot(w_s[d, rs, ps], sb, preferred_element_type=F32) for (d, rs, _, _, ps, _), sb in zip(probs, s_b)]
        u_b = [(u0_s[d, rs, ps] - w).astype(BF16) for (d, rs, _, _, ps, _), w in zip(probs, ws)]
        upd = [lax.dot_general(kd_s[d, rs, ps], u, (((0,), (0,)), ((), ())), preferred_element_type=F32)
               for (d, rs, _, _, ps, _), u in zip(probs, u_b)]
        for (_, _, _, idx, _, g_tot), s, up in zip(probs, s_prev, upd):
            st_s[idx] = s * g_tot + jnp.where(same_head, up, 0.0)
        for (d, rs, p, _, ps, _), sb, u in zip(probs, s_b, u_b):
            o_s[d, rs, ps] = (jnp.dot(qd_s[d, rs, ps], sb, preferred_element_type=F32)
                              + jnp.dot(qk_s[d, rs, p * 2 * CHUNK:(p + 1) * 2 * CHUNK], _block_diag(u, DV),
                                        preferred_element_type=F32))
        return carry

    lax.fori_loop(0, n_chunks, scan_step, 0)

    for h in range(DN_HEADS):
        hs = slice(h * DV, (h + 1) * DV)
        o = o_s[0, :, hs] + o_s[1, :, hs]
        y = o * lax.rsqrt(jnp.mean(o * o, axis=-1, keepdims=True) + EPS) * ogain_ref[...]
        dn_ref[:, hs] = (y * _silu(z_ref[:, hs].astype(F32))).astype(dn_ref.dtype)
    if write_state:
        for d in range(N_DIR):
            for p in range(n_pairs):
                idx = d * n_pairs + p
                sfin_ref[d, 2 * p] = st_s[idx, 0:DK, 0:DV]
                sfin_ref[d, 2 * p + 1] = st_s[idx, DK:2 * DK, DV:2 * DV]


def _gdn_call(l, qkvd, z, ab, conv_w, alog16, dtb16, o_gain, state_dn, mix, *, latent):
    L = DEC_SEQ if latent else SEQ
    n_seq = DEC_BATCH if latent else BATCH
    base = N_CTX_TOK // L if latent else 0
    seq = lambda w: pl.BlockSpec((L, w), lambda b: (base + b, 0))
    in_specs = [
        seq(DN_CONV_CH), seq(DN_V), seq(N_GATE_COLS),
        pl.BlockSpec((None, DN_CONV, DN_CONV_CH), lambda b: (l, 0, 0)),
        pl.BlockSpec((None, 1, N_GATE_COLS), lambda b: (l, 0, 0)),
        pl.BlockSpec((None, 1, N_GATE_COLS), lambda b: (l, 0, 0)),
        pl.BlockSpec((None, 1, DV), lambda b: (l, 0, 0)),
    ]
    args = [qkvd, z, ab, conv_w, alog16, dtb16, o_gain]
    out_specs = [pl.BlockSpec((L, DN_V), lambda b: (base + b, 1))]
    out_shape = [jax.ShapeDtypeStruct((N_TOK, ATT_Q + DN_V), BF16)]
    if latent:
        in_specs.append(pl.BlockSpec((None, None, N_DIR, DN_HEADS, DK, DV), lambda b: (b, l, 0, 0, 0, 0)))
        args.append(state_dn)
    in_specs.append(pl.BlockSpec(memory_space=pl.ANY))
    args.append(mix)
    if not latent:
        out_specs.append(pl.BlockSpec((None, N_DIR, DN_HEADS, DK, DV), lambda b: (b, 0, 0, 0, 0)))
        out_shape.append(jax.ShapeDtypeStruct((n_seq, N_DIR, DN_HEADS, DK, DV), F32))
    n_chunks = L // CHUNK
    return pl.pallas_call(
        functools.partial(_gdn_kernel, seq_len=L, has_s0=latent, write_state=not latent),
        grid=(n_seq,),
        in_specs=in_specs,
        out_specs=out_specs,
        out_shape=out_shape,
        scratch_shapes=[
            pltpu.VMEM((L, DN_QK), F32), pltpu.VMEM((L, DN_QK), F32), pltpu.VMEM((L, DN_V), F32),
            pltpu.VMEM((L, N_GATE_COLS), F32),
            pltpu.VMEM((N_DIR, L, DN_V), F32),
            pltpu.VMEM((N_DIR, L, DN_QK), BF16),
            pltpu.VMEM((N_DIR, L, DN_QK), BF16),
            pltpu.VMEM((N_DIR, L, DN_QK), BF16),
            pltpu.VMEM((N_DIR, L, DN_HEADS * CHUNK), BF16),
            pltpu.VMEM((N_DIR, n_chunks, N_GATE_COLS), F32),
            pltpu.VMEM((N_DIR, L, DN_V), F32),
            pltpu.VMEM((N_DIR * (DN_HEADS // 2), 2 * DK, 2 * DV), F32),
        ],
        input_output_aliases={len(args) - 1: 0},
        compiler_params=_cparams("arbitrary"),
        name="gdn_lat" if latent else "gdn_ctx",
    )(*args)


def _top2_of4(vals):
    m1 = jnp.maximum(jnp.maximum(vals[0], vals[1]), jnp.maximum(vals[2], vals[3]))
    i1 = jnp.where(vals[0] == m1, 0, jnp.where(vals[1] == m1, 1, jnp.where(vals[2] == m1, 2, 3)))
    rest = [jnp.where(i1 == j, -jnp.inf, vals[j]) for j in range(4)]
    m2 = jnp.maximum(jnp.maximum(rest[0], rest[1]), jnp.maximum(rest[2], rest[3]))
    i2 = jnp.where(rest[0] == m2, 0, jnp.where(rest[1] == m2, 1, jnp.where(rest[2] == m2, 2, 3)))
    return m1, i1, m2, i2


def _outproj_kernel(x_ref, mix_ref, mod_ref, gain_ref, wo_ref, rwt_ref, rb_ref, ut_ref,
                    x1_ref, h2_ref, ri_ref, rg_ref, cnt_ref, carry_s):
    i = pl.program_id(0)

    @pl.when(i == 0)
    def _():
        carry_s[...] = jnp.zeros_like(carry_s)

    gate1 = mod_ref[:, 2 * D_MODEL:3 * D_MODEL]
    shift2 = mod_ref[:, 3 * D_MODEL:4 * D_MODEL]
    scale2 = mod_ref[:, 4 * D_MODEL:5 * D_MODEL]
    x1 = x_ref[...] + gate1 * jnp.dot(mix_ref[...], wo_ref[...], preferred_element_type=F32)
    x1_ref[...] = x1
    h2 = _adaln(x1, gain_ref[...], shift2, scale2)
    h2_ref[...] = h2

    h_hi, h_lo = _split_bf16(h2)
    r_hi, r_lo = _split_bf16(rwt_ref[...])
    nt = (((1,), (1,)), ((), ()))
    logits = (lax.dot_general(r_hi, h_hi, nt, preferred_element_type=F32)
              + lax.dot_general(r_hi, h_lo, nt, preferred_element_type=F32)
              + lax.dot_general(r_lo, h_hi, nt, preferred_element_type=F32))
    scores = _sigmoid(logits)
    biased = scores + rb_ref[...]
    rows = [biased[e:e + 1, :] for e in range(N_EXPERTS)]
    per_group = [_top2_of4(rows[g * 4:(g + 1) * 4]) for g in range(N_GROUPS)]
    gs = [pg[0] + pg[2] for pg in per_group]
    _, gsel, _, _ = _top2_of4(gs)
    pick = lambda k: jnp.where(gsel == 0, per_group[0][k], jnp.where(gsel == 1, per_group[1][k],
                               jnp.where(gsel == 2, per_group[2][k], per_group[3][k])))
    e1 = gsel * EXPERTS_PER_GROUP + pick(1)
    e2 = gsel * EXPERTS_PER_GROUP + pick(3)
    erow = lax.broadcasted_iota(jnp.int32, scores.shape, 0)
    oh1 = erow == e1
    oh2 = erow == e2
    s1 = jnp.sum(jnp.where(oh1, scores, 0.0), axis=0, keepdims=True)
    s2 = jnp.sum(jnp.where(oh2, scores, 0.0), axis=0, keepdims=True)
    tot = s1 + s2
    grow = lax.broadcasted_iota(jnp.int32, rg_ref.shape, 0)
    rg_ref[...] = jnp.where(grow == 0, s1 / tot, s2 / tot)

    sel = (oh1 | oh2).astype(BF16)
    before = jnp.dot(sel, ut_ref[...], preferred_element_type=F32) + carry_s[...]
    r1 = jnp.sum(jnp.where(oh1, before, 0.0), axis=0, keepdims=True)
    r2 = jnp.sum(jnp.where(oh2, before, 0.0), axis=0, keepdims=True)
    carry_s[...] = carry_s[...] + jnp.sum(sel.astype(F32), axis=1, keepdims=True)
    cnt_ref[...] = carry_s[...].astype(jnp.int32)
    irow = lax.broadcasted_iota(jnp.int32, ri_ref.shape, 0)
    ri_ref[...] = jnp.where(irow == 0, e1, jnp.where(irow == 1, e2, jnp.where(
        irow == 2, r1.astype(jnp.int32), r2.astype(jnp.int32))))


def _outproj_call(l, x, mix, mod, gain, w_out, router_wt, router_b, ut):
    tok = lambda w: pl.BlockSpec((TOK_TILE, w), lambda i: (i, 0))
    per_tok = lambda r: pl.BlockSpec((r, TOK_TILE), lambda i: (0, i))
    full = lambda a: pl.BlockSpec(a.shape, lambda i: (0,) * a.ndim)
    return pl.pallas_call(
        _outproj_kernel,
        grid=(N_TOK // TOK_TILE,),
        in_specs=[
            tok(D_MODEL), tok(ATT_Q + DN_V),
            pl.BlockSpec((None, None, 1, 6 * D_MODEL), lambda i: (l, _mod_row(i), 0, 0)),
            pl.BlockSpec((None, 1, D_MODEL), lambda i: (l, 0, 0)),
            pl.BlockSpec((None, ATT_Q + DN_V, D_MODEL), lambda i: (l, 0, 0)),
            full(router_wt), full(router_b), full(ut),
        ],
        out_specs=[tok(D_MODEL), tok(D_MODEL), per_tok(4), per_tok(2), pl.BlockSpec((N_EXPERTS, 1), lambda i: (0, 0))],
        out_shape=[
            jax.ShapeDtypeStruct((N_TOK, D_MODEL), F32),
            jax.ShapeDtypeStruct((N_TOK, D_MODEL), F32),
            jax.ShapeDtypeStruct((4, N_TOK), jnp.int32),
            jax.ShapeDtypeStruct((2, N_TOK), F32),
            jax.ShapeDtypeStruct((N_EXPERTS, 1), jnp.int32),
        ],
        scratch_shapes=[pltpu.VMEM((N_EXPERTS, 1), F32)],
        compiler_params=_cparams("arbitrary"),
        name="outproj_router",
    )(x, mix, mod, gain, w_out, router_wt, router_b, ut)


def _dispatch_kernel(pos_ref, h2_ref, buf_in_ref, sorted_ref, sem):
    del buf_in_ref

    def row_copy(r, k):
        return pltpu.make_async_copy(h2_ref.at[pl.ds(r, 1), :], sorted_ref.at[pl.ds(pos_ref[0, k * TOK_TILE + r], 1), :], sem)

    def start(r, c):
        row_copy(r, 0).start()
        row_copy(r, 1).start()
        return c

    def wait(r, c):
        row_copy(r, 0).wait()
        row_copy(r, 1).wait()
        return c

    lax.fori_loop(0, TOK_TILE, start, 0)
    lax.fori_loop(0, TOK_TILE, wait, 0)


def _dispatch_call(pos_tiles, h2, zeros_sorted):
    return pl.pallas_call(
        _dispatch_kernel,
        grid=(N_TOK // TOK_TILE,),
        in_specs=[
            pl.BlockSpec((None, 1, TOP_K * TOK_TILE), lambda i: (i, 0, 0), memory_space=pltpu.SMEM),
            pl.BlockSpec((TOK_TILE, D_MODEL), lambda i: (i, 0)),
            pl.BlockSpec(memory_space=pl.ANY),
        ],
        out_specs=pl.BlockSpec(memory_space=pl.ANY),
        out_shape=jax.ShapeDtypeStruct((N_SORTED, D_MODEL), F32),
        scratch_shapes=[pltpu.SemaphoreType.DMA(())],
        input_output_aliases={2: 0},
        compiler_params=_cparams("arbitrary"),
        name="moe_dispatch",
    )(pos_tiles, h2, zeros_sorted)


def _experts_kernel(te_ref, nu_ref, x_ref, wg_ref, wu_ref, wd_ref, y_ref, wg_s, wu_s, wd_s):
    j = pl.program_id(0)
    used = j < nu_ref[0]

    @pl.when((j == 0) | (te_ref[j] != te_ref[jnp.maximum(j - 1, 0)]))
    def _():
        wg_s[...] = wg_ref[...].astype(BF16)
        wu_s[...] = wu_ref[...].astype(BF16)
        wd_s[...] = wd_ref[...].astype(BF16)

    @pl.when(used)
    def _():
        x = x_ref[...].astype(BF16)
        hid = _silu(jnp.dot(x, wg_s[...], preferred_element_type=F32)) * jnp.dot(x, wu_s[...], preferred_element_type=F32)
        y_ref[...] = jnp.dot(hid.astype(BF16), wd_s[...], preferred_element_type=F32)

    @pl.when(jnp.logical_not(used))
    def _():
        y_ref[...] = jnp.zeros_like(y_ref)


def _experts_call(l, tile_expert, n_used, xs, w_gate, w_up, w_down):
    row = lambda j, te, nu: (jnp.maximum(jnp.minimum(j, nu[0] - 1), 0), 0)
    grid_spec = pltpu.PrefetchScalarGridSpec(
        num_scalar_prefetch=2,
        grid=(N_EXP_TILES,),
        in_specs=[
            pl.BlockSpec((EXP_TILE, D_MODEL), row),
            pl.BlockSpec((None, None, D_MODEL, D_FF), lambda j, te, nu: (l, te[j], 0, 0)),
            pl.BlockSpec((None, None, D_MODEL, D_FF), lambda j, te, nu: (l, te[j], 0, 0)),
            pl.BlockSpec((None, None, D_FF, D_MODEL), lambda j, te, nu: (l, te[j], 0, 0)),
        ],
        out_specs=pl.BlockSpec((EXP_TILE, D_MODEL), lambda j, te, nu: (j, 0)),
        scratch_shapes=[pltpu.VMEM((D_MODEL, D_FF), BF16), pltpu.VMEM((D_MODEL, D_FF), BF16),
                        pltpu.VMEM((D_FF, D_MODEL), BF16)],
    )
    return pl.pallas_call(
        _experts_kernel,
        grid_spec=grid_spec,
        out_shape=jax.ShapeDtypeStruct((N_SORTED, D_MODEL), F32),
        compiler_params=_cparams("arbitrary"),
        name="moe_experts",
    )(tile_expert, n_used, xs, w_gate, w_up, w_down)


def _combine_kernel(pos_ref, x1_ref, rg_ref, mod_ref, y_ref, out_ref, buf_s, sem):
    def row_copy(r, k):
        return pltpu.make_async_copy(y_ref.at[pl.ds(pos_ref[0, k * TOK_TILE + r], 1), :], buf_s.at[k, pl.ds(r, 1), :], sem)

    def start(r, c):
        row_copy(r, 0).start()
        row_copy(r, 1).start()
        return c

    def wait(r, c):
        row_copy(r, 0).wait()
        row_copy(r, 1).wait()
        return c

    lax.fori_loop(0, TOK_TILE, start, 0)
    lax.fori_loop(0, TOK_TILE, wait, 0)
    gate2 = mod_ref[:, 5 * D_MODEL:6 * D_MODEL]
    moe = buf_s[0] * rg_ref[:, 0:1] + buf_s[1] * rg_ref[:, 1:2]
    out_ref[...] = x1_ref[...] + gate2 * moe


def _combine_call(l, pos_tiles, x1, rg, mod, y_sorted):
    tok = lambda w: pl.BlockSpec((TOK_TILE, w), lambda i: (i, 0))
    return pl.pallas_call(
        _combine_kernel,
        grid=(N_TOK // TOK_TILE,),
        in_specs=[
            pl.BlockSpec((None, 1, TOP_K * TOK_TILE), lambda i: (i, 0, 0), memory_space=pltpu.SMEM),
            tok(D_MODEL), tok(2),
            pl.BlockSpec((None, None, 1, 6 * D_MODEL), lambda i: (l, _mod_row(i), 0, 0)),
            pl.BlockSpec(memory_space=pl.ANY),
        ],
        out_specs=tok(D_MODEL),
        out_shape=jax.ShapeDtypeStruct((N_TOK, D_MODEL), F32),
        scratch_shapes=[pltpu.VMEM((TOP_K, TOK_TILE, D_MODEL), F32), pltpu.SemaphoreType.DMA(())],
        compiler_params=_cparams("arbitrary"),
        name="moe_combine",
    )(pos_tiles, x1, rg, mod, y_sorted)


def _rope_tables():
    pos = jnp.arange(DEC_SEQ)
    r = (pos // GRID_W).astype(F32)
    c = (pos % GRID_W).astype(F32)
    inv = ROPE_BASE ** (-jnp.arange(ROPE_PAIRS, dtype=F32) / ROPE_PAIRS)
    ar, ac = r[:, None] * inv, c[:, None] * inv
    cos = jnp.concatenate([jnp.cos(ar), jnp.cos(ar), jnp.cos(ac), jnp.cos(ac)], axis=-1)
    sin = jnp.concatenate([-jnp.sin(ar), jnp.sin(ar), -jnp.sin(ac), jnp.sin(ac)], axis=-1)
    return jnp.tile(cos, (1, N_HEADS)), jnp.tile(sin, (1, N_HEADS))


def _routing_layout(ri, counts):
    counts = counts.reshape(N_EXPERTS)
    padded = ((counts + EXP_TILE - 1) // EXP_TILE) * EXP_TILE
    ends = jnp.cumsum(padded)
    offs = ends - padded
    pos = offs[ri[0:2]] + ri[2:4]
    n_used = (ends[-1] // EXP_TILE).astype(jnp.int32)
    tile_start = jnp.arange(N_EXP_TILES, dtype=jnp.int32) * EXP_TILE
    tile_expert = jnp.sum((tile_start[:, None] >= ends[None, :]).astype(jnp.int32), axis=1)
    last = jnp.sum((jnp.maximum(ends[-1] - EXP_TILE, 0) >= ends).astype(jnp.int32))
    tile_expert = jnp.minimum(jnp.where(tile_start < ends[-1], tile_expert, last), N_EXPERTS - 1).astype(jnp.int32)
    pos_tiles = pos.reshape(TOP_K, N_TOK // TOK_TILE, TOK_TILE).transpose(1, 0, 2).reshape(
        N_TOK // TOK_TILE, 1, TOP_K * TOK_TILE).astype(jnp.int32)
    return pos_tiles, tile_expert, n_used.reshape(1)


def kernel(x_prompt, x_sample, cache_k, cache_v, state_dn, c, c_ctx, w_ada, b_ada, norm_attn, norm_ffn,
           w_in, conv_w, a_log, dt_bias, q_norm, k_norm, sink, o_norm, w_out, router_w, router_bias,
           w_gate, w_up, w_down):
    x = jnp.concatenate([x_prompt.reshape(N_CTX_TOK, D_MODEL), x_sample.reshape(N_LAT_TOK, D_MODEL)], axis=0)
    cond = jnp.concatenate([c_ctx[None, :], c, jnp.zeros((N_COND - 1 - DEC_BATCH, D_MODEL), F32)], axis=0)
    mod = _modulation_call(cond, w_ada, b_ada).reshape(DEPTH, N_COND, 1, 6 * D_MODEL)

    w_in_b = w_in.astype(BF16)
    w_out_b = w_out.astype(BF16)
    seg = jnp.arange(ATT_Q) // HEAD_DIM
    bd = jnp.where(seg[:, None] == seg[None, :], 1.0 / HEAD_DIM, 0.0).astype(BF16)
    qg = jnp.tile(q_norm, (1, N_HEADS)).reshape(DEPTH, 1, ATT_Q)
    kg = jnp.tile(k_norm, (1, N_KV)).reshape(DEPTH, 1, ATT_KV)
    cos, sin = _rope_tables()
    gain1 = norm_attn.reshape(DEPTH, 1, D_MODEL)
    gain2 = norm_ffn.reshape(DEPTH, 1, D_MODEL)
    pad8 = lambda a: jnp.pad(a.reshape(DEPTH, 1, N_DIR * DN_HEADS), ((0, 0), (0, 0), (0, N_GATE_COLS - N_DIR * DN_HEADS)))
    alog16, dtb16 = pad8(a_log), pad8(dt_bias)
    o_gain = o_norm.reshape(DEPTH, 1, DV)
    ck = cache_k.reshape(DEC_BATCH, DEPTH, PAST_LEN, ATT_KV)
    cv = cache_v.reshape(DEC_BATCH, DEPTH, PAST_LEN, ATT_KV)
    router_wt = router_w.T
    rb = router_bias.reshape(N_EXPERTS, 1)
    tri = jnp.arange(TOK_TILE)
    ut = (tri[:, None] < tri[None, :]).astype(BF16)
    zeros_sorted = jnp.zeros((N_SORTED, D_MODEL), F32)

    k_list, v_list, s_list = [], [], []
    for l in range(DEPTH):
        q, kv, qkvd, z, ab = _inproj_call(l, x, mod, gain1, w_in_b, bd, qg, kg, cos, sin)
        k_list.append(kv[:N_CTX_TOK, 0:ATT_KV].reshape(BATCH, SEQ, N_KV, HEAD_DIM))
        v_list.append(kv[:N_CTX_TOK, ATT_KV:].reshape(BATCH, SEQ, N_KV, HEAD_DIM))
        mix = _attn_ctx_call(sink[l], q, kv)
        mix = _attn_lat_call(l, sink[l], q, kv, ck, cv, mix)
        mix, s_c = _gdn_call(l, qkvd, z, ab, conv_w, alog16, dtb16, o_gain, state_dn, mix, latent=False)
        (mix,) = _gdn_call(l, qkvd, z, ab, conv_w, alog16, dtb16, o_gain, state_dn, mix, latent=True)
        s_list.append(s_c)
        x1, h2, ri, rg, counts = _outproj_call(l, x, mix, mod, gain2, w_out_b, router_wt, rb, ut)
        pos_tiles, tile_expert, n_used = _routing_layout(ri, counts)
        xs_sorted = _dispatch_call(pos_tiles, h2, zeros_sorted)
        y_sorted = _experts_call(l, tile_expert, n_used, xs_sorted, w_gate, w_up, w_down)
        x = _combine_call(l, pos_tiles, x1, rg.T, mod, y_sorted)

    y_prompt = x[:N_CTX_TOK].reshape(BATCH, SEQ, D_MODEL)
    y_sample = x[N_CTX_TOK:].reshape(DEC_BATCH, DEC_SEQ, D_MODEL)
    return (y_prompt, y_sample, jnp.stack(k_list, axis=1), jnp.stack(v_list, axis=1), jnp.stack(s_list, axis=1))
```

```python
import functools

import jax
import jax.numpy as jnp
from jax import lax
from jax.experimental import pallas as pl
from jax.experimental.pallas import tpu as pltpu

D_MODEL = 1024
BATCH = 32
SEQ = 256
DEPTH = 4
DEC_BATCH = 4
DEC_SEQ = 1024
PAST_LEN = 256
GRID_W = 64
N_HEADS = 8
N_KV = 2
HEAD_DIM = 64
WINDOW = 128
BLOCK = 128
ATT_SCALE = HEAD_DIM ** -0.5
ROPE_BASE = 10000.0
ROPE_PAIRS = HEAD_DIM // 4
DN_HEADS = 4
DK = 128
DV = 128
DN_CONV = 5
CHUNK = 64
N_DIR = 2
ATT_Q = N_HEADS * HEAD_DIM
ATT_KV = N_KV * HEAD_DIM
DN_QK = DN_HEADS * DK
DN_V = DN_HEADS * DV
DN_CONV_CH = 2 * DN_QK + DN_V
N_GATE_COLS = 2 * N_DIR * DN_HEADS
IN_DIM = ATT_Q + 2 * ATT_KV + DN_CONV_CH + DN_V + N_GATE_COLS
N_EXPERTS = 16
N_GROUPS = 4
EXPERTS_PER_GROUP = 4
TOP_K = 2
D_FF = 512
EPS = 1e-6

N_CTX_TOK = BATCH * SEQ
N_LAT_TOK = DEC_BATCH * DEC_SEQ
N_TOK = N_CTX_TOK + N_LAT_TOK
N_COND = 8

LANES = 128
TOK_TILE = 256
EXP_TILE = 256
N_EXP_TILES = (N_TOK * TOP_K) // EXP_TILE + N_EXPERTS
N_SORTED = N_EXP_TILES * EXP_TILE
VMEM_LIMIT = 56 * 1024 * 1024
ROW_DMA_UNROLL = 8

F32 = jnp.float32
BF16 = jnp.bfloat16


def _cparams(*sem):
    return pltpu.CompilerParams(dimension_semantics=sem, vmem_limit_bytes=VMEM_LIMIT)


def _dot(a, b):
    return jnp.dot(a.astype(BF16), b.astype(BF16), preferred_element_type=F32)


def _dot_nt(a, b):
    return lax.dot_general(a.astype(BF16), b.astype(BF16), (((1,), (1,)), ((), ())), preferred_element_type=F32)


def _dot_tn(a, b):
    return lax.dot_general(a.astype(BF16), b.astype(BF16), (((0,), (0,)), ((), ())), preferred_element_type=F32)


def _dot_f32(a, b):
    return jnp.dot(a, b, preferred_element_type=F32, precision=lax.Precision.HIGHEST)


def _dot_split(mat01, x):
    hi = x.astype(BF16)
    lo = (x - hi.astype(F32)).astype(BF16)
    return (jnp.dot(mat01, hi, preferred_element_type=F32) + jnp.dot(mat01, lo, preferred_element_type=F32))


def _sigmoid(x):
    return 1.0 / (1.0 + jnp.exp(-x))


def _silu(x):
    return x * _sigmoid(x)


def _softplus(x):
    return jnp.maximum(x, 0.0) + jnp.log(1.0 + jnp.exp(-jnp.abs(x)))


MOD_TN = 1536


def _mod_kernel(cond_ref, w_ref, b_ref, o_ref):
    c = _silu(cond_ref[...])
    o_ref[...] = _dot(c, w_ref[...]) + b_ref[...]


def _modulation_call(cond, w_ada, b_ada):
    n_col = 6 * D_MODEL
    return pl.pallas_call(
        _mod_kernel,
        grid=(DEPTH, n_col // MOD_TN),
        in_specs=[
            pl.BlockSpec((N_COND, D_MODEL), lambda l, j: (0, 0)),
            pl.BlockSpec((None, D_MODEL, MOD_TN), lambda l, j: (l, 0, j)),
            pl.BlockSpec((None, 1, MOD_TN), lambda l, j: (l, 0, j)),
        ],
        out_specs=pl.BlockSpec((None, N_COND, MOD_TN), lambda l, j: (l, 0, j)),
        out_shape=jax.ShapeDtypeStruct((DEPTH, N_COND, n_col), F32),
        compiler_params=_cparams("arbitrary", "arbitrary"),
        name="modulation",
    )(cond, w_ada, b_ada.reshape(DEPTH, 1, n_col))


def _mod_row(i):
    n_ctx = N_CTX_TOK // TOK_TILE
    per_seq = DEC_SEQ // TOK_TILE
    return jnp.where(i < n_ctx, 0, 1 + (jnp.maximum(i - n_ctx, 0)) // per_seq)


def _adaln(x, gain, shift, scale):
    ms = jnp.mean(x * x, axis=-1, keepdims=True)
    return (x * lax.rsqrt(ms + EPS) * gain) * (1.0 + scale) + shift


def _seg_rms(x, bd, gain):
    x2 = x * x
    hi = x2.astype(BF16)
    lo = (x2 - hi.astype(F32)).astype(BF16)
    ms = jnp.dot(hi, bd, preferred_element_type=F32) + jnp.dot(lo, bd, preferred_element_type=F32)
    return x * lax.rsqrt(ms + EPS) * gain


def _rope(x, cos, sin_signed):
    w = x.shape[-1]
    lane = lax.broadcasted_iota(jnp.int32, x.shape, 1)
    first = (lane % (2 * ROPE_PAIRS)) < ROPE_PAIRS
    partner = jnp.where(first, pltpu.roll(x, w - ROPE_PAIRS, axis=1), pltpu.roll(x, ROPE_PAIRS, axis=1))
    return x * cos + partner * sin_signed


def _inproj_kernel(x_ref, mod_ref, gain_ref, w_ref, bd_ref, qg_ref, kg_ref, cos_ref, sin_ref,
                   q_ref, kv_ref, qkvd_ref, z_ref, ab_ref):
    i = pl.program_id(0)
    is_lat = i >= N_CTX_TOK // TOK_TILE
    shift = mod_ref[:, 0:D_MODEL]
    scale = mod_ref[:, D_MODEL:2 * D_MODEL]
    h = _adaln(x_ref[...], gain_ref[...], shift, scale).astype(BF16)

    qa = jnp.dot(h, w_ref[:, 0:ATT_Q], preferred_element_type=F32)
    qn = _seg_rms(qa, bd_ref[...], qg_ref[...])
    ka = jnp.dot(h, w_ref[:, ATT_Q:ATT_Q + ATT_KV], preferred_element_type=F32)
    kn = _seg_rms(ka, bd_ref[0:ATT_KV, 0:ATT_KV], kg_ref[...])

    @pl.when(is_lat)
    def _():
        q_ref[...] = (_rope(qn, cos_ref[...], sin_ref[...]) * ATT_SCALE).astype(BF16)
        kv_ref[:, 0:ATT_KV] = _rope(kn, cos_ref[:, 0:ATT_KV], sin_ref[:, 0:ATT_KV])

    @pl.when(jnp.logical_not(is_lat))
    def _():
        q_ref[...] = (qn * ATT_SCALE).astype(BF16)
        kv_ref[:, 0:ATT_KV] = kn

    c0 = ATT_Q + ATT_KV
    kv_ref[:, ATT_KV:2 * ATT_KV] = jnp.dot(h, w_ref[:, c0:c0 + ATT_KV], preferred_element_type=F32)
    c0 += ATT_KV
    for j in range(DN_CONV_CH // 512):
        qkvd_ref[:, j * 512:(j + 1) * 512] = jnp.dot(
            h, w_ref[:, c0 + j * 512:c0 + (j + 1) * 512], preferred_element_type=F32).astype(BF16)
    c0 += DN_CONV_CH
    z_ref[...] = jnp.dot(h, w_ref[:, c0:c0 + DN_V], preferred_element_type=F32).astype(BF16)
    c0 += DN_V
    ab_ref[...] = jnp.dot(h, w_ref[:, c0:c0 + N_GATE_COLS], preferred_element_type=F32)


def _inproj_call(l, x, mod, gain, w_in, bd, qg, kg, cos, sin):
    n_ctx = N_CTX_TOK // TOK_TILE
    per_seq = DEC_SEQ // TOK_TILE

    def pos_map(i):
        return (jnp.maximum(i - n_ctx, 0) % per_seq, 0)

    tok = lambda w: pl.BlockSpec((TOK_TILE, w), lambda i: (i, 0))
    full = lambda a: pl.BlockSpec(a.shape, lambda i: (0,) * a.ndim)
    return pl.pallas_call(
        _inproj_kernel,
        grid=(N_TOK // TOK_TILE,),
        in_specs=[
            tok(D_MODEL),
            pl.BlockSpec((None, None, 1, 6 * D_MODEL), lambda i: (l, _mod_row(i), 0, 0)),
            pl.BlockSpec((None, 1, D_MODEL), lambda i: (l, 0, 0)),
            pl.BlockSpec((None, D_MODEL, IN_DIM), lambda i: (l, 0, 0)),
            full(bd),
            pl.BlockSpec((None, 1, ATT_Q), lambda i: (l, 0, 0)),
            pl.BlockSpec((None, 1, ATT_KV), lambda i: (l, 0, 0)),
            pl.BlockSpec((TOK_TILE, ATT_Q), pos_map),
            pl.BlockSpec((TOK_TILE, ATT_Q), pos_map),
        ],
        out_specs=[tok(ATT_Q), tok(2 * ATT_KV), tok(DN_CONV_CH), tok(DN_V), tok(N_GATE_COLS)],
        out_shape=[
            jax.ShapeDtypeStruct((N_TOK, ATT_Q), BF16),
            jax.ShapeDtypeStruct((N_TOK, 2 * ATT_KV), F32),
            jax.ShapeDtypeStruct((N_TOK, DN_CONV_CH), BF16),
            jax.ShapeDtypeStruct((N_TOK, DN_V), BF16),
            jax.ShapeDtypeStruct((N_TOK, N_GATE_COLS), F32),
        ],
        compiler_params=_cparams("arbitrary"),
        name="inproj",
    )(x, mod, gain, w_in, bd, qg, kg, cos, sin)


def _dup_half(x, g):
    lane = lax.broadcasted_iota(jnp.int32, x.shape, 1)
    lo = lane < HEAD_DIM
    xr = pltpu.roll(x, HEAD_DIM, axis=1)
    return jnp.where(lo, x, xr) if g == 0 else jnp.where(lo, xr, x)


def _attend(sink_ref, q_ref, o_ref, key_sets):
    rows = q_ref.shape[0]
    lane = lax.broadcasted_iota(jnp.int32, (rows, LANES), 1)
    lo = lane < HEAD_DIM
    for g in range(N_KV):
        ks = [(_dup_half(k, g).astype(BF16), _dup_half(v, g).astype(BF16), valid) for k, v, valid in key_sets]
        for jj in range(2):
            j = g * 2 + jj
            qp = q_ref[:, j * LANES:(j + 1) * LANES]
            outs = []
            for e in range(2):
                qm = jnp.where(lo if e == 0 else jnp.logical_not(lo), qp, jnp.zeros_like(qp))
                sk = sink_ref[2 * j + e]
                scores = []
                m = jnp.full((rows, 1), sk, F32)
                for k, _, valid in ks:
                    s = _dot_nt(qm, k)
                    if valid is not None:
                        s = jnp.where(valid, s, -jnp.inf)
                    scores.append(s)
                    m = jnp.maximum(m, jnp.max(s, axis=-1, keepdims=True))
                den = jnp.exp(sk - m)
                acc = jnp.zeros((rows, LANES), F32)
                for s, (_, v, _) in zip(scores, ks):
                    p = jnp.exp(s - m)
                    den = den + jnp.sum(p, axis=-1, keepdims=True)
                    acc = acc + _dot(p, v)
                outs.append(acc / den)
            o_ref[:, j * LANES:(j + 1) * LANES] = jnp.where(lo, outs[0], outs[1]).astype(o_ref.dtype)


def _attn_ctx_kernel(sink_ref, q_ref, kv_ref, o_ref):
    _attend(sink_ref, q_ref, o_ref, [(kv_ref[:, 0:ATT_KV], kv_ref[:, ATT_KV:2 * ATT_KV], None)])


def _attn_lat_kernel(sink_ref, q_ref, kv_ref, ck_ref, cv_ref, mix_in_ref, o_ref):
    del mix_in_ref
    n = pl.program_id(1)
    span = 3 * BLOCK
    start = pl.multiple_of(jnp.clip((n - 1) * BLOCK, 0, DEC_SEQ - span), BLOCK)
    kw = kv_ref[pl.ds(start, span), 0:ATT_KV]
    vw = kv_ref[pl.ds(start, span), ATT_KV:2 * ATT_KV]
    qpos = n * BLOCK + lax.broadcasted_iota(jnp.int32, (BLOCK, span), 0)
    kpos = start + lax.broadcasted_iota(jnp.int32, (BLOCK, span), 1)
    valid = jnp.abs(kpos - qpos) <= WINDOW
    _attend(sink_ref, q_ref, o_ref, [(kw, vw, valid), (ck_ref[...], cv_ref[...], None)])


def _attn_ctx_call(sink_l, q, kv):
    return pl.pallas_call(
        _attn_ctx_kernel,
        grid=(BATCH,),
        in_specs=[
            pl.BlockSpec(memory_space=pltpu.SMEM),
            pl.BlockSpec((SEQ, ATT_Q), lambda b: (b, 0)),
            pl.BlockSpec((SEQ, 2 * ATT_KV), lambda b: (b, 0)),
        ],
        out_specs=pl.BlockSpec((SEQ, ATT_Q), lambda b: (b, 0)),
        out_shape=jax.ShapeDtypeStruct((N_TOK, ATT_Q + DN_V), BF16),
        compiler_params=_cparams("arbitrary"),
        name="attn_ctx",
    )(sink_l, q, kv)


def _attn_lat_call(l, sink_l, q, kv, cache_k, cache_v, mix):
    nb = DEC_SEQ // BLOCK
    q0 = N_CTX_TOK // BLOCK
    s0 = N_CTX_TOK // DEC_SEQ
    return pl.pallas_call(
        _attn_lat_kernel,
        grid=(DEC_BATCH, nb),
        in_specs=[
            pl.BlockSpec(memory_space=pltpu.SMEM),
            pl.BlockSpec((BLOCK, ATT_Q), lambda b, n: (q0 + b * nb + n, 0)),
            pl.BlockSpec((DEC_SEQ, 2 * ATT_KV), lambda b, n: (s0 + b, 0)),
            pl.BlockSpec((None, None, PAST_LEN, ATT_KV), lambda b, n: (b, l, 0, 0)),
            pl.BlockSpec((None, None, PAST_LEN, ATT_KV), lambda b, n: (b, l, 0, 0)),
            pl.BlockSpec(memory_space=pl.ANY),
        ],
        out_specs=pl.BlockSpec((BLOCK, ATT_Q), lambda b, n: (q0 + b * nb + n, 0)),
        out_shape=jax.ShapeDtypeStruct((N_TOK, ATT_Q + DN_V), BF16),
        input_output_aliases={5: 0},
        compiler_params=_cparams("arbitrary", "arbitrary"),
        name="attn_lat",
    )(sink_l, q, kv, cache_k, cache_v, mix)


INV_PASSES = 1
GDN_UNROLL = 4


def _split_bf16(x):
    hi = x.astype(BF16)
    return hi, (x - hi.astype(F32)).astype(BF16)


def _block_diag(x, width):
    n = x.shape[1] // width
    blk = lax.broadcasted_iota(jnp.int32, x.shape, 1) // width
    zero = jnp.zeros_like(x)
    return jnp.concatenate([jnp.where(blk == h, x, zero) for h in range(n)], axis=0)


def _mm_heads(ts, xs, width):
    if INV_PASSES == 1:
        bds = [_block_diag(x.astype(BF16), width) for x in xs]
        return [jnp.dot(t.astype(BF16), bd, preferred_element_type=F32) for t, bd in zip(ts, bds)]
    tsp = [_split_bf16(t) for t in ts]
    xsp = [_split_bf16(x) for x in xs]
    bd_hi = [_block_diag(x_hi, width) for x_hi, _ in xsp]
    bd_lo = [_block_diag(x_lo, width) for _, x_lo in xsp]
    return [jnp.dot(t_hi, bh, preferred_element_type=F32) + jnp.dot(t_lo, bh, preferred_element_type=F32)
            + jnp.dot(t_hi, bl, preferred_element_type=F32) for (t_hi, t_lo), bh, bl in zip(tsp, bd_hi, bd_lo)]


def _unit_tri_inverse(mats, row, col):
    eye = (row == col).astype(F32)
    a8 = [jnp.where((row // 8) == (col // 8), a, 0.0) for a in mats]
    a8_2 = _mm_heads(a8, a8, CHUNK)
    a8_4 = _mm_heads(a8_2, a8_2, CHUNK)
    ts = [eye - a for a in a8]
    ts = [t + p for t, p in zip(ts, _mm_heads(ts, a8_2, CHUNK))]
    ts = [t + p for t, p in zip(ts, _mm_heads(ts, a8_4, CHUNK))]
    b = 8
    while b < CHUNK:
        level = ((row // (2 * b)) == (col // (2 * b))) & ((row // b) != (col // b))
        off = [jnp.where(level, a, 0.0) for a in mats]
        ts = [t - p for t, p in zip(ts, _mm_heads(_mm_heads(ts, off, CHUNK), ts, CHUNK))]
        b *= 2
    return ts


def _bcast_cols(x, first, width):
    rows = x.shape[0]
    if width == LANES:
        return jnp.concatenate([jnp.broadcast_to(x[:, first + h:first + h + 1], (rows, width)) for h in range(DN_HEADS)], axis=1)
    blk = lax.broadcasted_iota(jnp.int32, (rows, DN_HEADS * width), 1) // width
    out = jnp.broadcast_to(x[:, first:first + 1], (rows, DN_HEADS * width))
    for h in range(1, DN_HEADS):
        out = jnp.where(blk == h, jnp.broadcast_to(x[:, first + h:first + h + 1], (rows, DN_HEADS * width)), out)
    return out


def _gdn_kernel(*refs, seq_len, has_s0, write_state):
    qkvd_ref, z_ref, ab_ref, convw_ref, alog_ref, dtb_ref, ogain_ref = refs[:7]
    pos = 7
    s0_ref = None
    if has_s0:
        s0_ref = refs[pos]
        pos += 1
    pos += 1
    dn_ref = refs[pos]
    pos += 1
    sfin_ref = None
    if write_state:
        sfin_ref = refs[pos]
        pos += 1
    qn_s, kn_s, vn_s, gb_s, u0_s, w_s, qd_s, kd_s, qk_s, gt_s, o_s, st_s = refs[pos:]
    L = seq_len
    n_chunks = L // CHUNK
    n_pairs = DN_HEADS // 2
    pair_w = 2 * DK

    trow = lax.broadcasted_iota(jnp.int32, (L, LANES), 0)
    for cb in range(DN_CONV_CH // LANES):
        cs = slice(cb * LANES, (cb + 1) * LANES)
        x = qkvd_ref[:, cs].astype(F32)
        acc = None
        for i in range(DN_CONV):
            sh = i - (DN_CONV - 1) // 2
            if sh == 0:
                term = x
            else:
                xr = pltpu.roll(x, (-sh) % L, axis=0)
                ok = (trow + sh >= 0) & (trow + sh < L)
                term = jnp.where(ok, xr, 0.0)
            term = term * convw_ref[i:i + 1, cs]
            acc = term if acc is None else acc + term
        y = _silu(acc)
        kind, head = divmod(cb, DN_HEADS)
        hs = slice(head * DK, (head + 1) * DK)
        if kind == 0:
            qn_s[:, hs] = y * lax.rsqrt(jnp.sum(y * y, axis=-1, keepdims=True) + EPS) * (DK ** -0.5)
        elif kind == 1:
            kn_s[:, hs] = y * lax.rsqrt(jnp.sum(y * y, axis=-1, keepdims=True) + EPS)
        else:
            vn_s[:, hs] = y

    ab = ab_ref[...]
    glane = lax.broadcasted_iota(jnp.int32, ab.shape, 1)
    g = -jnp.exp(alog_ref[...]) * _softplus(ab + dtb_ref[...])
    gb_s[...] = jnp.where(glane < N_DIR * DN_HEADS, g, _sigmoid(ab))

    for d in range(N_DIR):
        for p in range(n_pairs):
            idx = d * n_pairs + p
            st_s[idx] = jnp.zeros((pair_w, pair_w), F32)
            if has_s0:
                st_s[idx, 0:DK, 0:DV] = s0_ref[d, 2 * p]
                st_s[idx, DK:2 * DK, DV:2 * DV] = s0_ref[d, 2 * p + 1]

    row = lax.broadcasted_iota(jnp.int32, (CHUNK, DN_HEADS * CHUNK), 0)
    col = lax.broadcasted_iota(jnp.int32, (CHUNK, DN_HEADS * CHUNK), 1) % CHUNK
    before_incl = [col <= row, col >= row]
    before_strict = [col < row, col > row]
    after_strict01 = [m.astype(F32) for m in before_strict]
    r64 = lax.broadcasted_iota(jnp.int32, (CHUNK, CHUNK), 0)
    c64 = lax.broadcasted_iota(jnp.int32, (CHUNK, CHUNK), 1)
    tri01 = [(c64 <= r64).astype(BF16), (c64 >= r64).astype(BF16)]

    def prep_step(n, carry):
        a_mats, v_betas, kb_egcs, slots = [], [], [], []
        for j in range(GDN_UNROLL):
            c = n * GDN_UNROLL + j
            rs = pl.ds(pl.multiple_of(c * CHUNK, CHUNK), CHUNK)
            gbc = gb_s[rs, :]
            q = qn_s[rs, :]
            k = kn_s[rs, :]
            v = vn_s[rs, :]
            k_bd = _block_diag(k.astype(BF16), DK)
            for d in range(N_DIR):
                first = d * DN_HEADS
                gc_all = _dot_split(tri01[d], gbc)
                last = CHUNK - 1 if d == 0 else 0
                gc_last = gc_all[last:last + 1, :]
                gt_s[d, pl.ds(c, 1), :] = jnp.exp(gc_last)
                beta = _bcast_cols(gbc, N_DIR * DN_HEADS + first, DK)
                e_gc = _bcast_cols(jnp.exp(gc_all), first, DK)
                e_kd = _bcast_cols(jnp.exp(gc_last - gc_all), first, DK)
                kb = k * beta
                gdiff = _dot_split(tri01[d], _bcast_cols(gbc, first, CHUNK) * after_strict01[d])
                decay = jnp.where(before_incl[d], jnp.exp(gdiff), 0.0)
                kk_qk = _dot_nt(jnp.concatenate([kb, q], axis=0), k_bd)
                a_mats.append(jnp.where(before_strict[d], kk_qk[0:CHUNK] * decay, 0.0))
                qk_s[d, rs, :] = jnp.where(before_incl[d], kk_qk[CHUNK:2 * CHUNK] * decay, 0.0).astype(BF16)
                qd_s[d, rs, :] = (q * e_gc).astype(BF16)
                kd_s[d, rs, :] = (k * e_kd).astype(BF16)
                v_betas.append(v * beta)
                kb_egcs.append(kb * e_gc)
                slots.append((d, rs))
        t_invs = _unit_tri_inverse(a_mats, row, col)
        for (d, rs), u0, w in zip(slots, _mm_heads(t_invs, v_betas, DV), _mm_heads(t_invs, kb_egcs, DK)):
            u0_s[d, rs, :] = u0
            w_s[d, rs, :] = w.astype(BF16)
        return carry

    lax.fori_loop(0, n_chunks // GDN_UNROLL, prep_step, 0)

    plane = lax.broadcasted_iota(jnp.int32, (1, pair_w), 1)
    srow = lax.broadcasted_iota(jnp.int32, (pair_w, pair_w), 0) // DK
    scol = lax.broadcasted_iota(jnp.int32, (pair_w, pair_w), 1) // DV
    same_head = srow == scol

    def scan_step(n, carry):
        probs = []
        for d in range(N_DIR):
            c = n if d == 0 else n_chunks - 1 - n
            rs = pl.ds(pl.multiple_of(c * CHUNK, CHUNK), CHUNK)
            gt = gt_s[d, pl.ds(c, 1), :]
            for p in range(n_pairs):
                c0 = d * DN_HEADS + 2 * p
                g_tot = jnp.where(plane < DV, gt[:, c0:c0 + 1], gt[:, c0 + 1:c0 + 2])
                probs.append((d, rs, p, d * n_pairs + p, slice(p * pair_w, (p + 1) * pair_w), g_tot))
        s_prev = [st_s[idx] for _, _, _, idx, _, _ in probs]
        s_b = [s.astype(BF16) for s in s_prev]
        ws = [jnp.dot(w_s[d, rs, ps], sb, preferred_element_type=F32) for (d, rs, _, _, ps, _), sb in zip(probs, s_b)]
        u_b = [(u0_s[d, rs, ps] - w).astype(BF16) for (d, rs, _, _, ps, _), w in zip(probs, ws)]
        upd = [lax.dot_general(kd_s[d, rs, ps], u, (((0,), (0,)), ((), ())), preferred_element_type=F32)
               for (d, rs, _, _, ps, _), u in zip(probs, u_b)]
        for (_, _, _, idx, _, g_tot), s, up in zip(probs, s_prev, upd):
            st_s[idx] = s * g_tot + jnp.where(same_head, up, 0.0)
        for (d, rs, p, _, ps, _), sb, u in zip(probs, s_b, u_b):
            o_s[d, rs, ps] = (jnp.dot(qd_s[d, rs, ps], sb, preferred_element_type=F32)
                              + jnp.dot(qk_s[d, rs, p * 2 * CHUNK:(p + 1) * 2 * CHUNK], _block_diag(u, DV),
                                        preferred_element_type=F32))
        return carry

    lax.fori_loop(0, n_chunks, scan_step, 0)

    for h in range(DN_HEADS):
        hs = slice(h * DV, (h + 1) * DV)
        o = o_s[0, :, hs] + o_s[1, :, hs]
        y = o * lax.rsqrt(jnp.mean(o * o, axis=-1, keepdims=True) + EPS) * ogain_ref[...]
        dn_ref[:, hs] = (y * _silu(z_ref[:, hs].astype(F32))).astype(dn_ref.dtype)
    if write_state:
        for d in range(N_DIR):
            for p in range(n_pairs):
                idx = d * n_pairs + p
                sfin_ref[d, 2 * p] = st_s[idx, 0:DK, 0:DV]
                sfin_ref[d, 2 * p + 1] = st_s[idx, DK:2 * DK, DV:2 * DV]


def _gdn_call(l, qkvd, z, ab, conv_w, alog16, dtb16, o_gain, state_dn, mix, *, latent):
    L = DEC_SEQ if latent else SEQ
    n_seq = DEC_BATCH if latent else BATCH
    base = N_CTX_TOK // L if latent else 0
    seq = lambda w: pl.BlockSpec((L, w), lambda b: (base + b, 0))
    in_specs = [
        seq(DN_CONV_CH), seq(DN_V), seq(N_GATE_COLS),
        pl.BlockSpec((None, DN_CONV, DN_CONV_CH), lambda b: (l, 0, 0)),
        pl.BlockSpec((None, 1, N_GATE_COLS), lambda b: (l, 0, 0)),
        pl.BlockSpec((None, 1, N_GATE_COLS), lambda b: (l, 0, 0)),
        pl.BlockSpec((None, 1, DV), lambda b: (l, 0, 0)),
    ]
    args = [qkvd, z, ab, conv_w, alog16, dtb16, o_gain]
    out_specs = [pl.BlockSpec((L, DN_V), lambda b: (base + b, 1))]
    out_shape = [jax.ShapeDtypeStruct((N_TOK, ATT_Q + DN_V), BF16)]
    if latent:
        in_specs.append(pl.BlockSpec((None, None, N_DIR, DN_HEADS, DK, DV), lambda b: (b, l, 0, 0, 0, 0)))
        args.append(state_dn)
    in_specs.append(pl.BlockSpec(memory_space=pl.ANY))
    args.append(mix)
    if not latent:
        out_specs.append(pl.BlockSpec((None, N_DIR, DN_HEADS, DK, DV), lambda b: (b, 0, 0, 0, 0)))
        out_shape.append(jax.ShapeDtypeStruct((n_seq, N_DIR, DN_HEADS, DK, DV), F32))
    n_chunks = L // CHUNK
    return pl.pallas_call(
        functools.partial(_gdn_kernel, seq_len=L, has_s0=latent, write_state=not latent),
        grid=(n_seq,),
        in_specs=in_specs,
        out_specs=out_specs,
        out_shape=out_shape,
        scratch_shapes=[
            pltpu.VMEM((L, DN_QK), F32), pltpu.VMEM((L, DN_QK), F32), pltpu.VMEM((L, DN_V), F32),
            pltpu.VMEM((L, N_GATE_COLS), F32),
            pltpu.VMEM((N_DIR, L, DN_V), F32),
            pltpu.VMEM((N_DIR, L, DN_QK), BF16),
            pltpu.VMEM((N_DIR, L, DN_QK), BF16),
            pltpu.VMEM((N_DIR, L, DN_QK), BF16),
            pltpu.VMEM((N_DIR, L, DN_HEADS * CHUNK), BF16),
            pltpu.VMEM((N_DIR, n_chunks, N_GATE_COLS), F32),
            pltpu.VMEM((N_DIR, L, DN_V), F32),
            pltpu.VMEM((N_DIR * (DN_HEADS // 2), 2 * DK, 2 * DV), F32),
        ],
        input_output_aliases={len(args) - 1: 0},
        compiler_params=_cparams("arbitrary"),
        name="gdn_lat" if latent else "gdn_ctx",
    )(*args)


def _top2_of4(vals):
    m1 = jnp.maximum(jnp.maximum(vals[0], vals[1]), jnp.maximum(vals[2], vals[3]))
    i1 = jnp.where(vals[0] == m1, 0, jnp.where(vals[1] == m1, 1, jnp.where(vals[2] == m1, 2, 3)))
    rest = [jnp.where(i1 == j, -jnp.inf, vals[j]) for j in range(4)]
    m2 = jnp.maximum(jnp.maximum(rest[0], rest[1]), jnp.maximum(rest[2], rest[3]))
    i2 = jnp.where(rest[0] == m2, 0, jnp.where(rest[1] == m2, 1, jnp.where(rest[2] == m2, 2, 3)))
    return m1, i1, m2, i2


def _outproj_kernel(x_ref, mix_ref, mod_ref, gain_ref, wo_ref, rwt_ref, rb_ref, ut_ref,
                    x1_ref, h2_ref, ri_ref, rg_ref, cnt_ref, carry_s):
    i = pl.program_id(0)

    @pl.when(i == 0)
    def _():
        carry_s[...] = jnp.zeros_like(carry_s)

    gate1 = mod_ref[:, 2 * D_MODEL:3 * D_MODEL]
    shift2 = mod_ref[:, 3 * D_MODEL:4 * D_MODEL]
    scale2 = mod_ref[:, 4 * D_MODEL:5 * D_MODEL]
    x1 = x_ref[...] + gate1 * jnp.dot(mix_ref[...], wo_ref[...], preferred_element_type=F32)
    x1_ref[...] = x1
    h2 = _adaln(x1, gain_ref[...], shift2, scale2)
    h2_ref[...] = h2

    h_hi, h_lo = _split_bf16(h2)
    r_hi, r_lo = _split_bf16(rwt_ref[...])
    nt = (((1,), (1,)), ((), ()))
    logits = (lax.dot_general(r_hi, h_hi, nt, preferred_element_type=F32)
              + lax.dot_general(r_hi, h_lo, nt, preferred_element_type=F32)
              + lax.dot_general(r_lo, h_hi, nt, preferred_element_type=F32))
    scores = _sigmoid(logits)
    biased = scores + rb_ref[...]
    rows = [biased[e:e + 1, :] for e in range(N_EXPERTS)]
    per_group = [_top2_of4(rows[g * 4:(g + 1) * 4]) for g in range(N_GROUPS)]
    gs = [pg[0] + pg[2] for pg in per_group]
    _, gsel, _, _ = _top2_of4(gs)
    pick = lambda k: jnp.where(gsel == 0, per_group[0][k], jnp.where(gsel == 1, per_group[1][k],
                               jnp.where(gsel == 2, per_group[2][k], per_group[3][k])))
    e1 = gsel * EXPERTS_PER_GROUP + pick(1)
    e2 = gsel * EXPERTS_PER_GROUP + pick(3)
    erow = lax.broadcasted_iota(jnp.int32, scores.shape, 0)
    oh1 = erow == e1
    oh2 = erow == e2
    s1 = jnp.sum(jnp.where(oh1, scores, 0.0), axis=0, keepdims=True)
    s2 = jnp.sum(jnp.where(oh2, scores, 0.0), axis=0, keepdims=True)
    tot = s1 + s2
    grow = lax.broadcasted_iota(jnp.int32, rg_ref.shape, 0)
    rg_ref[...] = jnp.where(grow == 0, s1 / tot, s2 / tot)

    sel = (oh1 | oh2).astype(BF16)
    before = jnp.dot(sel, ut_ref[...], preferred_element_type=F32) + carry_s[...]
    r1 = jnp.sum(jnp.where(oh1, before, 0.0), axis=0, keepdims=True)
    r2 = jnp.sum(jnp.where(oh2, before, 0.0), axis=0, keepdims=True)
    carry_s[...] = carry_s[...] + jnp.sum(sel.astype(F32), axis=1, keepdims=True)
    cnt_ref[...] = carry_s[...].astype(jnp.int32)
    irow = lax.broadcasted_iota(jnp.int32, ri_ref.shape, 0)
    ri_ref[...] = jnp.where(irow == 0, e1, jnp.where(irow == 1, e2, jnp.where(
        irow == 2, r1.astype(jnp.int32), r2.astype(jnp.int32))))


def _outproj_call(l, x, mix, mod, gain, w_out, router_wt, router_b, ut):
    tok = lambda w: pl.BlockSpec((TOK_TILE, w), lambda i: (i, 0))
    per_tok = lambda r: pl.BlockSpec((r, TOK_TILE), lambda i: (0, i))
    full = lambda a: pl.BlockSpec(a.shape, lambda i: (0,) * a.ndim)
    return pl.pallas_call(
        _outproj_kernel,
        grid=(N_TOK // TOK_TILE,),
        in_specs=[
            tok(D_MODEL), tok(ATT_Q + DN_V),
            pl.BlockSpec((None, None, 1, 6 * D_MODEL), lambda i: (l, _mod_row(i), 0, 0)),
            pl.BlockSpec((None, 1, D_MODEL), lambda i: (l, 0, 0)),
            pl.BlockSpec((None, ATT_Q + DN_V, D_MODEL), lambda i: (l, 0, 0)),
            full(router_wt), full(router_b), full(ut),
        ],
        out_specs=[tok(D_MODEL), tok(D_MODEL), per_tok(4), per_tok(2), pl.BlockSpec((N_EXPERTS, 1), lambda i: (0, 0))],
        out_shape=[
            jax.ShapeDtypeStruct((N_TOK, D_MODEL), F32),
            jax.ShapeDtypeStruct((N_TOK, D_MODEL), F32),
            jax.ShapeDtypeStruct((4, N_TOK), jnp.int32),
            jax.ShapeDtypeStruct((2, N_TOK), F32),
            jax.ShapeDtypeStruct((N_EXPERTS, 1), jnp.int32),
        ],
        scratch_shapes=[pltpu.VMEM((N_EXPERTS, 1), F32)],
        compiler_params=_cparams("arbitrary"),
        name="outproj_router",
    )(x, mix, mod, gain, w_out, router_wt, router_b, ut)


def _dispatch_kernel(pos_ref, h2_ref, buf_in_ref, sorted_ref, sem):
    del buf_in_ref

    def row_copy(r, k):
        return pltpu.make_async_copy(h2_ref.at[pl.ds(r, 1), :], sorted_ref.at[pl.ds(pos_ref[0, k * TOK_TILE + r], 1), :], sem)

    def start(r, c):
        row_copy(r, 0).start()
        row_copy(r, 1).start()
        return c

    def wait(r, c):
        row_copy(r, 0).wait()
        row_copy(r, 1).wait()
        return c

    lax.fori_loop(0, TOK_TILE, start, 0, unroll=ROW_DMA_UNROLL)
    lax.fori_loop(0, TOK_TILE, wait, 0, unroll=ROW_DMA_UNROLL)


def _dispatch_call(pos_tiles, h2, zeros_sorted):
    return pl.pallas_call(
        _dispatch_kernel,
        grid=(N_TOK // TOK_TILE,),
        in_specs=[
            pl.BlockSpec((None, 1, TOP_K * TOK_TILE), lambda i: (i, 0, 0), memory_space=pltpu.SMEM),
            pl.BlockSpec((TOK_TILE, D_MODEL), lambda i: (i, 0)),
            pl.BlockSpec(memory_space=pl.ANY),
        ],
        out_specs=pl.BlockSpec(memory_space=pl.ANY),
        out_shape=jax.ShapeDtypeStruct((N_SORTED, D_MODEL), F32),
        scratch_shapes=[pltpu.SemaphoreType.DMA(())],
        input_output_aliases={2: 0},
        compiler_params=_cparams("arbitrary"),
        name="moe_dispatch",
    )(pos_tiles, h2, zeros_sorted)


def _experts_kernel(te_ref, nu_ref, x_ref, wg_ref, wu_ref, wd_ref, y_ref, wg_s, wu_s, wd_s):
    j = pl.program_id(0)
    used = j < nu_ref[0]

    @pl.when((j == 0) | (te_ref[j] != te_ref[jnp.maximum(j - 1, 0)]))
    def _():
        wg_s[...] = wg_ref[...].astype(BF16)
        wu_s[...] = wu_ref[...].astype(BF16)
        wd_s[...] = wd_ref[...].astype(BF16)

    @pl.when(used)
    def _():
        x = x_ref[...].astype(BF16)
        hid = _silu(jnp.dot(x, wg_s[...], preferred_element_type=F32)) * jnp.dot(x, wu_s[...], preferred_element_type=F32)
        y_ref[...] = jnp.dot(hid.astype(BF16), wd_s[...], preferred_element_type=F32)

    @pl.when(jnp.logical_not(used))
    def _():
        y_ref[...] = jnp.zeros_like(y_ref)


def _experts_call(l, tile_expert, n_used, xs, w_gate, w_up, w_down):
    row = lambda j, te, nu: (jnp.maximum(jnp.minimum(j, nu[0] - 1), 0), 0)
    grid_spec = pltpu.PrefetchScalarGridSpec(
        num_scalar_prefetch=2,
        grid=(N_EXP_TILES,),
        in_specs=[
            pl.BlockSpec((EXP_TILE, D_MODEL), row),
            pl.BlockSpec((None, None, D_MODEL, D_FF), lambda j, te, nu: (l, te[j], 0, 0)),
            pl.BlockSpec((None, None, D_MODEL, D_FF), lambda j, te, nu: (l, te[j], 0, 0)),
            pl.BlockSpec((None, None, D_FF, D_MODEL), lambda j, te, nu: (l, te[j], 0, 0)),
        ],
        out_specs=pl.BlockSpec((EXP_TILE, D_MODEL), lambda j, te, nu: (j, 0)),
        scratch_shapes=[pltpu.VMEM((D_MODEL, D_FF), BF16), pltpu.VMEM((D_MODEL, D_FF), BF16),
                        pltpu.VMEM((D_FF, D_MODEL), BF16)],
    )
    return pl.pallas_call(
        _experts_kernel,
        grid_spec=grid_spec,
        out_shape=jax.ShapeDtypeStruct((N_SORTED, D_MODEL), F32),
        compiler_params=_cparams("arbitrary"),
        name="moe_experts",
    )(tile_expert, n_used, xs, w_gate, w_up, w_down)


def _combine_kernel(pos_ref, x1_ref, rg_ref, mod_ref, y_ref, out_ref, buf_s, sem):
    def row_copy(r, k):
        return pltpu.make_async_copy(y_ref.at[pl.ds(pos_ref[0, k * TOK_TILE + r], 1), :], buf_s.at[k, pl.ds(r, 1), :], sem)

    def start(r, c):
        row_copy(r, 0).start()
        row_copy(r, 1).start()
        return c

    def wait(r, c):
        row_copy(r, 0).wait()
        row_copy(r, 1).wait()
        return c

    lax.fori_loop(0, TOK_TILE, start, 0, unroll=ROW_DMA_UNROLL)
    lax.fori_loop(0, TOK_TILE, wait, 0, unroll=ROW_DMA_UNROLL)
    gate2 = mod_ref[:, 5 * D_MODEL:6 * D_MODEL]
    moe = buf_s[0] * rg_ref[:, 0:1] + buf_s[1] * rg_ref[:, 1:2]
    out_ref[...] = x1_ref[...] + gate2 * moe


def _combine_call(l, pos_tiles, x1, rg, mod, y_sorted):
    tok = lambda w: pl.BlockSpec((TOK_TILE, w), lambda i: (i, 0))
    return pl.pallas_call(
        _combine_kernel,
        grid=(N_TOK // TOK_TILE,),
        in_specs=[
            pl.BlockSpec((None, 1, TOP_K * TOK_TILE), lambda i: (i, 0, 0), memory_space=pltpu.SMEM),
            tok(D_MODEL), tok(2),
            pl.BlockSpec((None, None, 1, 6 * D_MODEL), lambda i: (l, _mod_row(i), 0, 0)),
            pl.BlockSpec(memory_space=pl.ANY),
        ],
        out_specs=tok(D_MODEL),
        out_shape=jax.ShapeDtypeStruct((N_TOK, D_MODEL), F32),
        scratch_shapes=[pltpu.VMEM((TOP_K, TOK_TILE, D_MODEL), F32), pltpu.SemaphoreType.DMA(())],
        compiler_params=_cparams("arbitrary"),
        name="moe_combine",
    )(pos_tiles, x1, rg, mod, y_sorted)


def _rope_tables():
    pos = jnp.arange(DEC_SEQ)
    r = (pos // GRID_W).astype(F32)
    c = (pos % GRID_W).astype(F32)
    inv = ROPE_BASE ** (-jnp.arange(ROPE_PAIRS, dtype=F32) / ROPE_PAIRS)
    ar, ac = r[:, None] * inv, c[:, None] * inv
    cos = jnp.concatenate([jnp.cos(ar), jnp.cos(ar), jnp.cos(ac), jnp.cos(ac)], axis=-1)
    sin = jnp.concatenate([-jnp.sin(ar), jnp.sin(ar), -jnp.sin(ac), jnp.sin(ac)], axis=-1)
    return jnp.tile(cos, (1, N_HEADS)), jnp.tile(sin, (1, N_HEADS))


def _routing_layout(ri, counts):
    counts = counts.reshape(N_EXPERTS)
    padded = ((counts + EXP_TILE - 1) // EXP_TILE) * EXP_TILE
    ends = jnp.cumsum(padded)
    offs = ends - padded
    eids = jnp.arange(N_EXPERTS, dtype=jnp.int32)[:, None, None]
    pos = jnp.sum(jnp.where(ri[None, 0:2] == eids, offs[:, None, None], 0), axis=0) + ri[2:4]
    n_used = (ends[-1] // EXP_TILE).astype(jnp.int32)
    tile_start = jnp.arange(N_EXP_TILES, dtype=jnp.int32) * EXP_TILE
    tile_expert = jnp.sum((tile_start[:, None] >= ends[None, :]).astype(jnp.int32), axis=1)
    last = jnp.sum((jnp.maximum(ends[-1] - EXP_TILE, 0) >= ends).astype(jnp.int32))
    tile_expert = jnp.minimum(jnp.where(tile_start < ends[-1], tile_expert, last), N_EXPERTS - 1).astype(jnp.int32)
    pos_tiles = pos.reshape(TOP_K, N_TOK // TOK_TILE, TOK_TILE).transpose(1, 0, 2).reshape(
        N_TOK // TOK_TILE, 1, TOP_K * TOK_TILE).astype(jnp.int32)
    return pos_tiles, tile_expert, n_used.reshape(1)


def kernel(x_prompt, x_sample, cache_k, cache_v, state_dn, c, c_ctx, w_ada, b_ada, norm_attn, norm_ffn,
           w_in, conv_w, a_log, dt_bias, q_norm, k_norm, sink, o_norm, w_out, router_w, router_bias,
           w_gate, w_up, w_down):
    x = jnp.concatenate([x_prompt.reshape(N_CTX_TOK, D_MODEL), x_sample.reshape(N_LAT_TOK, D_MODEL)], axis=0)
    cond = jnp.concatenate([c_ctx[None, :], c, jnp.zeros((N_COND - 1 - DEC_BATCH, D_MODEL), F32)], axis=0)
    mod = _modulation_call(cond, w_ada, b_ada).reshape(DEPTH, N_COND, 1, 6 * D_MODEL)

    w_in_b = w_in.astype(BF16)
    w_out_b = w_out.astype(BF16)
    seg = jnp.arange(ATT_Q) // HEAD_DIM
    bd = jnp.where(seg[:, None] == seg[None, :], 1.0 / HEAD_DIM, 0.0).astype(BF16)
    qg = jnp.tile(q_norm, (1, N_HEADS)).reshape(DEPTH, 1, ATT_Q)
    kg = jnp.tile(k_norm, (1, N_KV)).reshape(DEPTH, 1, ATT_KV)
    cos, sin = _rope_tables()
    gain1 = norm_attn.reshape(DEPTH, 1, D_MODEL)
    gain2 = norm_ffn.reshape(DEPTH, 1, D_MODEL)
    pad8 = lambda a: jnp.pad(a.reshape(DEPTH, 1, N_DIR * DN_HEADS), ((0, 0), (0, 0), (0, N_GATE_COLS - N_DIR * DN_HEADS)))
    alog16, dtb16 = pad8(a_log), pad8(dt_bias)
    o_gain = o_norm.reshape(DEPTH, 1, DV)
    ck = cache_k.reshape(DEC_BATCH, DEPTH, PAST_LEN, ATT_KV)
    cv = cache_v.reshape(DEC_BATCH, DEPTH, PAST_LEN, ATT_KV)
    router_wt = router_w.T
    rb = router_bias.reshape(N_EXPERTS, 1)
    tri = jnp.arange(TOK_TILE)
    ut = (tri[:, None] < tri[None, :]).astype(BF16)
    zeros_sorted = jnp.zeros((N_SORTED, D_MODEL), F32)

    k_list, v_list, s_list = [], [], []
    for l in range(DEPTH):
        q, kv, qkvd, z, ab = _inproj_call(l, x, mod, gain1, w_in_b, bd, qg, kg, cos, sin)
        k_list.append(kv[:N_CTX_TOK, 0:ATT_KV].reshape(BATCH, SEQ, N_KV, HEAD_DIM))
        v_list.append(kv[:N_CTX_TOK, ATT_KV:].reshape(BATCH, SEQ, N_KV, HEAD_DIM))
        mix = _attn_ctx_call(sink[l], q, kv)
        mix = _attn_lat_call(l, sink[l], q, kv, ck, cv, mix)
        mix, s_c = _gdn_call(l, qkvd, z, ab, conv_w, alog16, dtb16, o_gain, state_dn, mix, latent=False)
        (mix,) = _gdn_call(l, qkvd, z, ab, conv_w, alog16, dtb16, o_gain, state_dn, mix, latent=True)
        s_list.append(s_c)
        x1, h2, ri, rg, counts = _outproj_call(l, x, mix, mod, gain2, w_out_b, router_wt, rb, ut)
        pos_tiles, tile_expert, n_used = _routing_layout(ri, counts)
        xs_sorted = _dispatch_call(pos_tiles, h2, zeros_sorted)
        y_sorted = _experts_call(l, tile_expert, n_used, xs_sorted, w_gate, w_up, w_down)
        x = _combine_call(l, pos_tiles, x1, rg.T, mod, y_sorted)

    y_prompt = x[:N_CTX_TOK].reshape(BATCH, SEQ, D_MODEL)
    y_sample = x[N_CTX_TOK:].reshape(DEC_BATCH, DEC_SEQ, D_MODEL)
    return (y_prompt, y_sample, jnp.stack(k_list, axis=1), jnp.stack(v_list, axis=1), jnp.stack(s_list, axis=1))
```

```python
import functools

import jax
import jax.numpy as jnp
from jax import lax
from jax.experimental import pallas as pl
from jax.experimental.pallas import tpu as pltpu

D_MODEL = 1024
BATCH = 32
SEQ = 256
DEPTH = 4
DEC_BATCH = 4
DEC_SEQ = 1024
PAST_LEN = 256
GRID_W = 64
N_HEADS = 8
N_KV = 2
HEAD_DIM = 64
WINDOW = 128
BLOCK = 128
ATT_SCALE = HEAD_DIM ** -0.5
ROPE_BASE = 10000.0
ROPE_PAIRS = HEAD_DIM // 4
DN_HEADS = 4
DK = 128
DV = 128
DN_CONV = 5
CHUNK = 64
N_DIR = 2
ATT_Q = N_HEADS * HEAD_DIM
ATT_KV = N_KV * HEAD_DIM
DN_QK = DN_HEADS * DK
DN_V = DN_HEADS * DV
DN_CONV_CH = 2 * DN_QK + DN_V
N_GATE_COLS = 2 * N_DIR * DN_HEADS
IN_DIM = ATT_Q + 2 * ATT_KV + DN_CONV_CH + DN_V + N_GATE_COLS
N_EXPERTS = 16
N_GROUPS = 4
EXPERTS_PER_GROUP = 4
TOP_K = 2
D_FF = 512
EPS = 1e-6

N_CTX_TOK = BATCH * SEQ
N_LAT_TOK = DEC_BATCH * DEC_SEQ
N_TOK = N_CTX_TOK + N_LAT_TOK
N_COND = 8

LANES = 128
TOK_TILE = 256
EXP_TILE = 256
N_EXP_TILES = (N_TOK * TOP_K) // EXP_TILE + N_EXPERTS
N_SORTED = N_EXP_TILES * EXP_TILE
VMEM_LIMIT = 56 * 1024 * 1024
ROW_DMA_UNROLL = 8

F32 = jnp.float32
BF16 = jnp.bfloat16


def _cparams(*sem):
    return pltpu.CompilerParams(dimension_semantics=sem, vmem_limit_bytes=VMEM_LIMIT)


def _dot(a, b):
    return jnp.dot(a.astype(BF16), b.astype(BF16), preferred_element_type=F32)


def _dot_nt(a, b):
    return lax.dot_general(a.astype(BF16), b.astype(BF16), (((1,), (1,)), ((), ())), preferred_element_type=F32)


def _dot_tn(a, b):
    return lax.dot_general(a.astype(BF16), b.astype(BF16), (((0,), (0,)), ((), ())), preferred_element_type=F32)


def _dot_f32(a, b):
    return jnp.dot(a, b, preferred_element_type=F32, precision=lax.Precision.HIGHEST)


def _dot_split(mat01, x):
    hi = x.astype(BF16)
    lo = (x - hi.astype(F32)).astype(BF16)
    return (jnp.dot(mat01, hi, preferred_element_type=F32) + jnp.dot(mat01, lo, preferred_element_type=F32))


def _sigmoid(x):
    return 1.0 / (1.0 + jnp.exp(-x))


def _silu(x):
    return x * _sigmoid(x)


def _softplus(x):
    return jnp.maximum(x, 0.0) + jnp.log(1.0 + jnp.exp(-jnp.abs(x)))


MOD_TN = 1536


def _mod_kernel(cond_ref, w_ref, b_ref, o_ref):
    c = _silu(cond_ref[...])
    o_ref[...] = _dot(c, w_ref[...]) + b_ref[...]


def _modulation_call(cond, w_ada, b_ada):
    n_col = 6 * D_MODEL
    return pl.pallas_call(
        _mod_kernel,
        grid=(DEPTH, n_col // MOD_TN),
        in_specs=[
            pl.BlockSpec((N_COND, D_MODEL), lambda l, j: (0, 0)),
            pl.BlockSpec((None, D_MODEL, MOD_TN), lambda l, j: (l, 0, j)),
            pl.BlockSpec((None, 1, MOD_TN), lambda l, j: (l, 0, j)),
        ],
        out_specs=pl.BlockSpec((None, N_COND, MOD_TN), lambda l, j: (l, 0, j)),
        out_shape=jax.ShapeDtypeStruct((DEPTH, N_COND, n_col), F32),
        compiler_params=_cparams("arbitrary", "arbitrary"),
        name="modulation",
    )(cond, w_ada, b_ada.reshape(DEPTH, 1, n_col))


def _mod_row(i):
    n_ctx = N_CTX_TOK // TOK_TILE
    per_seq = DEC_SEQ // TOK_TILE
    return jnp.where(i < n_ctx, 0, 1 + (jnp.maximum(i - n_ctx, 0)) // per_seq)


def _adaln(x, gain, shift, scale):
    ms = jnp.mean(x * x, axis=-1, keepdims=True)
    return (x * lax.rsqrt(ms + EPS) * gain) * (1.0 + scale) + shift


def _seg_rms(x, bd, gain):
    x2 = x * x
    hi = x2.astype(BF16)
    lo = (x2 - hi.astype(F32)).astype(BF16)
    ms = jnp.dot(hi, bd, preferred_element_type=F32) + jnp.dot(lo, bd, preferred_element_type=F32)
    return x * lax.rsqrt(ms + EPS) * gain


def _rope(x, cos, sin_signed):
    w = x.shape[-1]
    lane = lax.broadcasted_iota(jnp.int32, x.shape, 1)
    first = (lane % (2 * ROPE_PAIRS)) < ROPE_PAIRS
    partner = jnp.where(first, pltpu.roll(x, w - ROPE_PAIRS, axis=1), pltpu.roll(x, ROPE_PAIRS, axis=1))
    return x * cos + partner * sin_signed


def _inproj_kernel(x_ref, mod_ref, gain_ref, w_ref, bd_ref, qg_ref, kg_ref, cos_ref, sin_ref,
                   q_ref, kv_ref, qkvd_ref, z_ref, ab_ref):
    i = pl.program_id(0)
    is_lat = i >= N_CTX_TOK // TOK_TILE
    shift = mod_ref[:, 0:D_MODEL]
    scale = mod_ref[:, D_MODEL:2 * D_MODEL]
    h = _adaln(x_ref[...], gain_ref[...], shift, scale).astype(BF16)

    qa = jnp.dot(h, w_ref[:, 0:ATT_Q], preferred_element_type=F32)
    qn = _seg_rms(qa, bd_ref[...], qg_ref[...])
    ka = jnp.dot(h, w_ref[:, ATT_Q:ATT_Q + ATT_KV], preferred_element_type=F32)
    kn = _seg_rms(ka, bd_ref[0:ATT_KV, 0:ATT_KV], kg_ref[...])

    @pl.when(is_lat)
    def _():
        q_ref[...] = (_rope(qn, cos_ref[...], sin_ref[...]) * ATT_SCALE).astype(BF16)
        kv_ref[:, 0:ATT_KV] = _rope(kn, cos_ref[:, 0:ATT_KV], sin_ref[:, 0:ATT_KV])

    @pl.when(jnp.logical_not(is_lat))
    def _():
        q_ref[...] = (qn * ATT_SCALE).astype(BF16)
        kv_ref[:, 0:ATT_KV] = kn

    c0 = ATT_Q + ATT_KV
    kv_ref[:, ATT_KV:2 * ATT_KV] = jnp.dot(h, w_ref[:, c0:c0 + ATT_KV], preferred_element_type=F32)
    c0 += ATT_KV
    for j in range(DN_CONV_CH // 512):
        qkvd_ref[:, j * 512:(j + 1) * 512] = jnp.dot(
            h, w_ref[:, c0 + j * 512:c0 + (j + 1) * 512], preferred_element_type=F32).astype(BF16)
    c0 += DN_CONV_CH
    z_ref[...] = jnp.dot(h, w_ref[:, c0:c0 + DN_V], preferred_element_type=F32).astype(BF16)
    c0 += DN_V
    ab_ref[...] = jnp.dot(h, w_ref[:, c0:c0 + N_GATE_COLS], preferred_element_type=F32)


def _inproj_call(l, x, mod, gain, w_in, bd, qg, kg, cos, sin):
    n_ctx = N_CTX_TOK // TOK_TILE
    per_seq = DEC_SEQ // TOK_TILE

    def pos_map(i):
        return (jnp.maximum(i - n_ctx, 0) % per_seq, 0)

    tok = lambda w: pl.BlockSpec((TOK_TILE, w), lambda i: (i, 0))
    full = lambda a: pl.BlockSpec(a.shape, lambda i: (0,) * a.ndim)
    return pl.pallas_call(
        _inproj_kernel,
        grid=(N_TOK // TOK_TILE,),
        in_specs=[
            tok(D_MODEL),
            pl.BlockSpec((None, None, 1, 6 * D_MODEL), lambda i: (l, _mod_row(i), 0, 0)),
            pl.BlockSpec((None, 1, D_MODEL), lambda i: (l, 0, 0)),
            pl.BlockSpec((None, D_MODEL, IN_DIM), lambda i: (l, 0, 0)),
            full(bd),
            pl.BlockSpec((None, 1, ATT_Q), lambda i: (l, 0, 0)),
            pl.BlockSpec((None, 1, ATT_KV), lambda i: (l, 0, 0)),
            pl.BlockSpec((TOK_TILE, ATT_Q), pos_map),
            pl.BlockSpec((TOK_TILE, ATT_Q), pos_map),
        ],
        out_specs=[tok(ATT_Q), tok(2 * ATT_KV), tok(DN_CONV_CH), tok(DN_V), tok(N_GATE_COLS)],
        out_shape=[
            jax.ShapeDtypeStruct((N_TOK, ATT_Q), BF16),
            jax.ShapeDtypeStruct((N_TOK, 2 * ATT_KV), F32),
            jax.ShapeDtypeStruct((N_TOK, DN_CONV_CH), BF16),
            jax.ShapeDtypeStruct((N_TOK, DN_V), BF16),
            jax.ShapeDtypeStruct((N_TOK, N_GATE_COLS), F32),
        ],
        compiler_params=_cparams("arbitrary"),
        name="inproj",
    )(x, mod, gain, w_in, bd, qg, kg, cos, sin)


def _dup_half(x, g):
    lane = lax.broadcasted_iota(jnp.int32, x.shape, 1)
    lo = lane < HEAD_DIM
    xr = pltpu.roll(x, HEAD_DIM, axis=1)
    return jnp.where(lo, x, xr) if g == 0 else jnp.where(lo, xr, x)


def _attend(sink_ref, q_ref, o_ref, key_sets):
    rows = q_ref.shape[0]
    lane = lax.broadcasted_iota(jnp.int32, (rows, LANES), 1)
    lo = lane < HEAD_DIM
    for g in range(N_KV):
        ks = [(_dup_half(k, g).astype(BF16), _dup_half(v, g).astype(BF16), valid) for k, v, valid in key_sets]
        for jj in range(2):
            j = g * 2 + jj
            qp = q_ref[:, j * LANES:(j + 1) * LANES]
            outs = []
            for e in range(2):
                qm = jnp.where(lo if e == 0 else jnp.logical_not(lo), qp, jnp.zeros_like(qp))
                sk = sink_ref[2 * j + e]
                scores = []
                m = jnp.full((rows, 1), sk, F32)
                for k, _, valid in ks:
                    s = _dot_nt(qm, k)
                    if valid is not None:
                        s = jnp.where(valid, s, -jnp.inf)
                    scores.append(s)
                    m = jnp.maximum(m, jnp.max(s, axis=-1, keepdims=True))
                den = jnp.exp(sk - m)
                acc = jnp.zeros((rows, LANES), F32)
                for s, (_, v, _) in zip(scores, ks):
                    p = jnp.exp(s - m)
                    den = den + jnp.sum(p, axis=-1, keepdims=True)
                    acc = acc + _dot(p, v)
                outs.append(acc / den)
            o_ref[:, j * LANES:(j + 1) * LANES] = jnp.where(lo, outs[0], outs[1]).astype(o_ref.dtype)


def _attn_ctx_kernel(sink_ref, q_ref, kv_ref, o_ref):
    _attend(sink_ref, q_ref, o_ref, [(kv_ref[:, 0:ATT_KV], kv_ref[:, ATT_KV:2 * ATT_KV], None)])


def _attn_lat_kernel(sink_ref, q_ref, kv_ref, ck_ref, cv_ref, mix_in_ref, o_ref):
    del mix_in_ref
    n = pl.program_id(1)
    span = 3 * BLOCK
    start = pl.multiple_of(jnp.clip((n - 1) * BLOCK, 0, DEC_SEQ - span), BLOCK)
    kw = kv_ref[pl.ds(start, span), 0:ATT_KV]
    vw = kv_ref[pl.ds(start, span), ATT_KV:2 * ATT_KV]
    qpos = n * BLOCK + lax.broadcasted_iota(jnp.int32, (BLOCK, span), 0)
    kpos = start + lax.broadcasted_iota(jnp.int32, (BLOCK, span), 1)
    valid = jnp.abs(kpos - qpos) <= WINDOW
    _attend(sink_ref, q_ref, o_ref, [(kw, vw, valid), (ck_ref[...], cv_ref[...], None)])


def _attn_ctx_call(sink_l, q, kv):
    return pl.pallas_call(
        _attn_ctx_kernel,
        grid=(BATCH,),
        in_specs=[
            pl.BlockSpec(memory_space=pltpu.SMEM),
            pl.BlockSpec((SEQ, ATT_Q), lambda b: (b, 0)),
            pl.BlockSpec((SEQ, 2 * ATT_KV), lambda b: (b, 0)),
        ],
        out_specs=pl.BlockSpec((SEQ, ATT_Q), lambda b: (b, 0)),
        out_shape=jax.ShapeDtypeStruct((N_TOK, ATT_Q + DN_V), BF16),
        compiler_params=_cparams("arbitrary"),
        name="attn_ctx",
    )(sink_l, q, kv)


def _attn_lat_call(l, sink_l, q, kv, cache_k, cache_v, mix):
    nb = DEC_SEQ // BLOCK
    q0 = N_CTX_TOK // BLOCK
    s0 = N_CTX_TOK // DEC_SEQ
    return pl.pallas_call(
        _attn_lat_kernel,
        grid=(DEC_BATCH, nb),
        in_specs=[
            pl.BlockSpec(memory_space=pltpu.SMEM),
            pl.BlockSpec((BLOCK, ATT_Q), lambda b, n: (q0 + b * nb + n, 0)),
            pl.BlockSpec((DEC_SEQ, 2 * ATT_KV), lambda b, n: (s0 + b, 0)),
            pl.BlockSpec((None, None, PAST_LEN, ATT_KV), lambda b, n: (b, l, 0, 0)),
            pl.BlockSpec((None, None, PAST_LEN, ATT_KV), lambda b, n: (b, l, 0, 0)),
            pl.BlockSpec(memory_space=pl.ANY),
        ],
        out_specs=pl.BlockSpec((BLOCK, ATT_Q), lambda b, n: (q0 + b * nb + n, 0)),
        out_shape=jax.ShapeDtypeStruct((N_TOK, ATT_Q + DN_V), BF16),
        input_output_aliases={5: 0},
        compiler_params=_cparams("arbitrary", "arbitrary"),
        name="attn_lat",
    )(sink_l, q, kv, cache_k, cache_v, mix)


INV_PASSES = 1
GDN_UNROLL = 4
GDN_CTX_SEQS = 2


def _split_bf16(x):
    hi = x.astype(BF16)
    return hi, (x - hi.astype(F32)).astype(BF16)


def _block_diag(x, width):
    n = x.shape[1] // width
    blk = lax.broadcasted_iota(jnp.int32, x.shape, 1) // width
    zero = jnp.zeros_like(x)
    return jnp.concatenate([jnp.where(blk == h, x, zero) for h in range(n)], axis=0)


def _mm_heads(ts, xs, width):
    if INV_PASSES == 1:
        bds = [_block_diag(x.astype(BF16), width) for x in xs]
        return [jnp.dot(t.astype(BF16), bd, preferred_element_type=F32) for t, bd in zip(ts, bds)]
    tsp = [_split_bf16(t) for t in ts]
    xsp = [_split_bf16(x) for x in xs]
    bd_hi = [_block_diag(x_hi, width) for x_hi, _ in xsp]
    bd_lo = [_block_diag(x_lo, width) for _, x_lo in xsp]
    return [jnp.dot(t_hi, bh, preferred_element_type=F32) + jnp.dot(t_lo, bh, preferred_element_type=F32)
            + jnp.dot(t_hi, bl, preferred_element_type=F32) for (t_hi, t_lo), bh, bl in zip(tsp, bd_hi, bd_lo)]


def _unit_tri_inverse(mats, row, col):
    eye = (row == col).astype(F32)
    a8 = [jnp.where((row // 8) == (col // 8), a, 0.0) for a in mats]
    a8_2 = _mm_heads(a8, a8, CHUNK)
    a8_4 = _mm_heads(a8_2, a8_2, CHUNK)
    ts = [eye - a for a in a8]
    ts = [t + p for t, p in zip(ts, _mm_heads(ts, a8_2, CHUNK))]
    ts = [t + p for t, p in zip(ts, _mm_heads(ts, a8_4, CHUNK))]
    b = 8
    while b < CHUNK:
        level = ((row // (2 * b)) == (col // (2 * b))) & ((row // b) != (col // b))
        off = [jnp.where(level, a, 0.0) for a in mats]
        ts = [t - p for t, p in zip(ts, _mm_heads(_mm_heads(ts, off, CHUNK), ts, CHUNK))]
        b *= 2
    return ts


def _bcast_cols(x, first, width):
    rows = x.shape[0]
    if width == LANES:
        return jnp.concatenate([jnp.broadcast_to(x[:, first + h:first + h + 1], (rows, width)) for h in range(DN_HEADS)], axis=1)
    blk = lax.broadcasted_iota(jnp.int32, (rows, DN_HEADS * width), 1) // width
    out = jnp.broadcast_to(x[:, first:first + 1], (rows, DN_HEADS * width))
    for h in range(1, DN_HEADS):
        out = jnp.where(blk == h, jnp.broadcast_to(x[:, first + h:first + h + 1], (rows, DN_HEADS * width)), out)
    return out


def _gdn_kernel(*refs, seq_len, n_seq, has_s0, write_state):
    qkvd_ref, z_ref, ab_ref, convw_ref, alog_ref, dtb_ref, ogain_ref = refs[:7]
    pos = 7
    s0_ref = None
    if has_s0:
        s0_ref = refs[pos]
        pos += 1
    pos += 1
    dn_ref = refs[pos]
    pos += 1
    sfin_ref = None
    if write_state:
        sfin_ref = refs[pos]
        pos += 1
    qn_s, kn_s, vn_s, gb_s, u0_s, w_s, qd_s, kd_s, qk_s, gt_s, o_s, st_s, xpad_s = refs[pos:]
    L = seq_len
    n_chunks = L // CHUNK
    n_pairs = DN_HEADS // 2
    pair_w = 2 * DK
    pad = (DN_CONV - 1) // 2

    halo = 8

    def finish(y, kind):
        y = _silu(y)
        if kind == 0:
            return y * lax.rsqrt(jnp.sum(y * y, axis=-1, keepdims=True) + EPS) * (DK ** -0.5)
        if kind == 1:
            return y * lax.rsqrt(jnp.sum(y * y, axis=-1, keepdims=True) + EPS)
        return y

    xpad_s[0:halo, :] = jnp.zeros((halo, LANES), F32)
    xpad_s[halo + L:2 * halo + L, :] = jnp.zeros((halo, LANES), F32)
    for kind, dst in enumerate((qn_s, kn_s, vn_s)):
        def conv_head(head, carry, kind=kind, dst=dst):
            cs = pl.ds(pl.multiple_of((kind * DN_HEADS + head) * LANES, LANES), LANES)
            hs = pl.ds(pl.multiple_of(head * DK, DK), DK)
            for s in range(n_seq):
                r0 = s * L
                for rb in range(0, L, SEQ):
                    xpad_s[halo + rb:halo + rb + SEQ, :] = qkvd_ref[r0 + rb:r0 + rb + SEQ, cs].astype(F32)
                for rb in range(0, L, SEQ):
                    acc = None
                    for i in range(DN_CONV):
                        lo = halo + rb + i - pad
                        term = xpad_s[lo:lo + SEQ, :] * convw_ref[i:i + 1, cs]
                        acc = term if acc is None else acc + term
                    dst[r0 + rb:r0 + rb + SEQ, hs] = finish(acc, kind)
            return carry

        lax.fori_loop(0, DN_HEADS, conv_head, 0)

    ab = ab_ref[...]
    glane = lax.broadcasted_iota(jnp.int32, ab.shape, 1)
    g = -jnp.exp(alog_ref[...]) * _softplus(ab + dtb_ref[...])
    gb_s[...] = jnp.where(glane < N_DIR * DN_HEADS, g, _sigmoid(ab))

    for s in range(n_seq):
        for d in range(N_DIR):
            for p in range(n_pairs):
                idx = (s * N_DIR + d) * n_pairs + p
                st_s[idx] = jnp.zeros((pair_w, pair_w), F32)
                if has_s0:
                    st_s[idx, 0:DK, 0:DV] = s0_ref[s, d, 2 * p]
                    st_s[idx, DK:2 * DK, DV:2 * DV] = s0_ref[s, d, 2 * p + 1]

    row = lax.broadcasted_iota(jnp.int32, (CHUNK, DN_HEADS * CHUNK), 0)
    col = lax.broadcasted_iota(jnp.int32, (CHUNK, DN_HEADS * CHUNK), 1) % CHUNK
    before_incl = [col <= row, col >= row]
    before_strict = [col < row, col > row]
    after_strict01 = [m.astype(F32) for m in before_strict]
    r64 = lax.broadcasted_iota(jnp.int32, (CHUNK, CHUNK), 0)
    c64 = lax.broadcasted_iota(jnp.int32, (CHUNK, CHUNK), 1)
    tri01 = [(c64 <= r64).astype(BF16), (c64 >= r64).astype(BF16)]

    def prep_step(n, carry):
        a_mats, v_betas, kb_egcs, slots = [], [], [], []
        for j in range(GDN_UNROLL):
            c = n * GDN_UNROLL + j
            rs = pl.ds(pl.multiple_of(c * CHUNK, CHUNK), CHUNK)
            gbc = gb_s[rs, :]
            q = qn_s[rs, :]
            k = kn_s[rs, :]
            v = vn_s[rs, :]
            k_bd = _block_diag(k.astype(BF16), DK)
            betas = [_bcast_cols(gbc, N_DIR * DN_HEADS + d * DN_HEADS, DK) for d in range(N_DIR)]
            kbs = [k * beta for beta in betas]
            kk_qk = _dot_nt(jnp.concatenate(kbs + [q], axis=0), k_bd)
            qk_raw = kk_qk[N_DIR * CHUNK:(N_DIR + 1) * CHUNK]
            for d in range(N_DIR):
                first = d * DN_HEADS
                gc_all = _dot_split(tri01[d], gbc)
                last = CHUNK - 1 if d == 0 else 0
                gc_last = gc_all[last:last + 1, :]
                gt_s[d, pl.ds(c, 1), :] = jnp.exp(gc_last)
                e_gc = _bcast_cols(jnp.exp(gc_all), first, DK)
                e_kd = _bcast_cols(jnp.exp(gc_last - gc_all), first, DK)
                gdiff = _dot_split(tri01[d], _bcast_cols(gbc, first, CHUNK) * after_strict01[d])
                decay = jnp.where(before_incl[d], jnp.exp(gdiff), 0.0)
                a_mats.append(jnp.where(before_strict[d], kk_qk[d * CHUNK:(d + 1) * CHUNK] * decay, 0.0))
                qk_s[d, rs, :] = jnp.where(before_incl[d], qk_raw * decay, 0.0).astype(BF16)
                qd_s[d, rs, :] = (q * e_gc).astype(BF16)
                kd_s[d, rs, :] = (k * e_kd).astype(BF16)
                v_betas.append(v * betas[d])
                kb_egcs.append(kbs[d] * e_gc)
                slots.append((d, rs))
        t_invs = _unit_tri_inverse(a_mats, row, col)
        for (d, rs), u0, w in zip(slots, _mm_heads(t_invs, v_betas, DV), _mm_heads(t_invs, kb_egcs, DK)):
            u0_s[d, rs, :] = u0
            w_s[d, rs, :] = w.astype(BF16)
        return carry

    lax.fori_loop(0, n_seq * n_chunks // GDN_UNROLL, prep_step, 0)

    plane = lax.broadcasted_iota(jnp.int32, (1, pair_w), 1)
    srow = lax.broadcasted_iota(jnp.int32, (pair_w, pair_w), 0) // DK
    scol = lax.broadcasted_iota(jnp.int32, (pair_w, pair_w), 1) // DV
    same_head = srow == scol

    def scan_step(n, carry):
        probs = []
        for s in range(n_seq):
            for d in range(N_DIR):
                c = s * n_chunks + (n if d == 0 else n_chunks - 1 - n)
                rs = pl.ds(pl.multiple_of(c * CHUNK, CHUNK), CHUNK)
                gt = gt_s[d, pl.ds(c, 1), :]
                for p in range(n_pairs):
                    c0 = d * DN_HEADS + 2 * p
                    g_tot = jnp.where(plane < DV, gt[:, c0:c0 + 1], gt[:, c0 + 1:c0 + 2])
                    probs.append((d, rs, p, (s * N_DIR + d) * n_pairs + p, slice(p * pair_w, (p + 1) * pair_w), g_tot))
        s_prev = [st_s[idx] for _, _, _, idx, _, _ in probs]
        s_b = [s.astype(BF16) for s in s_prev]
        ws = [jnp.dot(w_s[d, rs, ps], sb, preferred_element_type=F32) for (d, rs, _, _, ps, _), sb in zip(probs, s_b)]
        u_b = [(u0_s[d, rs, ps] - w).astype(BF16) for (d, rs, _, _, ps, _), w in zip(probs, ws)]
        upd = [lax.dot_general(kd_s[d, rs, ps], u, (((0,), (0,)), ((), ())), preferred_element_type=F32)
               for (d, rs, _, _, ps, _), u in zip(probs, u_b)]
        for (_, _, _, idx, _, g_tot), s, up in zip(probs, s_prev, upd):
            st_s[idx] = s * g_tot + jnp.where(same_head, up, 0.0)
        for (d, rs, p, _, ps, _), sb, u in zip(probs, s_b, u_b):
            o_s[d, rs, ps] = (jnp.dot(qd_s[d, rs, ps], sb, preferred_element_type=F32)
                              + jnp.dot(qk_s[d, rs, p * 2 * CHUNK:(p + 1) * 2 * CHUNK], _block_diag(u, DV),
                                        preferred_element_type=F32))
        return carry

    lax.fori_loop(0, n_chunks, scan_step, 0)

    for h in range(DN_HEADS):
        hs = slice(h * DV, (h + 1) * DV)
        o = o_s[0, :, hs] + o_s[1, :, hs]
        y = o * lax.rsqrt(jnp.mean(o * o, axis=-1, keepdims=True) + EPS) * ogain_ref[...]
        dn_ref[:, hs] = (y * _silu(z_ref[:, hs].astype(F32))).astype(dn_ref.dtype)
    if write_state:
        for s in range(n_seq):
            for d in range(N_DIR):
                for p in range(n_pairs):
                    idx = (s * N_DIR + d) * n_pairs + p
                    sfin_ref[s, d, 2 * p] = st_s[idx, 0:DK, 0:DV]
                    sfin_ref[s, d, 2 * p + 1] = st_s[idx, DK:2 * DK, DV:2 * DV]


def _gdn_call(l, qkvd, z, ab, conv_w, alog16, dtb16, o_gain, state_dn, mix, *, latent):
    L = DEC_SEQ if latent else SEQ
    n_seq = 1 if latent else GDN_CTX_SEQS
    n_steps = (DEC_BATCH if latent else BATCH) // n_seq
    rows = n_seq * L
    base = N_CTX_TOK // rows if latent else 0
    seq = lambda w: pl.BlockSpec((rows, w), lambda b: (base + b, 0))
    in_specs = [
        seq(DN_CONV_CH), seq(DN_V), seq(N_GATE_COLS),
        pl.BlockSpec((None, DN_CONV, DN_CONV_CH), lambda b: (l, 0, 0)),
        pl.BlockSpec((None, 1, N_GATE_COLS), lambda b: (l, 0, 0)),
        pl.BlockSpec((None, 1, N_GATE_COLS), lambda b: (l, 0, 0)),
        pl.BlockSpec((None, 1, DV), lambda b: (l, 0, 0)),
    ]
    args = [qkvd, z, ab, conv_w, alog16, dtb16, o_gain]
    out_specs = [pl.BlockSpec((rows, DN_V), lambda b: (base + b, 1))]
    out_shape = [jax.ShapeDtypeStruct((N_TOK, ATT_Q + DN_V), BF16)]
    if latent:
        in_specs.append(pl.BlockSpec((n_seq, None, N_DIR, DN_HEADS, DK, DV), lambda b: (b, l, 0, 0, 0, 0)))
        args.append(state_dn)
    in_specs.append(pl.BlockSpec(memory_space=pl.ANY))
    args.append(mix)
    if not latent:
        out_specs.append(pl.BlockSpec((n_seq, N_DIR, DN_HEADS, DK, DV), lambda b: (b, 0, 0, 0, 0)))
        out_shape.append(jax.ShapeDtypeStruct((BATCH, N_DIR, DN_HEADS, DK, DV), F32))
    n_chunks = rows // CHUNK
    return pl.pallas_call(
        functools.partial(_gdn_kernel, seq_len=L, n_seq=n_seq, has_s0=latent, write_state=not latent),
        grid=(n_steps,),
        in_specs=in_specs,
        out_specs=out_specs,
        out_shape=out_shape,
        scratch_shapes=[
            pltpu.VMEM((rows, DN_QK), F32), pltpu.VMEM((rows, DN_QK), F32), pltpu.VMEM((rows, DN_V), F32),
            pltpu.VMEM((rows, N_GATE_COLS), F32),
            pltpu.VMEM((N_DIR, rows, DN_V), F32),
            pltpu.VMEM((N_DIR, rows, DN_QK), BF16),
            pltpu.VMEM((N_DIR, rows, DN_QK), BF16),
            pltpu.VMEM((N_DIR, rows, DN_QK), BF16),
            pltpu.VMEM((N_DIR, rows, DN_HEADS * CHUNK), BF16),
            pltpu.VMEM((N_DIR, n_chunks, N_GATE_COLS), F32),
            pltpu.VMEM((N_DIR, rows, DN_V), F32),
            pltpu.VMEM((n_seq * N_DIR * (DN_HEADS // 2), 2 * DK, 2 * DV), F32),
            pltpu.VMEM((L + 16, LANES), F32),
        ],
        input_output_aliases={len(args) - 1: 0},
        compiler_params=_cparams("arbitrary"),
        name="gdn_lat" if latent else "gdn_ctx",
    )(*args)


def _top2_of4(vals):
    m1 = jnp.maximum(jnp.maximum(vals[0], vals[1]), jnp.maximum(vals[2], vals[3]))
    i1 = jnp.where(vals[0] == m1, 0, jnp.where(vals[1] == m1, 1, jnp.where(vals[2] == m1, 2, 3)))
    rest = [jnp.where(i1 == j, -jnp.inf, vals[j]) for j in range(4)]
    m2 = jnp.maximum(jnp.maximum(rest[0], rest[1]), jnp.maximum(rest[2], rest[3]))
    i2 = jnp.where(rest[0] == m2, 0, jnp.where(rest[1] == m2, 1, jnp.where(rest[2] == m2, 2, 3)))
    return m1, i1, m2, i2


def _outproj_kernel(x_ref, mix_ref, mod_ref, gain_ref, wo_ref, rwt_ref, rb_ref, ut_ref,
                    x1_ref, h2_ref, ri_ref, rg_ref, cnt_ref, carry_s):
    i = pl.program_id(0)

    @pl.when(i == 0)
    def _():
        carry_s[...] = jnp.zeros_like(carry_s)

    gate1 = mod_ref[:, 2 * D_MODEL:3 * D_MODEL]
    shift2 = mod_ref[:, 3 * D_MODEL:4 * D_MODEL]
    scale2 = mod_ref[:, 4 * D_MODEL:5 * D_MODEL]
    x1 = x_ref[...] + gate1 * jnp.dot(mix_ref[...], wo_ref[...], preferred_element_type=F32)
    x1_ref[...] = x1
    h2 = _adaln(x1, gain_ref[...], shift2, scale2)
    h2_ref[...] = h2

    h_hi, h_lo = _split_bf16(h2)
    r_hi, r_lo = _split_bf16(rwt_ref[...])
    nt = (((1,), (1,)), ((), ()))
    logits = (lax.dot_general(r_hi, h_hi, nt, preferred_element_type=F32)
              + lax.dot_general(r_hi, h_lo, nt, preferred_element_type=F32)
              + lax.dot_general(r_lo, h_hi, nt, preferred_element_type=F32))
    scores = _sigmoid(logits)
    biased = scores + rb_ref[...]
    rows = [biased[e:e + 1, :] for e in range(N_EXPERTS)]
    per_group = [_top2_of4(rows[g * 4:(g + 1) * 4]) for g in range(N_GROUPS)]
    gs = [pg[0] + pg[2] for pg in per_group]
    _, gsel, _, _ = _top2_of4(gs)
    pick = lambda k: jnp.where(gsel == 0, per_group[0][k], jnp.where(gsel == 1, per_group[1][k],
                               jnp.where(gsel == 2, per_group[2][k], per_group[3][k])))
    e1 = gsel * EXPERTS_PER_GROUP + pick(1)
    e2 = gsel * EXPERTS_PER_GROUP + pick(3)
    erow = lax.broadcasted_iota(jnp.int32, scores.shape, 0)
    oh1 = erow == e1
    oh2 = erow == e2
    s1 = jnp.sum(jnp.where(oh1, scores, 0.0), axis=0, keepdims=True)
    s2 = jnp.sum(jnp.where(oh2, scores, 0.0), axis=0, keepdims=True)
    tot = s1 + s2
    grow = lax.broadcasted_iota(jnp.int32, rg_ref.shape, 0)
    rg_ref[...] = jnp.where(grow == 0, s1 / tot, s2 / tot)

    sel = (oh1 | oh2).astype(BF16)
    before = jnp.dot(sel, ut_ref[...], preferred_element_type=F32) + carry_s[...]
    r1 = jnp.sum(jnp.where(oh1, before, 0.0), axis=0, keepdims=True)
    r2 = jnp.sum(jnp.where(oh2, before, 0.0), axis=0, keepdims=True)
    carry_s[...] = carry_s[...] + jnp.sum(sel.astype(F32), axis=1, keepdims=True)
    cnt_ref[...] = carry_s[...].astype(jnp.int32)
    irow = lax.broadcasted_iota(jnp.int32, ri_ref.shape, 0)
    ri_ref[...] = jnp.where(irow == 0, e1, jnp.where(irow == 1, e2, jnp.where(
        irow == 2, r1.astype(jnp.int32), r2.astype(jnp.int32))))


def _outproj_call(l, x, mix, mod, gain, w_out, router_wt, router_b, ut):
    tok = lambda w: pl.BlockSpec((TOK_TILE, w), lambda i: (i, 0))
    per_tok = lambda r: pl.BlockSpec((r, TOK_TILE), lambda i: (0, i))
    full = lambda a: pl.BlockSpec(a.shape, lambda i: (0,) * a.ndim)
    return pl.pallas_call(
        _outproj_kernel,
        grid=(N_TOK // TOK_TILE,),
        in_specs=[
            tok(D_MODEL), tok(ATT_Q + DN_V),
            pl.BlockSpec((None, None, 1, 6 * D_MODEL), lambda i: (l, _mod_row(i), 0, 0)),
            pl.BlockSpec((None, 1, D_MODEL), lambda i: (l, 0, 0)),
            pl.BlockSpec((None, ATT_Q + DN_V, D_MODEL), lambda i: (l, 0, 0)),
            full(router_wt), full(router_b), full(ut),
        ],
        out_specs=[tok(D_MODEL), tok(D_MODEL), per_tok(4), per_tok(2), pl.BlockSpec((N_EXPERTS, 1), lambda i: (0, 0))],
        out_shape=[
            jax.ShapeDtypeStruct((N_TOK, D_MODEL), F32),
            jax.ShapeDtypeStruct((N_TOK, D_MODEL), F32),
            jax.ShapeDtypeStruct((4, N_TOK), jnp.int32),
            jax.ShapeDtypeStruct((2, N_TOK), F32),
            jax.ShapeDtypeStruct((N_EXPERTS, 1), jnp.int32),
        ],
        scratch_shapes=[pltpu.VMEM((N_EXPERTS, 1), F32)],
        compiler_params=_cparams("arbitrary"),
        name="outproj_router",
    )(x, mix, mod, gain, w_out, router_wt, router_b, ut)


def _dispatch_kernel(pos_ref, h2_ref, buf_in_ref, sorted_ref, sem):
    del buf_in_ref

    def row_copy(r, k):
        return pltpu.make_async_copy(h2_ref.at[pl.ds(r, 1), :], sorted_ref.at[pl.ds(pos_ref[0, k * TOK_TILE + r], 1), :], sem)

    def start(r, c):
        row_copy(r, 0).start()
        row_copy(r, 1).start()
        return c

    def wait(r, c):
        row_copy(r, 0).wait()
        row_copy(r, 1).wait()
        return c

    lax.fori_loop(0, TOK_TILE, start, 0, unroll=ROW_DMA_UNROLL)
    lax.fori_loop(0, TOK_TILE, wait, 0, unroll=ROW_DMA_UNROLL)


def _dispatch_call(pos_tiles, h2, zeros_sorted):
    return pl.pallas_call(
        _dispatch_kernel,
        grid=(N_TOK // TOK_TILE,),
        in_specs=[
            pl.BlockSpec((None, 1, TOP_K * TOK_TILE), lambda i: (i, 0, 0), memory_space=pltpu.SMEM),
            pl.BlockSpec((TOK_TILE, D_MODEL), lambda i: (i, 0)),
            pl.BlockSpec(memory_space=pl.ANY),
        ],
        out_specs=pl.BlockSpec(memory_space=pl.ANY),
        out_shape=jax.ShapeDtypeStruct((N_SORTED, D_MODEL), F32),
        scratch_shapes=[pltpu.SemaphoreType.DMA(())],
        input_output_aliases={2: 0},
        compiler_params=_cparams("arbitrary"),
        name="moe_dispatch",
    )(pos_tiles, h2, zeros_sorted)


def _experts_kernel(te_ref, nu_ref, x_ref, wg_ref, wu_ref, wd_ref, y_ref, wg_s, wu_s, wd_s):
    j = pl.program_id(0)
    used = j < nu_ref[0]

    @pl.when((j == 0) | (te_ref[j] != te_ref[jnp.maximum(j - 1, 0)]))
    def _():
        wg_s[...] = wg_ref[...].astype(BF16)
        wu_s[...] = wu_ref[...].astype(BF16)
        wd_s[...] = wd_ref[...].astype(BF16)

    @pl.when(used)
    def _():
        x = x_ref[...].astype(BF16)
        hid = _silu(jnp.dot(x, wg_s[...], preferred_element_type=F32)) * jnp.dot(x, wu_s[...], preferred_element_type=F32)
        y_ref[...] = jnp.dot(hid.astype(BF16), wd_s[...], preferred_element_type=F32)

    @pl.when(jnp.logical_not(used))
    def _():
        y_ref[...] = jnp.zeros_like(y_ref)


def _experts_call(l, tile_expert, n_used, xs, w_gate, w_up, w_down):
    row = lambda j, te, nu: (jnp.maximum(jnp.minimum(j, nu[0] - 1), 0), 0)
    grid_spec = pltpu.PrefetchScalarGridSpec(
        num_scalar_prefetch=2,
        grid=(N_EXP_TILES,),
        in_specs=[
            pl.BlockSpec((EXP_TILE, D_MODEL), row),
            pl.BlockSpec((None, None, D_MODEL, D_FF), lambda j, te, nu: (l, te[j], 0, 0)),
            pl.BlockSpec((None, None, D_MODEL, D_FF), lambda j, te, nu: (l, te[j], 0, 0)),
            pl.BlockSpec((None, None, D_FF, D_MODEL), lambda j, te, nu: (l, te[j], 0, 0)),
        ],
        out_specs=pl.BlockSpec((EXP_TILE, D_MODEL), lambda j, te, nu: (j, 0)),
        scratch_shapes=[pltpu.VMEM((D_MODEL, D_FF), BF16), pltpu.VMEM((D_MODEL, D_FF), BF16),
                        pltpu.VMEM((D_FF, D_MODEL), BF16)],
    )
    return pl.pallas_call(
        _experts_kernel,
        grid_spec=grid_spec,
        out_shape=jax.ShapeDtypeStruct((N_SORTED, D_MODEL), F32),
        compiler_params=_cparams("arbitrary"),
        name="moe_experts",
    )(tile_expert, n_used, xs, w_gate, w_up, w_down)


def _combine_kernel(pos_ref, x1_ref, rg_ref, mod_ref, y_ref, out_ref, buf_s, sem):
    def row_copy(r, k):
        return pltpu.make_async_copy(y_ref.at[pl.ds(pos_ref[0, k * TOK_TILE + r], 1), :], buf_s.at[k, pl.ds(r, 1), :], sem)

    def start(r, c):
        row_copy(r, 0).start()
        row_copy(r, 1).start()
        return c

    def wait(r, c):
        row_copy(r, 0).wait()
        row_copy(r, 1).wait()
        return c

    lax.fori_loop(0, TOK_TILE, start, 0, unroll=ROW_DMA_UNROLL)
    lax.fori_loop(0, TOK_TILE, wait, 0, unroll=ROW_DMA_UNROLL)
    gate2 = mod_ref[:, 5 * D_MODEL:6 * D_MODEL]
    moe = buf_s[0] * rg_ref[:, 0:1] + buf_s[1] * rg_ref[:, 1:2]
    out_ref[...] = x1_ref[...] + gate2 * moe


def _combine_call(l, pos_tiles, x1, rg, mod, y_sorted):
    tok = lambda w: pl.BlockSpec((TOK_TILE, w), lambda i: (i, 0))
    return pl.pallas_call(
        _combine_kernel,
        grid=(N_TOK // TOK_TILE,),
        in_specs=[
            pl.BlockSpec((None, 1, TOP_K * TOK_TILE), lambda i: (i, 0, 0), memory_space=pltpu.SMEM),
            tok(D_MODEL), tok(2),
            pl.BlockSpec((None, None, 1, 6 * D_MODEL), lambda i: (l, _mod_row(i), 0, 0)),
            pl.BlockSpec(memory_space=pl.ANY),
        ],
        out_specs=tok(D_MODEL),
        out_shape=jax.ShapeDtypeStruct((N_TOK, D_MODEL), F32),
        scratch_shapes=[pltpu.VMEM((TOP_K, TOK_TILE, D_MODEL), F32), pltpu.SemaphoreType.DMA(())],
        compiler_params=_cparams("arbitrary"),
        name="moe_combine",
    )(pos_tiles, x1, rg, mod, y_sorted)


def _rope_tables():
    pos = jnp.arange(DEC_SEQ)
    r = (pos // GRID_W).astype(F32)
    c = (pos % GRID_W).astype(F32)
    inv = ROPE_BASE ** (-jnp.arange(ROPE_PAIRS, dtype=F32) / ROPE_PAIRS)
    ar, ac = r[:, None] * inv, c[:, None] * inv
    cos = jnp.concatenate([jnp.cos(ar), jnp.cos(ar), jnp.cos(ac), jnp.cos(ac)], axis=-1)
    sin = jnp.concatenate([-jnp.sin(ar), jnp.sin(ar), -jnp.sin(ac), jnp.sin(ac)], axis=-1)
    return jnp.tile(cos, (1, N_HEADS)), jnp.tile(sin, (1, N_HEADS))


def _routing_layout(ri, counts):
    counts = counts.reshape(N_EXPERTS)
    padded = ((counts + EXP_TILE - 1) // EXP_TILE) * EXP_TILE
    ends = jnp.cumsum(padded)
    offs = ends - padded
    eids = jnp.arange(N_EXPERTS, dtype=jnp.int32)[:, None, None]
    pos = jnp.sum(jnp.where(ri[None, 0:2] == eids, offs[:, None, None], 0), axis=0) + ri[2:4]
    n_used = (ends[-1] // EXP_TILE).astype(jnp.int32)
    tile_start = jnp.arange(N_EXP_TILES, dtype=jnp.int32) * EXP_TILE
    tile_expert = jnp.sum((tile_start[:, None] >= ends[None, :]).astype(jnp.int32), axis=1)
    last = jnp.sum((jnp.maximum(ends[-1] - EXP_TILE, 0) >= ends).astype(jnp.int32))
    tile_expert = jnp.minimum(jnp.where(tile_start < ends[-1], tile_expert, last), N_EXPERTS - 1).astype(jnp.int32)
    pos_tiles = pos.reshape(TOP_K, N_TOK // TOK_TILE, TOK_TILE).transpose(1, 0, 2).reshape(
        N_TOK // TOK_TILE, 1, TOP_K * TOK_TILE).astype(jnp.int32)
    return pos_tiles, tile_expert, n_used.reshape(1)


def kernel(x_prompt, x_sample, cache_k, cache_v, state_dn, c, c_ctx, w_ada, b_ada, norm_attn, norm_ffn,
           w_in, conv_w, a_log, dt_bias, q_norm, k_norm, sink, o_norm, w_out, router_w, router_bias,
           w_gate, w_up, w_down):
    x = jnp.concatenate([x_prompt.reshape(N_CTX_TOK, D_MODEL), x_sample.reshape(N_LAT_TOK, D_MODEL)], axis=0)
    cond = jnp.concatenate([c_ctx[None, :], c, jnp.zeros((N_COND - 1 - DEC_BATCH, D_MODEL), F32)], axis=0)
    mod = _modulation_call(cond, w_ada, b_ada).reshape(DEPTH, N_COND, 1, 6 * D_MODEL)

    w_in_b = w_in.astype(BF16)
    w_out_b = w_out.astype(BF16)
    seg = jnp.arange(ATT_Q) // HEAD_DIM
    bd = jnp.where(seg[:, None] == seg[None, :], 1.0 / HEAD_DIM, 0.0).astype(BF16)
    qg = jnp.tile(q_norm, (1, N_HEADS)).reshape(DEPTH, 1, ATT_Q)
    kg = jnp.tile(k_norm, (1, N_KV)).reshape(DEPTH, 1, ATT_KV)
    cos, sin = _rope_tables()
    gain1 = norm_attn.reshape(DEPTH, 1, D_MODEL)
    gain2 = norm_ffn.reshape(DEPTH, 1, D_MODEL)
    pad8 = lambda a: jnp.pad(a.reshape(DEPTH, 1, N_DIR * DN_HEADS), ((0, 0), (0, 0), (0, N_GATE_COLS - N_DIR * DN_HEADS)))
    alog16, dtb16 = pad8(a_log), pad8(dt_bias)
    o_gain = o_norm.reshape(DEPTH, 1, DV)
    ck = cache_k.reshape(DEC_BATCH, DEPTH, PAST_LEN, ATT_KV)
    cv = cache_v.reshape(DEC_BATCH, DEPTH, PAST_LEN, ATT_KV)
    router_wt = router_w.T
    rb = router_bias.reshape(N_EXPERTS, 1)
    tri = jnp.arange(TOK_TILE)
    ut = (tri[:, None] < tri[None, :]).astype(BF16)
    sorted_buf = jnp.zeros((N_SORTED, D_MODEL), F32)

    k_list, v_list, s_list = [], [], []
    for l in range(DEPTH):
        q, kv, qkvd, z, ab = _inproj_call(l, x, mod, gain1, w_in_b, bd, qg, kg, cos, sin)
        k_list.append(kv[:N_CTX_TOK, 0:ATT_KV].reshape(BATCH, SEQ, N_KV, HEAD_DIM))
        v_list.append(kv[:N_CTX_TOK, ATT_KV:].reshape(BATCH, SEQ, N_KV, HEAD_DIM))
        mix = _attn_ctx_call(sink[l], q, kv)
        mix = _attn_lat_call(l, sink[l], q, kv, ck, cv, mix)
        mix, s_c = _gdn_call(l, qkvd, z, ab, conv_w, alog16, dtb16, o_gain, state_dn, mix, latent=False)
        (mix,) = _gdn_call(l, qkvd, z, ab, conv_w, alog16, dtb16, o_gain, state_dn, mix, latent=True)
        s_list.append(s_c)
        x1, h2, ri, rg, counts = _outproj_call(l, x, mix, mod, gain2, w_out_b, router_wt, rb, ut)
        pos_tiles, tile_expert, n_used = _routing_layout(ri, counts)
        xs_sorted = _dispatch_call(pos_tiles, h2, sorted_buf)
        sorted_buf = xs_sorted
        y_sorted = _experts_call(l, tile_expert, n_used, xs_sorted, w_gate, w_up, w_down)
        x = _combine_call(l, pos_tiles, x1, rg.T, mod, y_sorted)

    y_prompt = x[:N_CTX_TOK].reshape(BATCH, SEQ, D_MODEL)
    y_sample = x[N_CTX_TOK:].reshape(DEC_BATCH, DEC_SEQ, D_MODEL)
    return (y_prompt, y_sample, jnp.stack(k_list, axis=1), jnp.stack(v_list, axis=1), jnp.stack(s_list, axis=1))
```

```python
import functools

import jax
import jax.numpy as jnp
from jax import lax
from jax.experimental import pallas as pl
from jax.experimental.pallas import tpu as pltpu

D_MODEL = 1024
BATCH = 32
SEQ = 256
DEPTH = 4
DEC_BATCH = 4
DEC_SEQ = 1024
PAST_LEN = 256
GRID_W = 64
N_HEADS = 8
N_KV = 2
HEAD_DIM = 64
WINDOW = 128
BLOCK = 128
ATT_SCALE = HEAD_DIM ** -0.5
ROPE_BASE = 10000.0
ROPE_PAIRS = HEAD_DIM // 4
DN_HEADS = 4
DK = 128
DV = 128
DN_CONV = 5
CHUNK = 64
N_DIR = 2
ATT_Q = N_HEADS * HEAD_DIM
ATT_KV = N_KV * HEAD_DIM
DN_QK = DN_HEADS * DK
DN_V = DN_HEADS * DV
DN_CONV_CH = 2 * DN_QK + DN_V
N_GATE_COLS = 2 * N_DIR * DN_HEADS
IN_DIM = ATT_Q + 2 * ATT_KV + DN_CONV_CH + DN_V + N_GATE_COLS
N_EXPERTS = 16
N_GROUPS = 4
EXPERTS_PER_GROUP = 4
TOP_K = 2
D_FF = 512
EPS = 1e-6

N_CTX_TOK = BATCH * SEQ
N_LAT_TOK = DEC_BATCH * DEC_SEQ
N_TOK = N_CTX_TOK + N_LAT_TOK
N_COND = 8

LANES = 128
TOK_TILE = 256
EXP_TILE = 256
N_EXP_TILES = (N_TOK * TOP_K) // EXP_TILE + N_EXPERTS
N_SORTED = N_EXP_TILES * EXP_TILE
VMEM_LIMIT = 56 * 1024 * 1024
LAT_Q_ROWS = 256
ROW_DMA_UNROLL = 8

F32 = jnp.float32
BF16 = jnp.bfloat16


def _cparams(*sem):
    return pltpu.CompilerParams(dimension_semantics=sem, vmem_limit_bytes=VMEM_LIMIT)


def _dot(a, b):
    return jnp.dot(a.astype(BF16), b.astype(BF16), preferred_element_type=F32)


def _dot_nt(a, b):
    return lax.dot_general(a.astype(BF16), b.astype(BF16), (((1,), (1,)), ((), ())), preferred_element_type=F32)


def _dot_tn(a, b):
    return lax.dot_general(a.astype(BF16), b.astype(BF16), (((0,), (0,)), ((), ())), preferred_element_type=F32)


def _dot_f32(a, b):
    return jnp.dot(a, b, preferred_element_type=F32, precision=lax.Precision.HIGHEST)


def _dot_split(mat01, x):
    hi = x.astype(BF16)
    lo = (x - hi.astype(F32)).astype(BF16)
    return (jnp.dot(mat01, hi, preferred_element_type=F32) + jnp.dot(mat01, lo, preferred_element_type=F32))


def _sigmoid(x):
    return 1.0 / (1.0 + jnp.exp(-x))


def _silu(x):
    return x * _sigmoid(x)


def _softplus(x):
    return jnp.maximum(x, 0.0) + jnp.log(1.0 + jnp.exp(-jnp.abs(x)))


MOD_TN = 1536


def _mod_kernel(cond_ref, w_ref, b_ref, o_ref):
    c = _silu(cond_ref[...])
    o_ref[...] = _dot(c, w_ref[...]) + b_ref[...]


def _modulation_call(cond, w_ada, b_ada):
    n_col = 6 * D_MODEL
    return pl.pallas_call(
        _mod_kernel,
        grid=(DEPTH, n_col // MOD_TN),
        in_specs=[
            pl.BlockSpec((N_COND, D_MODEL), lambda l, j: (0, 0)),
            pl.BlockSpec((None, D_MODEL, MOD_TN), lambda l, j: (l, 0, j)),
            pl.BlockSpec((None, 1, MOD_TN), lambda l, j: (l, 0, j)),
        ],
        out_specs=pl.BlockSpec((None, N_COND, MOD_TN), lambda l, j: (l, 0, j)),
        out_shape=jax.ShapeDtypeStruct((DEPTH, N_COND, n_col), F32),
        compiler_params=_cparams("arbitrary", "arbitrary"),
        name="modulation",
    )(cond, w_ada, b_ada.reshape(DEPTH, 1, n_col))


def _mod_row(i):
    n_ctx = N_CTX_TOK // TOK_TILE
    per_seq = DEC_SEQ // TOK_TILE
    return jnp.where(i < n_ctx, 0, 1 + (jnp.maximum(i - n_ctx, 0)) // per_seq)


def _adaln(x, gain, shift, scale):
    ms = jnp.mean(x * x, axis=-1, keepdims=True)
    return (x * lax.rsqrt(ms + EPS) * gain) * (1.0 + scale) + shift


def _seg_rms(x, bd, gain):
    x2 = x * x
    hi = x2.astype(BF16)
    lo = (x2 - hi.astype(F32)).astype(BF16)
    ms = jnp.dot(hi, bd, preferred_element_type=F32) + jnp.dot(lo, bd, preferred_element_type=F32)
    return x * lax.rsqrt(ms + EPS) * gain


def _rope(x, cos, sin_signed):
    w = x.shape[-1]
    lane = lax.broadcasted_iota(jnp.int32, x.shape, 1)
    first = (lane % (2 * ROPE_PAIRS)) < ROPE_PAIRS
    partner = jnp.where(first, pltpu.roll(x, w - ROPE_PAIRS, axis=1), pltpu.roll(x, ROPE_PAIRS, axis=1))
    return x * cos + partner * sin_signed


def _inproj_kernel(x_ref, mod_ref, gain_ref, w_ref, bd_ref, qg_ref, kg_ref, cos_ref, sin_ref,
                   q_ref, kv_ref, qkvd_ref, z_ref, ab_ref):
    i = pl.program_id(0)
    is_lat = i >= N_CTX_TOK // TOK_TILE
    shift = mod_ref[:, 0:D_MODEL]
    scale = mod_ref[:, D_MODEL:2 * D_MODEL]
    h = _adaln(x_ref[...], gain_ref[...], shift, scale).astype(BF16)

    qa = jnp.dot(h, w_ref[:, 0:ATT_Q], preferred_element_type=F32)
    qn = _seg_rms(qa, bd_ref[...], qg_ref[...])
    ka = jnp.dot(h, w_ref[:, ATT_Q:ATT_Q + ATT_KV], preferred_element_type=F32)
    kn = _seg_rms(ka, bd_ref[0:ATT_KV, 0:ATT_KV], kg_ref[...])

    @pl.when(is_lat)
    def _():
        q_ref[...] = (_rope(qn, cos_ref[...], sin_ref[...]) * ATT_SCALE).astype(BF16)
        kv_ref[:, 0:ATT_KV] = _rope(kn, cos_ref[:, 0:ATT_KV], sin_ref[:, 0:ATT_KV])

    @pl.when(jnp.logical_not(is_lat))
    def _():
        q_ref[...] = (qn * ATT_SCALE).astype(BF16)
        kv_ref[:, 0:ATT_KV] = kn

    c0 = ATT_Q + ATT_KV
    kv_ref[:, ATT_KV:2 * ATT_KV] = jnp.dot(h, w_ref[:, c0:c0 + ATT_KV], preferred_element_type=F32)
    c0 += ATT_KV
    for j in range(DN_CONV_CH // 512):
        qkvd_ref[:, j * 512:(j + 1) * 512] = jnp.dot(
            h, w_ref[:, c0 + j * 512:c0 + (j + 1) * 512], preferred_element_type=F32).astype(BF16)
    c0 += DN_CONV_CH
    z_ref[...] = jnp.dot(h, w_ref[:, c0:c0 + DN_V], preferred_element_type=F32).astype(BF16)
    c0 += DN_V
    ab_ref[...] = jnp.dot(h, w_ref[:, c0:c0 + N_GATE_COLS], preferred_element_type=F32)


def _inproj_call(l, x, mod, gain, w_in, bd, qg, kg, cos, sin):
    n_ctx = N_CTX_TOK // TOK_TILE
    per_seq = DEC_SEQ // TOK_TILE

    def pos_map(i):
        return (jnp.maximum(i - n_ctx, 0) % per_seq, 0)

    tok = lambda w: pl.BlockSpec((TOK_TILE, w), lambda i: (i, 0))
    full = lambda a: pl.BlockSpec(a.shape, lambda i: (0,) * a.ndim)
    return pl.pallas_call(
        _inproj_kernel,
        grid=(N_TOK // TOK_TILE,),
        in_specs=[
            tok(D_MODEL),
            pl.BlockSpec((None, None, 1, 6 * D_MODEL), lambda i: (l, _mod_row(i), 0, 0)),
            pl.BlockSpec((None, 1, D_MODEL), lambda i: (l, 0, 0)),
            pl.BlockSpec((None, D_MODEL, IN_DIM), lambda i: (l, 0, 0)),
            full(bd),
            pl.BlockSpec((None, 1, ATT_Q), lambda i: (l, 0, 0)),
            pl.BlockSpec((None, 1, ATT_KV), lambda i: (l, 0, 0)),
            pl.BlockSpec((TOK_TILE, ATT_Q), pos_map),
            pl.BlockSpec((TOK_TILE, ATT_Q), pos_map),
        ],
        out_specs=[tok(ATT_Q), tok(2 * ATT_KV), tok(DN_CONV_CH), tok(DN_V), tok(N_GATE_COLS)],
        out_shape=[
            jax.ShapeDtypeStruct((N_TOK, ATT_Q), BF16),
            jax.ShapeDtypeStruct((N_TOK, 2 * ATT_KV), F32),
            jax.ShapeDtypeStruct((N_TOK, DN_CONV_CH), BF16),
            jax.ShapeDtypeStruct((N_TOK, DN_V), BF16),
            jax.ShapeDtypeStruct((N_TOK, N_GATE_COLS), F32),
        ],
        compiler_params=_cparams("arbitrary"),
        name="inproj",
    )(x, mod, gain, w_in, bd, qg, kg, cos, sin)


def _dup_half(x, g):
    lane = lax.broadcasted_iota(jnp.int32, x.shape, 1)
    lo = lane < HEAD_DIM
    xr = pltpu.roll(x, HEAD_DIM, axis=1)
    return jnp.where(lo, x, xr) if g == 0 else jnp.where(lo, xr, x)


def _attend(sink_ref, q_ref, o_ref, key_sets):
    rows = q_ref.shape[0]
    lane = lax.broadcasted_iota(jnp.int32, (rows, LANES), 1)
    lo = lane < HEAD_DIM
    for g in range(N_KV):
        ks = [(_dup_half(k, g).astype(BF16), _dup_half(v, g).astype(BF16), valid) for k, v, valid in key_sets]
        for jj in range(2):
            j = g * 2 + jj
            qp = q_ref[:, j * LANES:(j + 1) * LANES]
            outs = []
            for e in range(2):
                qm = jnp.where(lo if e == 0 else jnp.logical_not(lo), qp, jnp.zeros_like(qp))
                sk = sink_ref[2 * j + e]
                scores = []
                m = jnp.full((rows, 1), sk, F32)
                for k, _, valid in ks:
                    s = _dot_nt(qm, k)
                    if valid is not None:
                        s = jnp.where(valid, s, -jnp.inf)
                    scores.append(s)
                    m = jnp.maximum(m, jnp.max(s, axis=-1, keepdims=True))
                den = jnp.exp(sk - m)
                acc = jnp.zeros((rows, LANES), F32)
                for s, (_, v, _) in zip(scores, ks):
                    p = jnp.exp(s - m)
                    den = den + jnp.sum(p, axis=-1, keepdims=True)
                    acc = acc + _dot(p, v)
                outs.append(acc / den)
            o_ref[:, j * LANES:(j + 1) * LANES] = jnp.where(lo, outs[0], outs[1]).astype(o_ref.dtype)


def _attn_ctx_kernel(sink_ref, q_ref, kv_ref, o_ref):
    _attend(sink_ref, q_ref, o_ref, [(kv_ref[:, 0:ATT_KV], kv_ref[:, ATT_KV:2 * ATT_KV], None)])


def _attn_lat_kernel(sink_ref, q_ref, kv_ref, ck_ref, cv_ref, mix_in_ref, o_ref):
    del mix_in_ref
    n = pl.program_id(1)
    span = LAT_Q_ROWS + 2 * WINDOW
    start = pl.multiple_of(jnp.clip(n * LAT_Q_ROWS - WINDOW, 0, DEC_SEQ - span), BLOCK)
    kw = kv_ref[pl.ds(start, span), 0:ATT_KV]
    vw = kv_ref[pl.ds(start, span), ATT_KV:2 * ATT_KV]
    qpos = n * LAT_Q_ROWS + lax.broadcasted_iota(jnp.int32, (LAT_Q_ROWS, span), 0)
    kpos = start + lax.broadcasted_iota(jnp.int32, (LAT_Q_ROWS, span), 1)
    valid = jnp.abs(kpos - qpos) <= WINDOW
    _attend(sink_ref, q_ref, o_ref, [(kw, vw, valid), (ck_ref[...], cv_ref[...], None)])


def _attn_ctx_call(sink_l, q, kv):
    return pl.pallas_call(
        _attn_ctx_kernel,
        grid=(BATCH,),
        in_specs=[
            pl.BlockSpec(memory_space=pltpu.SMEM),
            pl.BlockSpec((SEQ, ATT_Q), lambda b: (b, 0)),
            pl.BlockSpec((SEQ, 2 * ATT_KV), lambda b: (b, 0)),
        ],
        out_specs=pl.BlockSpec((SEQ, ATT_Q), lambda b: (b, 0)),
        out_shape=jax.ShapeDtypeStruct((N_TOK, ATT_Q + DN_V), BF16),
        compiler_params=_cparams("arbitrary"),
        name="attn_ctx",
    )(sink_l, q, kv)


def _attn_lat_call(l, sink_l, q, kv, cache_k, cache_v, mix):
    nb = DEC_SEQ // LAT_Q_ROWS
    q0 = N_CTX_TOK // LAT_Q_ROWS
    s0 = N_CTX_TOK // DEC_SEQ
    return pl.pallas_call(
        _attn_lat_kernel,
        grid=(DEC_BATCH, nb),
        in_specs=[
            pl.BlockSpec(memory_space=pltpu.SMEM),
            pl.BlockSpec((LAT_Q_ROWS, ATT_Q), lambda b, n: (q0 + b * nb + n, 0)),
            pl.BlockSpec((DEC_SEQ, 2 * ATT_KV), lambda b, n: (s0 + b, 0)),
            pl.BlockSpec((None, None, PAST_LEN, ATT_KV), lambda b, n: (b, l, 0, 0)),
            pl.BlockSpec((None, None, PAST_LEN, ATT_KV), lambda b, n: (b, l, 0, 0)),
            pl.BlockSpec(memory_space=pl.ANY),
        ],
        out_specs=pl.BlockSpec((LAT_Q_ROWS, ATT_Q), lambda b, n: (q0 + b * nb + n, 0)),
        out_shape=jax.ShapeDtypeStruct((N_TOK, ATT_Q + DN_V), BF16),
        input_output_aliases={5: 0},
        compiler_params=_cparams("arbitrary", "arbitrary"),
        name="attn_lat",
    )(sink_l, q, kv, cache_k, cache_v, mix)


INV_PASSES = 1
GDN_UNROLL = 4
GDN_CTX_SEQS = 2


def _split_bf16(x):
    hi = x.astype(BF16)
    return hi, (x - hi.astype(F32)).astype(BF16)


def _block_diag(x, width):
    n = x.shape[1] // width
    blk = lax.broadcasted_iota(jnp.int32, x.shape, 1) // width
    zero = jnp.zeros_like(x)
    return jnp.concatenate([jnp.where(blk == h, x, zero) for h in range(n)], axis=0)


def _mm_heads(ts, xs, width):
    if INV_PASSES == 1:
        bds = [_block_diag(x.astype(BF16), width) for x in xs]
        return [jnp.dot(t.astype(BF16), bd, preferred_element_type=F32) for t, bd in zip(ts, bds)]
    tsp = [_split_bf16(t) for t in ts]
    xsp = [_split_bf16(x) for x in xs]
    bd_hi = [_block_diag(x_hi, width) for x_hi, _ in xsp]
    bd_lo = [_block_diag(x_lo, width) for _, x_lo in xsp]
    return [jnp.dot(t_hi, bh, preferred_element_type=F32) + jnp.dot(t_lo, bh, preferred_element_type=F32)
            + jnp.dot(t_hi, bl, preferred_element_type=F32) for (t_hi, t_lo), bh, bl in zip(tsp, bd_hi, bd_lo)]


def _unit_tri_inverse(mats, row, col):
    eye = (row == col).astype(F32)
    a8 = [jnp.where((row // 8) == (col // 8), a, 0.0) for a in mats]
    a8_2 = _mm_heads(a8, a8, CHUNK)
    a8_4 = _mm_heads(a8_2, a8_2, CHUNK)
    ts = [eye - a for a in a8]
    ts = [t + p for t, p in zip(ts, _mm_heads(ts, a8_2, CHUNK))]
    ts = [t + p for t, p in zip(ts, _mm_heads(ts, a8_4, CHUNK))]
    b = 8
    while b < CHUNK:
        level = ((row // (2 * b)) == (col // (2 * b))) & ((row // b) != (col // b))
        off = [jnp.where(level, a, 0.0) for a in mats]
        ts = [t - p for t, p in zip(ts, _mm_heads(_mm_heads(ts, off, CHUNK), ts, CHUNK))]
        b *= 2
    return ts


def _bcast_cols(x, first, width):
    rows = x.shape[0]
    if width == LANES:
        return jnp.concatenate([jnp.broadcast_to(x[:, first + h:first + h + 1], (rows, width)) for h in range(DN_HEADS)], axis=1)
    blk = lax.broadcasted_iota(jnp.int32, (rows, DN_HEADS * width), 1) // width
    out = jnp.broadcast_to(x[:, first:first + 1], (rows, DN_HEADS * width))
    for h in range(1, DN_HEADS):
        out = jnp.where(blk == h, jnp.broadcast_to(x[:, first + h:first + h + 1], (rows, DN_HEADS * width)), out)
    return out


def _gdn_kernel(*refs, seq_len, n_seq, has_s0, write_state, n_alias):
    qkvd_ref, z_ref, ab_ref, convw_ref, alog_ref, dtb_ref, ogain_ref = refs[:7]
    pos = 7
    s0_ref = None
    if has_s0:
        s0_ref = refs[pos]
        pos += 1
    pos += n_alias
    dn_ref = refs[pos]
    pos += 1
    sfin_ref = None
    if write_state:
        sfin_ref = refs[pos]
        pos += 1
    qn_s, kn_s, vn_s, gb_s, u0_s, w_s, qd_s, kd_s, qk_s, gt_s, o_s, st_s, xpad_s = refs[pos:]
    L = seq_len
    n_chunks = L // CHUNK
    n_pairs = DN_HEADS // 2
    pair_w = 2 * DK
    pad = (DN_CONV - 1) // 2

    halo = 8

    def finish(y, kind):
        y = _silu(y)
        if kind == 0:
            return y * lax.rsqrt(jnp.sum(y * y, axis=-1, keepdims=True) + EPS) * (DK ** -0.5)
        if kind == 1:
            return y * lax.rsqrt(jnp.sum(y * y, axis=-1, keepdims=True) + EPS)
        return y

    xpad_s[0:halo, :] = jnp.zeros((halo, LANES), F32)
    xpad_s[halo + L:2 * halo + L, :] = jnp.zeros((halo, LANES), F32)
    for kind, dst in enumerate((qn_s, kn_s, vn_s)):
        def conv_head(head, carry, kind=kind, dst=dst):
            cs = pl.ds(pl.multiple_of((kind * DN_HEADS + head) * LANES, LANES), LANES)
            hs = pl.ds(pl.multiple_of(head * DK, DK), DK)
            for s in range(n_seq):
                r0 = s * L
                for rb in range(0, L, SEQ):
                    xpad_s[halo + rb:halo + rb + SEQ, :] = qkvd_ref[r0 + rb:r0 + rb + SEQ, cs].astype(F32)
                for rb in range(0, L, SEQ):
                    acc = None
                    for i in range(DN_CONV):
                        lo = halo + rb + i - pad
                        term = xpad_s[lo:lo + SEQ, :] * convw_ref[i:i + 1, cs]
                        acc = term if acc is None else acc + term
                    dst[r0 + rb:r0 + rb + SEQ, hs] = finish(acc, kind)
            return carry

        lax.fori_loop(0, DN_HEADS, conv_head, 0)

    ab = ab_ref[...]
    glane = lax.broadcasted_iota(jnp.int32, ab.shape, 1)
    g = -jnp.exp(alog_ref[...]) * _softplus(ab + dtb_ref[...])
    gb_s[...] = jnp.where(glane < N_DIR * DN_HEADS, g, _sigmoid(ab))

    for s in range(n_seq):
        for d in range(N_DIR):
            for p in range(n_pairs):
                idx = (s * N_DIR + d) * n_pairs + p
                st_s[idx] = jnp.zeros((pair_w, pair_w), F32)
                if has_s0:
                    st_s[idx, 0:DK, 0:DV] = s0_ref[s, d, 2 * p]
                    st_s[idx, DK:2 * DK, DV:2 * DV] = s0_ref[s, d, 2 * p + 1]

    row = lax.broadcasted_iota(jnp.int32, (CHUNK, DN_HEADS * CHUNK), 0)
    col = lax.broadcasted_iota(jnp.int32, (CHUNK, DN_HEADS * CHUNK), 1) % CHUNK
    before_incl = [col <= row, col >= row]
    before_strict = [col < row, col > row]
    after_strict01 = [m.astype(F32) for m in before_strict]
    r64 = lax.broadcasted_iota(jnp.int32, (CHUNK, CHUNK), 0)
    c64 = lax.broadcasted_iota(jnp.int32, (CHUNK, CHUNK), 1)
    tri01 = [(c64 <= r64).astype(BF16), (c64 >= r64).astype(BF16)]

    def prep_step(n, carry):
        a_mats, v_betas, kb_egcs, slots = [], [], [], []
        for j in range(GDN_UNROLL):
            c = n * GDN_UNROLL + j
            rs = pl.ds(pl.multiple_of(c * CHUNK, CHUNK), CHUNK)
            gbc = gb_s[rs, :]
            q = qn_s[rs, :]
            k = kn_s[rs, :]
            v = vn_s[rs, :]
            k_bd = _block_diag(k.astype(BF16), DK)
            betas = [_bcast_cols(gbc, N_DIR * DN_HEADS + d * DN_HEADS, DK) for d in range(N_DIR)]
            kbs = [k * beta for beta in betas]
            kk_qk = _dot_nt(jnp.concatenate(kbs + [q], axis=0), k_bd)
            qk_raw = kk_qk[N_DIR * CHUNK:(N_DIR + 1) * CHUNK]
            for d in range(N_DIR):
                first = d * DN_HEADS
                gc_all = _dot_split(tri01[d], gbc)
                last = CHUNK - 1 if d == 0 else 0
                gc_last = gc_all[last:last + 1, :]
                gt_s[d, pl.ds(c, 1), :] = jnp.exp(gc_last)
                e_gc = _bcast_cols(jnp.exp(gc_all), first, DK)
                e_kd = _bcast_cols(jnp.exp(gc_last - gc_all), first, DK)
                gdiff = _dot_split(tri01[d], _bcast_cols(gbc, first, CHUNK) * after_strict01[d])
                decay = jnp.where(before_incl[d], jnp.exp(gdiff), 0.0)
                a_mats.append(jnp.where(before_strict[d], kk_qk[d * CHUNK:(d + 1) * CHUNK] * decay, 0.0))
                qk_s[d, rs, :] = jnp.where(before_incl[d], qk_raw * decay, 0.0).astype(BF16)
                qd_s[d, rs, :] = (q * e_gc).astype(BF16)
                kd_s[d, rs, :] = (k * e_kd).astype(BF16)
                v_betas.append(v * betas[d])
                kb_egcs.append(kbs[d] * e_gc)
                slots.append((d, rs))
        t_invs = _unit_tri_inverse(a_mats, row, col)
        for (d, rs), u0, w in zip(slots, _mm_heads(t_invs, v_betas, DV), _mm_heads(t_invs, kb_egcs, DK)):
            u0_s[d, rs, :] = u0
            w_s[d, rs, :] = w.astype(BF16)
        return carry

    lax.fori_loop(0, n_seq * n_chunks // GDN_UNROLL, prep_step, 0)

    plane = lax.broadcasted_iota(jnp.int32, (1, pair_w), 1)
    srow = lax.broadcasted_iota(jnp.int32, (pair_w, pair_w), 0) // DK
    scol = lax.broadcasted_iota(jnp.int32, (pair_w, pair_w), 1) // DV
    same_head = srow == scol

    def scan_step(n, carry):
        probs = []
        for s in range(n_seq):
            for d in range(N_DIR):
                c = s * n_chunks + (n if d == 0 else n_chunks - 1 - n)
                rs = pl.ds(pl.multiple_of(c * CHUNK, CHUNK), CHUNK)
                gt = gt_s[d, pl.ds(c, 1), :]
                for p in range(n_pairs):
                    c0 = d * DN_HEADS + 2 * p
                    g_tot = jnp.where(plane < DV, gt[:, c0:c0 + 1], gt[:, c0 + 1:c0 + 2])
                    probs.append((d, rs, p, (s * N_DIR + d) * n_pairs + p, slice(p * pair_w, (p + 1) * pair_w), g_tot))
        s_prev = [st_s[idx] for _, _, _, idx, _, _ in probs]
        s_b = [s.astype(BF16) for s in s_prev]
        ws = [jnp.dot(w_s[d, rs, ps], sb, preferred_element_type=F32) for (d, rs, _, _, ps, _), sb in zip(probs, s_b)]
        u_b = [(u0_s[d, rs, ps] - w).astype(BF16) for (d, rs, _, _, ps, _), w in zip(probs, ws)]
        upd = [lax.dot_general(kd_s[d, rs, ps], u, (((0,), (0,)), ((), ())), preferred_element_type=F32)
               for (d, rs, _, _, ps, _), u in zip(probs, u_b)]
        for (_, _, _, idx, _, g_tot), s, up in zip(probs, s_prev, upd):
            st_s[idx] = s * g_tot + jnp.where(same_head, up, 0.0)
        for (d, rs, p, _, ps, _), sb, u in zip(probs, s_b, u_b):
            o_s[d, rs, ps] = (jnp.dot(qd_s[d, rs, ps], sb, preferred_element_type=F32)
                              + jnp.dot(qk_s[d, rs, p * 2 * CHUNK:(p + 1) * 2 * CHUNK], _block_diag(u, DV),
                                        preferred_element_type=F32))
        return carry

    lax.fori_loop(0, n_chunks, scan_step, 0)

    for h in range(DN_HEADS):
        hs = slice(h * DV, (h + 1) * DV)
        o = o_s[0, :, hs] + o_s[1, :, hs]
        y = o * lax.rsqrt(jnp.mean(o * o, axis=-1, keepdims=True) + EPS) * ogain_ref[...]
        dn_ref[:, hs] = (y * _silu(z_ref[:, hs].astype(F32))).astype(dn_ref.dtype)
    if write_state:
        for s in range(n_seq):
            for d in range(N_DIR):
                for p in range(n_pairs):
                    idx = (s * N_DIR + d) * n_pairs + p
                    sfin_ref[s, d, 2 * p] = st_s[idx, 0:DK, 0:DV]
                    sfin_ref[s, d, 2 * p + 1] = st_s[idx, DK:2 * DK, DV:2 * DV]


def _gdn_call(l, qkvd, z, ab, conv_w, alog16, dtb16, o_gain, state_dn, mix, new_state=None, *, latent):
    L = DEC_SEQ if latent else SEQ
    n_seq = 1 if latent else GDN_CTX_SEQS
    n_steps = (DEC_BATCH if latent else BATCH) // n_seq
    rows = n_seq * L
    base = N_CTX_TOK // rows if latent else 0
    seq = lambda w: pl.BlockSpec((rows, w), lambda b: (base + b, 0))
    in_specs = [
        seq(DN_CONV_CH), seq(DN_V), seq(N_GATE_COLS),
        pl.BlockSpec((None, DN_CONV, DN_CONV_CH), lambda b: (l, 0, 0)),
        pl.BlockSpec((None, 1, N_GATE_COLS), lambda b: (l, 0, 0)),
        pl.BlockSpec((None, 1, N_GATE_COLS), lambda b: (l, 0, 0)),
        pl.BlockSpec((None, 1, DV), lambda b: (l, 0, 0)),
    ]
    args = [qkvd, z, ab, conv_w, alog16, dtb16, o_gain]
    out_specs = [pl.BlockSpec((rows, DN_V), lambda b: (base + b, 1))]
    out_shape = [jax.ShapeDtypeStruct((N_TOK, ATT_Q + DN_V), BF16)]
    if latent:
        in_specs.append(pl.BlockSpec((n_seq, None, N_DIR, DN_HEADS, DK, DV), lambda b: (b, l, 0, 0, 0, 0)))
        args.append(state_dn)
    in_specs.append(pl.BlockSpec(memory_space=pl.ANY))
    args.append(mix)
    aliases = {len(args) - 1: 0}
    if not latent:
        out_specs.append(pl.BlockSpec((n_seq, None, N_DIR, DN_HEADS, DK, DV), lambda b: (b, l, 0, 0, 0, 0)))
        out_shape.append(jax.ShapeDtypeStruct((BATCH, DEPTH, N_DIR, DN_HEADS, DK, DV), F32))
        if new_state is not None:
            in_specs.append(pl.BlockSpec(memory_space=pl.ANY))
            args.append(new_state)
            aliases[len(args) - 1] = 1
    n_chunks = rows // CHUNK
    return pl.pallas_call(
        functools.partial(_gdn_kernel, seq_len=L, n_seq=n_seq, has_s0=latent, write_state=not latent,
                          n_alias=len(aliases)),
        grid=(n_steps,),
        in_specs=in_specs,
        out_specs=out_specs,
        out_shape=out_shape,
        scratch_shapes=[
            pltpu.VMEM((rows, DN_QK), F32), pltpu.VMEM((rows, DN_QK), F32), pltpu.VMEM((rows, DN_V), F32),
            pltpu.VMEM((rows, N_GATE_COLS), F32),
            pltpu.VMEM((N_DIR, rows, DN_V), F32),
            pltpu.VMEM((N_DIR, rows, DN_QK), BF16),
            pltpu.VMEM((N_DIR, rows, DN_QK), BF16),
            pltpu.VMEM((N_DIR, rows, DN_QK), BF16),
            pltpu.VMEM((N_DIR, rows, DN_HEADS * CHUNK), BF16),
            pltpu.VMEM((N_DIR, n_chunks, N_GATE_COLS), F32),
            pltpu.VMEM((N_DIR, rows, DN_V), F32),
            pltpu.VMEM((n_seq * N_DIR * (DN_HEADS // 2), 2 * DK, 2 * DV), F32),
            pltpu.VMEM((L + 16, LANES), F32),
        ],
        input_output_aliases=aliases,
        compiler_params=_cparams("arbitrary"),
        name="gdn_lat" if latent else "gdn_ctx",
    )(*args)


def _top2_of4(vals):
    m1 = jnp.maximum(jnp.maximum(vals[0], vals[1]), jnp.maximum(vals[2], vals[3]))
    i1 = jnp.where(vals[0] == m1, 0, jnp.where(vals[1] == m1, 1, jnp.where(vals[2] == m1, 2, 3)))
    rest = [jnp.where(i1 == j, -jnp.inf, vals[j]) for j in range(4)]
    m2 = jnp.maximum(jnp.maximum(rest[0], rest[1]), jnp.maximum(rest[2], rest[3]))
    i2 = jnp.where(rest[0] == m2, 0, jnp.where(rest[1] == m2, 1, jnp.where(rest[2] == m2, 2, 3)))
    return m1, i1, m2, i2


def _outproj_kernel(x_ref, mix_ref, mod_ref, gain_ref, wo_ref, rwt_ref, rb_ref, ut_ref,
                    x1_ref, h2_ref, ri_ref, rg_ref, cnt_ref, carry_s):
    i = pl.program_id(0)

    @pl.when(i == 0)
    def _():
        carry_s[...] = jnp.zeros_like(carry_s)

    gate1 = mod_ref[:, 2 * D_MODEL:3 * D_MODEL]
    shift2 = mod_ref[:, 3 * D_MODEL:4 * D_MODEL]
    scale2 = mod_ref[:, 4 * D_MODEL:5 * D_MODEL]
    x1 = x_ref[...] + gate1 * jnp.dot(mix_ref[...], wo_ref[...], preferred_element_type=F32)
    x1_ref[...] = x1
    h2 = _adaln(x1, gain_ref[...], shift2, scale2)
    h2_ref[...] = h2

    h_hi, h_lo = _split_bf16(h2)
    r_hi, r_lo = _split_bf16(rwt_ref[...])
    nt = (((1,), (1,)), ((), ()))
    logits = (lax.dot_general(r_hi, h_hi, nt, preferred_element_type=F32)
              + lax.dot_general(r_hi, h_lo, nt, preferred_element_type=F32)
              + lax.dot_general(r_lo, h_hi, nt, preferred_element_type=F32))
    scores = _sigmoid(logits)
    biased = scores + rb_ref[...]
    rows = [biased[e:e + 1, :] for e in range(N_EXPERTS)]
    per_group = [_top2_of4(rows[g * 4:(g + 1) * 4]) for g in range(N_GROUPS)]
    gs = [pg[0] + pg[2] for pg in per_group]
    _, gsel, _, _ = _top2_of4(gs)
    pick = lambda k: jnp.where(gsel == 0, per_group[0][k], jnp.where(gsel == 1, per_group[1][k],
                               jnp.where(gsel == 2, per_group[2][k], per_group[3][k])))
    e1 = gsel * EXPERTS_PER_GROUP + pick(1)
    e2 = gsel * EXPERTS_PER_GROUP + pick(3)
    erow = lax.broadcasted_iota(jnp.int32, scores.shape, 0)
    oh1 = erow == e1
    oh2 = erow == e2
    s1 = jnp.sum(jnp.where(oh1, scores, 0.0), axis=0, keepdims=True)
    s2 = jnp.sum(jnp.where(oh2, scores, 0.0), axis=0, keepdims=True)
    tot = s1 + s2
    grow = lax.broadcasted_iota(jnp.int32, rg_ref.shape, 0)
    rg_ref[...] = jnp.where(grow == 0, s1 / tot, s2 / tot)

    sel = (oh1 | oh2).astype(BF16)
    before = jnp.dot(sel, ut_ref[...], preferred_element_type=F32) + carry_s[...]
    r1 = jnp.sum(jnp.where(oh1, before, 0.0), axis=0, keepdims=True)
    r2 = jnp.sum(jnp.where(oh2, before, 0.0), axis=0, keepdims=True)
    carry_s[...] = carry_s[...] + jnp.sum(sel.astype(F32), axis=1, keepdims=True)
    cnt_ref[...] = carry_s[...].astype(jnp.int32)
    irow = lax.broadcasted_iota(jnp.int32, ri_ref.shape, 0)
    ri_ref[...] = jnp.where(irow == 0, e1, jnp.where(irow == 1, e2, jnp.where(
        irow == 2, r1.astype(jnp.int32), r2.astype(jnp.int32))))


def _outproj_call(l, x, mix, mod, gain, w_out, router_wt, router_b, ut):
    tok = lambda w: pl.BlockSpec((TOK_TILE, w), lambda i: (i, 0))
    per_tok = lambda r: pl.BlockSpec((r, TOK_TILE), lambda i: (0, i))
    full = lambda a: pl.BlockSpec(a.shape, lambda i: (0,) * a.ndim)
    return pl.pallas_call(
        _outproj_kernel,
        grid=(N_TOK // TOK_TILE,),
        in_specs=[
            tok(D_MODEL), tok(ATT_Q + DN_V),
            pl.BlockSpec((None, None, 1, 6 * D_MODEL), lambda i: (l, _mod_row(i), 0, 0)),
            pl.BlockSpec((None, 1, D_MODEL), lambda i: (l, 0, 0)),
            pl.BlockSpec((None, ATT_Q + DN_V, D_MODEL), lambda i: (l, 0, 0)),
            full(router_wt), full(router_b), full(ut),
        ],
        out_specs=[tok(D_MODEL), tok(D_MODEL), per_tok(4), per_tok(2), pl.BlockSpec((N_EXPERTS, 1), lambda i: (0, 0))],
        out_shape=[
            jax.ShapeDtypeStruct((N_TOK, D_MODEL), F32),
            jax.ShapeDtypeStruct((N_TOK, D_MODEL), F32),
            jax.ShapeDtypeStruct((4, N_TOK), jnp.int32),
            jax.ShapeDtypeStruct((2, N_TOK), F32),
            jax.ShapeDtypeStruct((N_EXPERTS, 1), jnp.int32),
        ],
        scratch_shapes=[pltpu.VMEM((N_EXPERTS, 1), F32)],
        compiler_params=_cparams("arbitrary"),
        name="outproj_router",
    )(x, mix, mod, gain, w_out, router_wt, router_b, ut)


def _dispatch_kernel(pos_ref, h2_ref, buf_in_ref, sorted_ref, sem):
    del buf_in_ref

    def row_copy(r, k):
        return pltpu.make_async_copy(h2_ref.at[pl.ds(r, 1), :], sorted_ref.at[pl.ds(pos_ref[0, k * TOK_TILE + r], 1), :], sem)

    def start(r, c):
        row_copy(r, 0).start()
        row_copy(r, 1).start()
        return c

    def wait(r, c):
        row_copy(r, 0).wait()
        row_copy(r, 1).wait()
        return c

    lax.fori_loop(0, TOK_TILE, start, 0, unroll=ROW_DMA_UNROLL)
    lax.fori_loop(0, TOK_TILE, wait, 0, unroll=ROW_DMA_UNROLL)


def _dispatch_call(pos_tiles, h2, zeros_sorted):
    return pl.pallas_call(
        _dispatch_kernel,
        grid=(N_TOK // TOK_TILE,),
        in_specs=[
            pl.BlockSpec((None, 1, TOP_K * TOK_TILE), lambda i: (i, 0, 0), memory_space=pltpu.SMEM),
            pl.BlockSpec((TOK_TILE, D_MODEL), lambda i: (i, 0)),
            pl.BlockSpec(memory_space=pl.ANY),
        ],
        out_specs=pl.BlockSpec(memory_space=pl.ANY),
        out_shape=jax.ShapeDtypeStruct((N_SORTED, D_MODEL), F32),
        scratch_shapes=[pltpu.SemaphoreType.DMA(())],
        input_output_aliases={2: 0},
        compiler_params=_cparams("arbitrary"),
        name="moe_dispatch",
    )(pos_tiles, h2, zeros_sorted)


def _experts_kernel(te_ref, nu_ref, x_ref, wg_ref, wu_ref, wd_ref, y_ref, wg_s, wu_s, wd_s):
    j = pl.program_id(0)
    used = j < nu_ref[0]

    @pl.when((j == 0) | (te_ref[j] != te_ref[jnp.maximum(j - 1, 0)]))
    def _():
        wg_s[...] = wg_ref[...].astype(BF16)
        wu_s[...] = wu_ref[...].astype(BF16)
        wd_s[...] = wd_ref[...].astype(BF16)

    @pl.when(used)
    def _():
        x = x_ref[...].astype(BF16)
        hid = _silu(jnp.dot(x, wg_s[...], preferred_element_type=F32)) * jnp.dot(x, wu_s[...], preferred_element_type=F32)
        y_ref[...] = jnp.dot(hid.astype(BF16), wd_s[...], preferred_element_type=F32)

    @pl.when(jnp.logical_not(used))
    def _():
        y_ref[...] = jnp.zeros_like(y_ref)


def _experts_call(l, tile_expert, n_used, xs, w_gate, w_up, w_down):
    row = lambda j, te, nu: (jnp.maximum(jnp.minimum(j, nu[0] - 1), 0), 0)
    grid_spec = pltpu.PrefetchScalarGridSpec(
        num_scalar_prefetch=2,
        grid=(N_EXP_TILES,),
        in_specs=[
            pl.BlockSpec((EXP_TILE, D_MODEL), row),
            pl.BlockSpec((None, None, D_MODEL, D_FF), lambda j, te, nu: (l, te[j], 0, 0)),
            pl.BlockSpec((None, None, D_MODEL, D_FF), lambda j, te, nu: (l, te[j], 0, 0)),
            pl.BlockSpec((None, None, D_FF, D_MODEL), lambda j, te, nu: (l, te[j], 0, 0)),
        ],
        out_specs=pl.BlockSpec((EXP_TILE, D_MODEL), lambda j, te, nu: (j, 0)),
        scratch_shapes=[pltpu.VMEM((D_MODEL, D_FF), BF16), pltpu.VMEM((D_MODEL, D_FF), BF16),
                        pltpu.VMEM((D_FF, D_MODEL), BF16)],
    )
    return pl.pallas_call(
        _experts_kernel,
        grid_spec=grid_spec,
        out_shape=jax.ShapeDtypeStruct((N_SORTED, D_MODEL), F32),
        compiler_params=_cparams("arbitrary"),
        name="moe_experts",
    )(tile_expert, n_used, xs, w_gate, w_up, w_down)


def _combine_kernel(pos_ref, x1_ref, rg_ref, mod_ref, y_ref, out_ref, buf_s, sem):
    def row_copy(r, k):
        return pltpu.make_async_copy(y_ref.at[pl.ds(pos_ref[0, k * TOK_TILE + r], 1), :], buf_s.at[k, pl.ds(r, 1), :], sem)

    def start(r, c):
        row_copy(r, 0).start()
        row_copy(r, 1).start()
        return c

    def wait(r, c):
        row_copy(r, 0).wait()
        row_copy(r, 1).wait()
        return c

    lax.fori_loop(0, TOK_TILE, start, 0, unroll=ROW_DMA_UNROLL)
    lax.fori_loop(0, TOK_TILE, wait, 0, unroll=ROW_DMA_UNROLL)
    gate2 = mod_ref[:, 5 * D_MODEL:6 * D_MODEL]
    moe = buf_s[0] * rg_ref[:, 0:1] + buf_s[1] * rg_ref[:, 1:2]
    out_ref[...] = x1_ref[...] + gate2 * moe


def _combine_call(l, pos_tiles, x1, rg, mod, y_sorted):
    tok = lambda w: pl.BlockSpec((TOK_TILE, w), lambda i: (i, 0))
    return pl.pallas_call(
        _combine_kernel,
        grid=(N_TOK // TOK_TILE,),
        in_specs=[
            pl.BlockSpec((None, 1, TOP_K * TOK_TILE), lambda i: (i, 0, 0), memory_space=pltpu.SMEM),
            tok(D_MODEL), tok(2),
            pl.BlockSpec((None, None, 1, 6 * D_MODEL), lambda i: (l, _mod_row(i), 0, 0)),
            pl.BlockSpec(memory_space=pl.ANY),
        ],
        out_specs=tok(D_MODEL),
        out_shape=jax.ShapeDtypeStruct((N_TOK, D_MODEL), F32),
        scratch_shapes=[pltpu.VMEM((TOP_K, TOK_TILE, D_MODEL), F32), pltpu.SemaphoreType.DMA(())],
        compiler_params=_cparams("arbitrary"),
        name="moe_combine",
    )(pos_tiles, x1, rg, mod, y_sorted)


def _rope_tables():
    pos = jnp.arange(DEC_SEQ)
    r = (pos // GRID_W).astype(F32)
    c = (pos % GRID_W).astype(F32)
    inv = ROPE_BASE ** (-jnp.arange(ROPE_PAIRS, dtype=F32) / ROPE_PAIRS)
    ar, ac = r[:, None] * inv, c[:, None] * inv
    cos = jnp.concatenate([jnp.cos(ar), jnp.cos(ar), jnp.cos(ac), jnp.cos(ac)], axis=-1)
    sin = jnp.concatenate([-jnp.sin(ar), jnp.sin(ar), -jnp.sin(ac), jnp.sin(ac)], axis=-1)
    return jnp.tile(cos, (1, N_HEADS)), jnp.tile(sin, (1, N_HEADS))


def _routing_layout(ri, counts):
    counts = counts.reshape(N_EXPERTS)
    padded = ((counts + EXP_TILE - 1) // EXP_TILE) * EXP_TILE
    ends = jnp.cumsum(padded)
    offs = ends - padded
    eids = jnp.arange(N_EXPERTS, dtype=jnp.int32)[:, None, None]
    pos = jnp.sum(jnp.where(ri[None, 0:2] == eids, offs[:, None, None], 0), axis=0) + ri[2:4]
    n_used = (ends[-1] // EXP_TILE).astype(jnp.int32)
    tile_start = jnp.arange(N_EXP_TILES, dtype=jnp.int32) * EXP_TILE
    tile_expert = jnp.sum((tile_start[:, None] >= ends[None, :]).astype(jnp.int32), axis=1)
    last = jnp.sum((jnp.maximum(ends[-1] - EXP_TILE, 0) >= ends).astype(jnp.int32))
    tile_expert = jnp.minimum(jnp.where(tile_start < ends[-1], tile_expert, last), N_EXPERTS - 1).astype(jnp.int32)
    pos_tiles = pos.reshape(TOP_K, N_TOK // TOK_TILE, TOK_TILE).transpose(1, 0, 2).reshape(
        N_TOK // TOK_TILE, 1, TOP_K * TOK_TILE).astype(jnp.int32)
    return pos_tiles, tile_expert, n_used.reshape(1)


def kernel(x_prompt, x_sample, cache_k, cache_v, state_dn, c, c_ctx, w_ada, b_ada, norm_attn, norm_ffn,
           w_in, conv_w, a_log, dt_bias, q_norm, k_norm, sink, o_norm, w_out, router_w, router_bias,
           w_gate, w_up, w_down):
    x = jnp.concatenate([x_prompt.reshape(N_CTX_TOK, D_MODEL), x_sample.reshape(N_LAT_TOK, D_MODEL)], axis=0)
    cond = jnp.concatenate([c_ctx[None, :], c, jnp.zeros((N_COND - 1 - DEC_BATCH, D_MODEL), F32)], axis=0)
    mod = _modulation_call(cond, w_ada, b_ada).reshape(DEPTH, N_COND, 1, 6 * D_MODEL)

    w_in_b = w_in.astype(BF16)
    w_out_b = w_out.astype(BF16)
    seg = jnp.arange(ATT_Q) // HEAD_DIM
    bd = jnp.where(seg[:, None] == seg[None, :], 1.0 / HEAD_DIM, 0.0).astype(BF16)
    qg = jnp.tile(q_norm, (1, N_HEADS)).reshape(DEPTH, 1, ATT_Q)
    kg = jnp.tile(k_norm, (1, N_KV)).reshape(DEPTH, 1, ATT_KV)
    cos, sin = _rope_tables()
    gain1 = norm_attn.reshape(DEPTH, 1, D_MODEL)
    gain2 = norm_ffn.reshape(DEPTH, 1, D_MODEL)
    pad8 = lambda a: jnp.pad(a.reshape(DEPTH, 1, N_DIR * DN_HEADS), ((0, 0), (0, 0), (0, N_GATE_COLS - N_DIR * DN_HEADS)))
    alog16, dtb16 = pad8(a_log), pad8(dt_bias)
    o_gain = o_norm.reshape(DEPTH, 1, DV)
    ck = cache_k.reshape(DEC_BATCH, DEPTH, PAST_LEN, ATT_KV)
    cv = cache_v.reshape(DEC_BATCH, DEPTH, PAST_LEN, ATT_KV)
    router_wt = router_w.T
    rb = router_bias.reshape(N_EXPERTS, 1)
    tri = jnp.arange(TOK_TILE)
    ut = (tri[:, None] < tri[None, :]).astype(BF16)
    sorted_buf = jnp.zeros((N_SORTED, D_MODEL), F32)

    k_list, v_list, new_state = [], [], None
    for l in range(DEPTH):
        q, kv, qkvd, z, ab = _inproj_call(l, x, mod, gain1, w_in_b, bd, qg, kg, cos, sin)
        k_list.append(kv[:N_CTX_TOK, 0:ATT_KV].reshape(BATCH, SEQ, N_KV, HEAD_DIM))
        v_list.append(kv[:N_CTX_TOK, ATT_KV:].reshape(BATCH, SEQ, N_KV, HEAD_DIM))
        mix = _attn_ctx_call(sink[l], q, kv)
        mix = _attn_lat_call(l, sink[l], q, kv, ck, cv, mix)
        mix, new_state = _gdn_call(l, qkvd, z, ab, conv_w, alog16, dtb16, o_gain, state_dn, mix, new_state, latent=False)
        (mix,) = _gdn_call(l, qkvd, z, ab, conv_w, alog16, dtb16, o_gain, state_dn, mix, latent=True)
        x1, h2, ri, rg, counts = _outproj_call(l, x, mix, mod, gain2, w_out_b, router_wt, rb, ut)
        pos_tiles, tile_expert, n_used = _routing_layout(ri, counts)
        xs_sorted = _dispatch_call(pos_tiles, h2, sorted_buf)
        sorted_buf = xs_sorted
        y_sorted = _experts_call(l, tile_expert, n_used, xs_sorted, w_gate, w_up, w_down)
        x = _combine_call(l, pos_tiles, x1, rg.T, mod, y_sorted)

    y_prompt = x[:N_CTX_TOK].reshape(BATCH, SEQ, D_MODEL)
    y_sample = x[N_CTX_TOK:].reshape(DEC_BATCH, DEC_SEQ, D_MODEL)
    return (y_prompt, y_sample, jnp.stack(k_list, axis=1), jnp.stack(v_list, axis=1), new_state)
```

```python
import functools

import jax
import jax.numpy as jnp
from jax import lax
from jax.experimental import pallas as pl
from jax.experimental.pallas import tpu as pltpu

D_MODEL = 1024
BATCH = 32
SEQ = 256
DEPTH = 4
DEC_BATCH = 4
DEC_SEQ = 1024
PAST_LEN = 256
GRID_W = 64
N_HEADS = 8
N_KV = 2
HEAD_DIM = 64
WINDOW = 128
BLOCK = 128
ATT_SCALE = HEAD_DIM ** -0.5
ROPE_BASE = 10000.0
ROPE_PAIRS = HEAD_DIM // 4
DN_HEADS = 4
DK = 128
DV = 128
DN_CONV = 5
CHUNK = 64
N_DIR = 2
ATT_Q = N_HEADS * HEAD_DIM
ATT_KV = N_KV * HEAD_DIM
DN_QK = DN_HEADS * DK
DN_V = DN_HEADS * DV
DN_CONV_CH = 2 * DN_QK + DN_V
N_GATE_COLS = 2 * N_DIR * DN_HEADS
IN_DIM = ATT_Q + 2 * ATT_KV + DN_CONV_CH + DN_V + N_GATE_COLS
N_EXPERTS = 16
N_GROUPS = 4
EXPERTS_PER_GROUP = 4
TOP_K = 2
D_FF = 512
EPS = 1e-6

N_CTX_TOK = BATCH * SEQ
N_LAT_TOK = DEC_BATCH * DEC_SEQ
N_TOK = N_CTX_TOK + N_LAT_TOK
N_COND = 8

LANES = 128
TOK_TILE = 512
EXP_TILE = 256
N_EXP_TILES = (N_TOK * TOP_K) // EXP_TILE + N_EXPERTS
N_SORTED = N_EXP_TILES * EXP_TILE
VMEM_LIMIT = 56 * 1024 * 1024
LAT_Q_ROWS = 256
ROW_DMA_UNROLL = 8

F32 = jnp.float32
BF16 = jnp.bfloat16


def _cparams(*sem):
    return pltpu.CompilerParams(dimension_semantics=sem, vmem_limit_bytes=VMEM_LIMIT)


def _dot(a, b):
    return jnp.dot(a.astype(BF16), b.astype(BF16), preferred_element_type=F32)


def _dot_nt(a, b):
    return lax.dot_general(a.astype(BF16), b.astype(BF16), (((1,), (1,)), ((), ())), preferred_element_type=F32)


def _dot_tn(a, b):
    return lax.dot_general(a.astype(BF16), b.astype(BF16), (((0,), (0,)), ((), ())), preferred_element_type=F32)


def _dot_f32(a, b):
    return jnp.dot(a, b, preferred_element_type=F32, precision=lax.Precision.HIGHEST)


def _dot_split(mat01, x):
    hi = x.astype(BF16)
    lo = (x - hi.astype(F32)).astype(BF16)
    return (jnp.dot(mat01, hi, preferred_element_type=F32) + jnp.dot(mat01, lo, preferred_element_type=F32))


def _sigmoid(x):
    return 1.0 / (1.0 + jnp.exp(-x))


def _silu(x):
    return x * _sigmoid(x)


def _softplus(x):
    return jnp.maximum(x, 0.0) + jnp.log(1.0 + jnp.exp(-jnp.abs(x)))


MOD_TN = 1536


def _mod_kernel(cond_ref, w_ref, b_ref, o_ref):
    c = _silu(cond_ref[...])
    o_ref[...] = _dot(c, w_ref[...]) + b_ref[...]


def _modulation_call(cond, w_ada, b_ada):
    n_col = 6 * D_MODEL
    return pl.pallas_call(
        _mod_kernel,
        grid=(DEPTH, n_col // MOD_TN),
        in_specs=[
            pl.BlockSpec((N_COND, D_MODEL), lambda l, j: (0, 0)),
            pl.BlockSpec((None, D_MODEL, MOD_TN), lambda l, j: (l, 0, j)),
            pl.BlockSpec((None, 1, MOD_TN), lambda l, j: (l, 0, j)),
        ],
        out_specs=pl.BlockSpec((None, N_COND, MOD_TN), lambda l, j: (l, 0, j)),
        out_shape=jax.ShapeDtypeStruct((DEPTH, N_COND, n_col), F32),
        compiler_params=_cparams("arbitrary", "arbitrary"),
        name="modulation",
    )(cond, w_ada, b_ada.reshape(DEPTH, 1, n_col))


def _mod_row(i):
    n_ctx = N_CTX_TOK // TOK_TILE
    per_seq = DEC_SEQ // TOK_TILE
    return jnp.where(i < n_ctx, 0, 1 + (jnp.maximum(i - n_ctx, 0)) // per_seq)


def _adaln(x, gain, shift, scale):
    ms = jnp.mean(x * x, axis=-1, keepdims=True)
    return (x * lax.rsqrt(ms + EPS) * gain) * (1.0 + scale) + shift


def _seg_rms(x, bd, gain):
    x2 = x * x
    hi = x2.astype(BF16)
    lo = (x2 - hi.astype(F32)).astype(BF16)
    ms = jnp.dot(hi, bd, preferred_element_type=F32) + jnp.dot(lo, bd, preferred_element_type=F32)
    return x * lax.rsqrt(ms + EPS) * gain


def _rope(x, cos, sin_signed):
    w = x.shape[-1]
    lane = lax.broadcasted_iota(jnp.int32, x.shape, 1)
    first = (lane % (2 * ROPE_PAIRS)) < ROPE_PAIRS
    partner = jnp.where(first, pltpu.roll(x, w - ROPE_PAIRS, axis=1), pltpu.roll(x, ROPE_PAIRS, axis=1))
    return x * cos + partner * sin_signed


def _inproj_kernel(x_ref, mod_ref, gain_ref, w_ref, bd_ref, qg_ref, kg_ref, cos_ref, sin_ref,
                   q_ref, kv_ref, qkvd_ref, z_ref, ab_ref):
    i = pl.program_id(0)
    is_lat = i >= N_CTX_TOK // TOK_TILE
    shift = mod_ref[:, 0:D_MODEL]
    scale = mod_ref[:, D_MODEL:2 * D_MODEL]
    h = _adaln(x_ref[...], gain_ref[...], shift, scale).astype(BF16)

    qa = jnp.dot(h, w_ref[:, 0:ATT_Q], preferred_element_type=F32)
    qn = _seg_rms(qa, bd_ref[...], qg_ref[...])
    ka = jnp.dot(h, w_ref[:, ATT_Q:ATT_Q + ATT_KV], preferred_element_type=F32)
    kn = _seg_rms(ka, bd_ref[0:ATT_KV, 0:ATT_KV], kg_ref[...])

    @pl.when(is_lat)
    def _():
        q_ref[...] = (_rope(qn, cos_ref[...], sin_ref[...]) * ATT_SCALE).astype(BF16)
        kv_ref[:, 0:ATT_KV] = _rope(kn, cos_ref[:, 0:ATT_KV], sin_ref[:, 0:ATT_KV])

    @pl.when(jnp.logical_not(is_lat))
    def _():
        q_ref[...] = (qn * ATT_SCALE).astype(BF16)
        kv_ref[:, 0:ATT_KV] = kn

    c0 = ATT_Q + ATT_KV
    kv_ref[:, ATT_KV:2 * ATT_KV] = jnp.dot(h, w_ref[:, c0:c0 + ATT_KV], preferred_element_type=F32)
    c0 += ATT_KV
    for j in range(DN_CONV_CH // 512):
        qkvd_ref[:, j * 512:(j + 1) * 512] = jnp.dot(
            h, w_ref[:, c0 + j * 512:c0 + (j + 1) * 512], preferred_element_type=F32).astype(BF16)
    c0 += DN_CONV_CH
    z_ref[...] = jnp.dot(h, w_ref[:, c0:c0 + DN_V], preferred_element_type=F32).astype(BF16)
    c0 += DN_V
    ab_ref[...] = jnp.dot(h, w_ref[:, c0:c0 + N_GATE_COLS], preferred_element_type=F32)


def _inproj_call(l, x, mod, gain, w_in, bd, qg, kg, cos, sin):
    n_ctx = N_CTX_TOK // TOK_TILE
    per_seq = DEC_SEQ // TOK_TILE

    def pos_map(i):
        return (jnp.maximum(i - n_ctx, 0) % per_seq, 0)

    tok = lambda w: pl.BlockSpec((TOK_TILE, w), lambda i: (i, 0))
    full = lambda a: pl.BlockSpec(a.shape, lambda i: (0,) * a.ndim)
    return pl.pallas_call(
        _inproj_kernel,
        grid=(N_TOK // TOK_TILE,),
        in_specs=[
            tok(D_MODEL),
            pl.BlockSpec((None, None, 1, 6 * D_MODEL), lambda i: (l, _mod_row(i), 0, 0)),
            pl.BlockSpec((None, 1, D_MODEL), lambda i: (l, 0, 0)),
            pl.BlockSpec((None, D_MODEL, IN_DIM), lambda i: (l, 0, 0)),
            full(bd),
            pl.BlockSpec((None, 1, ATT_Q), lambda i: (l, 0, 0)),
            pl.BlockSpec((None, 1, ATT_KV), lambda i: (l, 0, 0)),
            pl.BlockSpec((TOK_TILE, ATT_Q), pos_map),
            pl.BlockSpec((TOK_TILE, ATT_Q), pos_map),
        ],
        out_specs=[tok(ATT_Q), tok(2 * ATT_KV), tok(DN_CONV_CH), tok(DN_V), tok(N_GATE_COLS)],
        out_shape=[
            jax.ShapeDtypeStruct((N_TOK, ATT_Q), BF16),
            jax.ShapeDtypeStruct((N_TOK, 2 * ATT_KV), F32),
            jax.ShapeDtypeStruct((N_TOK, DN_CONV_CH), BF16),
            jax.ShapeDtypeStruct((N_TOK, DN_V), BF16),
            jax.ShapeDtypeStruct((N_TOK, N_GATE_COLS), F32),
        ],
        compiler_params=_cparams("arbitrary"),
        name="inproj",
    )(x, mod, gain, w_in, bd, qg, kg, cos, sin)


def _dup_half(x, g):
    lane = lax.broadcasted_iota(jnp.int32, x.shape, 1)
    lo = lane < HEAD_DIM
    xr = pltpu.roll(x, HEAD_DIM, axis=1)
    return jnp.where(lo, x, xr) if g == 0 else jnp.where(lo, xr, x)


def _attend(sink_ref, q_ref, o_ref, key_sets):
    rows = q_ref.shape[0]
    lane = lax.broadcasted_iota(jnp.int32, (rows, LANES), 1)
    lo = lane < HEAD_DIM
    for g in range(N_KV):
        ks = [(_dup_half(k, g).astype(BF16), _dup_half(v, g).astype(BF16), valid) for k, v, valid in key_sets]
        for jj in range(2):
            j = g * 2 + jj
            qp = q_ref[:, j * LANES:(j + 1) * LANES]
            outs = []
            for e in range(2):
                qm = jnp.where(lo if e == 0 else jnp.logical_not(lo), qp, jnp.zeros_like(qp))
                sk = sink_ref[2 * j + e]
                scores = []
                m = jnp.full((rows, 1), sk, F32)
                for k, _, valid in ks:
                    s = _dot_nt(qm, k)
                    if valid is not None:
                        s = jnp.where(valid, s, -jnp.inf)
                    scores.append(s)
                    m = jnp.maximum(m, jnp.max(s, axis=-1, keepdims=True))
                den = jnp.exp(sk - m)
                acc = jnp.zeros((rows, LANES), F32)
                for s, (_, v, _) in zip(scores, ks):
                    p = jnp.exp(s - m)
                    den = den + jnp.sum(p, axis=-1, keepdims=True)
                    acc = acc + _dot(p, v)
                outs.append(acc / den)
            o_ref[:, j * LANES:(j + 1) * LANES] = jnp.where(lo, outs[0], outs[1]).astype(o_ref.dtype)


def _attn_ctx_kernel(sink_ref, q_ref, kv_ref, o_ref):
    _attend(sink_ref, q_ref, o_ref, [(kv_ref[:, 0:ATT_KV], kv_ref[:, ATT_KV:2 * ATT_KV], None)])


def _attn_lat_kernel(sink_ref, q_ref, kv_ref, ck_ref, cv_ref, mix_in_ref, o_ref):
    del mix_in_ref
    n = pl.program_id(1)
    span = LAT_Q_ROWS + 2 * WINDOW
    start = pl.multiple_of(jnp.clip(n * LAT_Q_ROWS - WINDOW, 0, DEC_SEQ - span), BLOCK)
    kw = kv_ref[pl.ds(start, span), 0:ATT_KV]
    vw = kv_ref[pl.ds(start, span), ATT_KV:2 * ATT_KV]
    qpos = n * LAT_Q_ROWS + lax.broadcasted_iota(jnp.int32, (LAT_Q_ROWS, span), 0)
    kpos = start + lax.broadcasted_iota(jnp.int32, (LAT_Q_ROWS, span), 1)
    valid = jnp.abs(kpos - qpos) <= WINDOW
    _attend(sink_ref, q_ref, o_ref, [(kw, vw, valid), (ck_ref[...], cv_ref[...], None)])


def _attn_ctx_call(sink_l, q, kv):
    return pl.pallas_call(
        _attn_ctx_kernel,
        grid=(BATCH,),
        in_specs=[
            pl.BlockSpec(memory_space=pltpu.SMEM),
            pl.BlockSpec((SEQ, ATT_Q), lambda b: (b, 0)),
            pl.BlockSpec((SEQ, 2 * ATT_KV), lambda b: (b, 0)),
        ],
        out_specs=pl.BlockSpec((SEQ, ATT_Q), lambda b: (b, 0)),
        out_shape=jax.ShapeDtypeStruct((N_TOK, ATT_Q + DN_V), BF16),
        compiler_params=_cparams("arbitrary"),
        name="attn_ctx",
    )(sink_l, q, kv)


def _attn_lat_call(l, sink_l, q, kv, cache_k, cache_v, mix):
    nb = DEC_SEQ // LAT_Q_ROWS
    q0 = N_CTX_TOK // LAT_Q_ROWS
    s0 = N_CTX_TOK // DEC_SEQ
    return pl.pallas_call(
        _attn_lat_kernel,
        grid=(DEC_BATCH, nb),
        in_specs=[
            pl.BlockSpec(memory_space=pltpu.SMEM),
            pl.BlockSpec((LAT_Q_ROWS, ATT_Q), lambda b, n: (q0 + b * nb + n, 0)),
            pl.BlockSpec((DEC_SEQ, 2 * ATT_KV), lambda b, n: (s0 + b, 0)),
            pl.BlockSpec((None, None, PAST_LEN, ATT_KV), lambda b, n: (b, l, 0, 0)),
            pl.BlockSpec((None, None, PAST_LEN, ATT_KV), lambda b, n: (b, l, 0, 0)),
            pl.BlockSpec(memory_space=pl.ANY),
        ],
        out_specs=pl.BlockSpec((LAT_Q_ROWS, ATT_Q), lambda b, n: (q0 + b * nb + n, 0)),
        out_shape=jax.ShapeDtypeStruct((N_TOK, ATT_Q + DN_V), BF16),
        input_output_aliases={5: 0},
        compiler_params=_cparams("arbitrary", "arbitrary"),
        name="attn_lat",
    )(sink_l, q, kv, cache_k, cache_v, mix)


INV_PASSES = 1
GDN_UNROLL = 4
GDN_CTX_SEQS = 2


def _split_bf16(x):
    hi = x.astype(BF16)
    return hi, (x - hi.astype(F32)).astype(BF16)


def _block_diag(x, width):
    n = x.shape[1] // width
    blk = lax.broadcasted_iota(jnp.int32, x.shape, 1) // width
    zero = jnp.zeros_like(x)
    return jnp.concatenate([jnp.where(blk == h, x, zero) for h in range(n)], axis=0)


def _mm_heads(ts, xs, width):
    if INV_PASSES == 1:
        bds = [_block_diag(x.astype(BF16), width) for x in xs]
        return [jnp.dot(t.astype(BF16), bd, preferred_element_type=F32) for t, bd in zip(ts, bds)]
    tsp = [_split_bf16(t) for t in ts]
    xsp = [_split_bf16(x) for x in xs]
    bd_hi = [_block_diag(x_hi, width) for x_hi, _ in xsp]
    bd_lo = [_block_diag(x_lo, width) for _, x_lo in xsp]
    return [jnp.dot(t_hi, bh, preferred_element_type=F32) + jnp.dot(t_lo, bh, preferred_element_type=F32)
            + jnp.dot(t_hi, bl, preferred_element_type=F32) for (t_hi, t_lo), bh, bl in zip(tsp, bd_hi, bd_lo)]


def _unit_tri_inverse(mats, row, col):
    eye = (row == col).astype(F32)
    a8 = [jnp.where((row // 8) == (col // 8), a, 0.0) for a in mats]
    a8_2 = _mm_heads(a8, a8, CHUNK)
    a8_4 = _mm_heads(a8_2, a8_2, CHUNK)
    ts = [eye - a for a in a8]
    ts = [t + p for t, p in zip(ts, _mm_heads(ts, a8_2, CHUNK))]
    ts = [t + p for t, p in zip(ts, _mm_heads(ts, a8_4, CHUNK))]
    b = 8
    while b < CHUNK:
        level = ((row // (2 * b)) == (col // (2 * b))) & ((row // b) != (col // b))
        off = [jnp.where(level, a, 0.0) for a in mats]
        ts = [t - p for t, p in zip(ts, _mm_heads(_mm_heads(ts, off, CHUNK), ts, CHUNK))]
        b *= 2
    return ts


def _bcast_cols(x, first, width):
    rows = x.shape[0]
    if width == LANES:
        return jnp.concatenate([jnp.broadcast_to(x[:, first + h:first + h + 1], (rows, width)) for h in range(DN_HEADS)], axis=1)
    blk = lax.broadcasted_iota(jnp.int32, (rows, DN_HEADS * width), 1) // width
    out = jnp.broadcast_to(x[:, first:first + 1], (rows, DN_HEADS * width))
    for h in range(1, DN_HEADS):
        out = jnp.where(blk == h, jnp.broadcast_to(x[:, first + h:first + h + 1], (rows, DN_HEADS * width)), out)
    return out


def _gdn_kernel(*refs, seq_len, n_seq, has_s0, write_state, n_alias):
    qkvd_ref, z_ref, ab_ref, convw_ref, alog_ref, dtb_ref, ogain_ref = refs[:7]
    pos = 7
    s0_ref = None
    if has_s0:
        s0_ref = refs[pos]
        pos += 1
    pos += n_alias
    dn_ref = refs[pos]
    pos += 1
    sfin_ref = None
    if write_state:
        sfin_ref = refs[pos]
        pos += 1
    qn_s, kn_s, vn_s, gb_s, u0_s, w_s, qd_s, kd_s, qk_s, gt_s, o_s, st_s, xpad_s = refs[pos:]
    L = seq_len
    n_chunks = L // CHUNK
    n_pairs = DN_HEADS // 2
    pair_w = 2 * DK
    pad = (DN_CONV - 1) // 2

    halo = 8

    def finish(y, kind):
        y = _silu(y)
        if kind == 0:
            return y * lax.rsqrt(jnp.sum(y * y, axis=-1, keepdims=True) + EPS) * (DK ** -0.5)
        if kind == 1:
            return y * lax.rsqrt(jnp.sum(y * y, axis=-1, keepdims=True) + EPS)
        return y

    xpad_s[0:halo, :] = jnp.zeros((halo, LANES), F32)
    xpad_s[halo + L:2 * halo + L, :] = jnp.zeros((halo, LANES), F32)
    for kind, dst in enumerate((qn_s, kn_s, vn_s)):
        def conv_head(head, carry, kind=kind, dst=dst):
            cs = pl.ds(pl.multiple_of((kind * DN_HEADS + head) * LANES, LANES), LANES)
            hs = pl.ds(pl.multiple_of(head * DK, DK), DK)
            for s in range(n_seq):
                r0 = s * L
                for rb in range(0, L, SEQ):
                    xpad_s[halo + rb:halo + rb + SEQ, :] = qkvd_ref[r0 + rb:r0 + rb + SEQ, cs].astype(F32)
                for rb in range(0, L, SEQ):
                    acc = None
                    for i in range(DN_CONV):
                        lo = halo + rb + i - pad
                        term = xpad_s[lo:lo + SEQ, :] * convw_ref[i:i + 1, cs]
                        acc = term if acc is None else acc + term
                    dst[r0 + rb:r0 + rb + SEQ, hs] = finish(acc, kind)
            return carry

        lax.fori_loop(0, DN_HEADS, conv_head, 0)

    ab = ab_ref[...]
    glane = lax.broadcasted_iota(jnp.int32, ab.shape, 1)
    g = -jnp.exp(alog_ref[...]) * _softplus(ab + dtb_ref[...])
    gb_s[...] = jnp.where(glane < N_DIR * DN_HEADS, g, _sigmoid(ab))

    for s in range(n_seq):
        for d in range(N_DIR):
            for p in range(n_pairs):
                idx = (s * N_DIR + d) * n_pairs + p
                st_s[idx] = jnp.zeros((pair_w, pair_w), F32)
                if has_s0:
                    st_s[idx, 0:DK, 0:DV] = s0_ref[s, d, 2 * p]
                    st_s[idx, DK:2 * DK, DV:2 * DV] = s0_ref[s, d, 2 * p + 1]

    row = lax.broadcasted_iota(jnp.int32, (CHUNK, DN_HEADS * CHUNK), 0)
    col = lax.broadcasted_iota(jnp.int32, (CHUNK, DN_HEADS * CHUNK), 1) % CHUNK
    before_incl = [col <= row, col >= row]
    before_strict = [col < row, col > row]
    after_strict01 = [m.astype(F32) for m in before_strict]
    r64 = lax.broadcasted_iota(jnp.int32, (CHUNK, CHUNK), 0)
    c64 = lax.broadcasted_iota(jnp.int32, (CHUNK, CHUNK), 1)
    tri01 = [(c64 <= r64).astype(BF16), (c64 >= r64).astype(BF16)]

    def prep_step(n, carry):
        a_mats, v_betas, kb_egcs, slots = [], [], [], []
        for j in range(GDN_UNROLL):
            c = n * GDN_UNROLL + j
            rs = pl.ds(pl.multiple_of(c * CHUNK, CHUNK), CHUNK)
            gbc = gb_s[rs, :]
            q = qn_s[rs, :]
            k = kn_s[rs, :]
            v = vn_s[rs, :]
            k_bd = _block_diag(k.astype(BF16), DK)
            betas = [_bcast_cols(gbc, N_DIR * DN_HEADS + d * DN_HEADS, DK) for d in range(N_DIR)]
            kbs = [k * beta for beta in betas]
            kk_qk = _dot_nt(jnp.concatenate(kbs + [q], axis=0), k_bd)
            qk_raw = kk_qk[N_DIR * CHUNK:(N_DIR + 1) * CHUNK]
            for d in range(N_DIR):
                first = d * DN_HEADS
                gc_all = _dot_split(tri01[d], gbc)
                last = CHUNK - 1 if d == 0 else 0
                gc_last = gc_all[last:last + 1, :]
                gt_s[d, pl.ds(c, 1), :] = jnp.exp(gc_last)
                e_gc = _bcast_cols(jnp.exp(gc_all), first, DK)
                e_kd = _bcast_cols(jnp.exp(gc_last - gc_all), first, DK)
                gdiff = _dot_split(tri01[d], _bcast_cols(gbc, first, CHUNK) * after_strict01[d])
                decay = jnp.where(before_incl[d], jnp.exp(gdiff), 0.0)
                a_mats.append(jnp.where(before_strict[d], kk_qk[d * CHUNK:(d + 1) * CHUNK] * decay, 0.0))
                qk_s[d, rs, :] = jnp.where(before_incl[d], qk_raw * decay, 0.0).astype(BF16)
                qd_s[d, rs, :] = (q * e_gc).astype(BF16)
                kd_s[d, rs, :] = (k * e_kd).astype(BF16)
                v_betas.append(v * betas[d])
                kb_egcs.append(kbs[d] * e_gc)
                slots.append((d, rs))
        t_invs = _unit_tri_inverse(a_mats, row, col)
        for (d, rs), u0, w in zip(slots, _mm_heads(t_invs, v_betas, DV), _mm_heads(t_invs, kb_egcs, DK)):
            u0_s[d, rs, :] = u0
            w_s[d, rs, :] = w.astype(BF16)
        return carry

    lax.fori_loop(0, n_seq * n_chunks // GDN_UNROLL, prep_step, 0)

    plane = lax.broadcasted_iota(jnp.int32, (1, pair_w), 1)
    srow = lax.broadcasted_iota(jnp.int32, (pair_w, pair_w), 0) // DK
    scol = lax.broadcasted_iota(jnp.int32, (pair_w, pair_w), 1) // DV
    same_head = srow == scol

    def scan_step(n, carry):
        probs = []
        for s in range(n_seq):
            for d in range(N_DIR):
                c = s * n_chunks + (n if d == 0 else n_chunks - 1 - n)
                rs = pl.ds(pl.multiple_of(c * CHUNK, CHUNK), CHUNK)
                gt = gt_s[d, pl.ds(c, 1), :]
                for p in range(n_pairs):
                    c0 = d * DN_HEADS + 2 * p
                    g_tot = jnp.where(plane < DV, gt[:, c0:c0 + 1], gt[:, c0 + 1:c0 + 2])
                    probs.append((d, rs, p, (s * N_DIR + d) * n_pairs + p, slice(p * pair_w, (p + 1) * pair_w), g_tot))
        s_prev = [st_s[idx] for _, _, _, idx, _, _ in probs]
        s_b = [s.astype(BF16) for s in s_prev]
        ws = [jnp.dot(w_s[d, rs, ps], sb, preferred_element_type=F32) for (d, rs, _, _, ps, _), sb in zip(probs, s_b)]
        u_b = [(u0_s[d, rs, ps] - w).astype(BF16) for (d, rs, _, _, ps, _), w in zip(probs, ws)]
        upd = [lax.dot_general(kd_s[d, rs, ps], u, (((0,), (0,)), ((), ())), preferred_element_type=F32)
               for (d, rs, _, _, ps, _), u in zip(probs, u_b)]
        for (_, _, _, idx, _, g_tot), s, up in zip(probs, s_prev, upd):
            st_s[idx] = s * g_tot + jnp.where(same_head, up, 0.0)
        for (d, rs, p, _, ps, _), sb, u in zip(probs, s_b, u_b):
            o_s[d, rs, ps] = (jnp.dot(qd_s[d, rs, ps], sb, preferred_element_type=F32)
                              + jnp.dot(qk_s[d, rs, p * 2 * CHUNK:(p + 1) * 2 * CHUNK], _block_diag(u, DV),
                                        preferred_element_type=F32))
        return carry

    lax.fori_loop(0, n_chunks, scan_step, 0)

    for h in range(DN_HEADS):
        hs = slice(h * DV, (h + 1) * DV)
        o = o_s[0, :, hs] + o_s[1, :, hs]
        y = o * lax.rsqrt(jnp.mean(o * o, axis=-1, keepdims=True) + EPS) * ogain_ref[...]
        dn_ref[:, hs] = (y * _silu(z_ref[:, hs].astype(F32))).astype(dn_ref.dtype)
    if write_state:
        for s in range(n_seq):
            for d in range(N_DIR):
                for p in range(n_pairs):
                    idx = (s * N_DIR + d) * n_pairs + p
                    sfin_ref[s, d, 2 * p] = st_s[idx, 0:DK, 0:DV]
                    sfin_ref[s, d, 2 * p + 1] = st_s[idx, DK:2 * DK, DV:2 * DV]


def _gdn_call(l, qkvd, z, ab, conv_w, alog16, dtb16, o_gain, state_dn, mix, new_state=None, *, latent):
    L = DEC_SEQ if latent else SEQ
    n_seq = 1 if latent else GDN_CTX_SEQS
    n_steps = (DEC_BATCH if latent else BATCH) // n_seq
    rows = n_seq * L
    base = N_CTX_TOK // rows if latent else 0
    seq = lambda w: pl.BlockSpec((rows, w), lambda b: (base + b, 0))
    in_specs = [
        seq(DN_CONV_CH), seq(DN_V), seq(N_GATE_COLS),
        pl.BlockSpec((None, DN_CONV, DN_CONV_CH), lambda b: (l, 0, 0)),
        pl.BlockSpec((None, 1, N_GATE_COLS), lambda b: (l, 0, 0)),
        pl.BlockSpec((None, 1, N_GATE_COLS), lambda b: (l, 0, 0)),
        pl.BlockSpec((None, 1, DV), lambda b: (l, 0, 0)),
    ]
    args = [qkvd, z, ab, conv_w, alog16, dtb16, o_gain]
    out_specs = [pl.BlockSpec((rows, DN_V), lambda b: (base + b, 1))]
    out_shape = [jax.ShapeDtypeStruct((N_TOK, ATT_Q + DN_V), BF16)]
    if latent:
        in_specs.append(pl.BlockSpec((n_seq, None, N_DIR, DN_HEADS, DK, DV), lambda b: (b, l, 0, 0, 0, 0)))
        args.append(state_dn)
    in_specs.append(pl.BlockSpec(memory_space=pl.ANY))
    args.append(mix)
    aliases = {len(args) - 1: 0}
    if not latent:
        out_specs.append(pl.BlockSpec((n_seq, None, N_DIR, DN_HEADS, DK, DV), lambda b: (b, l, 0, 0, 0, 0)))
        out_shape.append(jax.ShapeDtypeStruct((BATCH, DEPTH, N_DIR, DN_HEADS, DK, DV), F32))
        if new_state is not None:
            in_specs.append(pl.BlockSpec(memory_space=pl.ANY))
            args.append(new_state)
            aliases[len(args) - 1] = 1
    n_chunks = rows // CHUNK
    return pl.pallas_call(
        functools.partial(_gdn_kernel, seq_len=L, n_seq=n_seq, has_s0=latent, write_state=not latent,
                          n_alias=len(aliases)),
        grid=(n_steps,),
        in_specs=in_specs,
        out_specs=out_specs,
        out_shape=out_shape,
        scratch_shapes=[
            pltpu.VMEM((rows, DN_QK), F32), pltpu.VMEM((rows, DN_QK), F32), pltpu.VMEM((rows, DN_V), F32),
            pltpu.VMEM((rows, N_GATE_COLS), F32),
            pltpu.VMEM((N_DIR, rows, DN_V), F32),
            pltpu.VMEM((N_DIR, rows, DN_QK), BF16),
            pltpu.VMEM((N_DIR, rows, DN_QK), BF16),
            pltpu.VMEM((N_DIR, rows, DN_QK), BF16),
            pltpu.VMEM((N_DIR, rows, DN_HEADS * CHUNK), BF16),
            pltpu.VMEM((N_DIR, n_chunks, N_GATE_COLS), F32),
            pltpu.VMEM((N_DIR, rows, DN_V), F32),
            pltpu.VMEM((n_seq * N_DIR * (DN_HEADS // 2), 2 * DK, 2 * DV), F32),
            pltpu.VMEM((L + 16, LANES), F32),
        ],
        input_output_aliases=aliases,
        compiler_params=_cparams("arbitrary"),
        name="gdn_lat" if latent else "gdn_ctx",
    )(*args)


def _top2_of4(vals):
    m1 = jnp.maximum(jnp.maximum(vals[0], vals[1]), jnp.maximum(vals[2], vals[3]))
    i1 = jnp.where(vals[0] == m1, 0, jnp.where(vals[1] == m1, 1, jnp.where(vals[2] == m1, 2, 3)))
    rest = [jnp.where(i1 == j, -jnp.inf, vals[j]) for j in range(4)]
    m2 = jnp.maximum(jnp.maximum(rest[0], rest[1]), jnp.maximum(rest[2], rest[3]))
    i2 = jnp.where(rest[0] == m2, 0, jnp.where(rest[1] == m2, 1, jnp.where(rest[2] == m2, 2, 3)))
    return m1, i1, m2, i2


def _outproj_kernel(x_ref, mix_ref, mod_ref, gain_ref, wo_ref, rwt_ref, rb_ref, ut_ref,
                    x1_ref, h2_ref, ri_ref, rg_ref, cnt_ref, carry_s):
    i = pl.program_id(0)

    @pl.when(i == 0)
    def _():
        carry_s[...] = jnp.zeros_like(carry_s)

    gate1 = mod_ref[:, 2 * D_MODEL:3 * D_MODEL]
    shift2 = mod_ref[:, 3 * D_MODEL:4 * D_MODEL]
    scale2 = mod_ref[:, 4 * D_MODEL:5 * D_MODEL]
    x1 = x_ref[...] + gate1 * jnp.dot(mix_ref[...], wo_ref[...], preferred_element_type=F32)
    x1_ref[...] = x1
    h2 = _adaln(x1, gain_ref[...], shift2, scale2)
    h2_ref[...] = h2

    h_hi, h_lo = _split_bf16(h2)
    r_hi, r_lo = _split_bf16(rwt_ref[...])
    nt = (((1,), (1,)), ((), ()))
    logits = (lax.dot_general(r_hi, h_hi, nt, preferred_element_type=F32)
              + lax.dot_general(r_hi, h_lo, nt, preferred_element_type=F32)
              + lax.dot_general(r_lo, h_hi, nt, preferred_element_type=F32))
    scores = _sigmoid(logits)
    biased = scores + rb_ref[...]
    rows = [biased[e:e + 1, :] for e in range(N_EXPERTS)]
    per_group = [_top2_of4(rows[g * 4:(g + 1) * 4]) for g in range(N_GROUPS)]
    gs = [pg[0] + pg[2] for pg in per_group]
    _, gsel, _, _ = _top2_of4(gs)
    pick = lambda k: jnp.where(gsel == 0, per_group[0][k], jnp.where(gsel == 1, per_group[1][k],
                               jnp.where(gsel == 2, per_group[2][k], per_group[3][k])))
    e1 = gsel * EXPERTS_PER_GROUP + pick(1)
    e2 = gsel * EXPERTS_PER_GROUP + pick(3)
    erow = lax.broadcasted_iota(jnp.int32, scores.shape, 0)
    oh1 = erow == e1
    oh2 = erow == e2
    s1 = jnp.sum(jnp.where(oh1, scores, 0.0), axis=0, keepdims=True)
    s2 = jnp.sum(jnp.where(oh2, scores, 0.0), axis=0, keepdims=True)
    tot = s1 + s2
    grow = lax.broadcasted_iota(jnp.int32, rg_ref.shape, 0)
    rg_ref[...] = jnp.where(grow == 0, s1 / tot, s2 / tot)

    sel = (oh1 | oh2).astype(BF16)
    before = jnp.dot(sel, ut_ref[...], preferred_element_type=F32) + carry_s[...]
    r1 = jnp.sum(jnp.where(oh1, before, 0.0), axis=0, keepdims=True)
    r2 = jnp.sum(jnp.where(oh2, before, 0.0), axis=0, keepdims=True)
    carry_s[...] = carry_s[...] + jnp.sum(sel.astype(F32), axis=1, keepdims=True)
    cnt_ref[...] = carry_s[...].astype(jnp.int32)
    irow = lax.broadcasted_iota(jnp.int32, ri_ref.shape, 0)
    ri_ref[...] = jnp.where(irow == 0, e1, jnp.where(irow == 1, e2, jnp.where(
        irow == 2, r1.astype(jnp.int32), r2.astype(jnp.int32))))


def _outproj_call(l, x, mix, mod, gain, w_out, router_wt, router_b, ut):
    tok = lambda w: pl.BlockSpec((TOK_TILE, w), lambda i: (i, 0))
    per_tok = lambda r: pl.BlockSpec((r, TOK_TILE), lambda i: (0, i))
    full = lambda a: pl.BlockSpec(a.shape, lambda i: (0,) * a.ndim)
    return pl.pallas_call(
        _outproj_kernel,
        grid=(N_TOK // TOK_TILE,),
        in_specs=[
            tok(D_MODEL), tok(ATT_Q + DN_V),
            pl.BlockSpec((None, None, 1, 6 * D_MODEL), lambda i: (l, _mod_row(i), 0, 0)),
            pl.BlockSpec((None, 1, D_MODEL), lambda i: (l, 0, 0)),
            pl.BlockSpec((None, ATT_Q + DN_V, D_MODEL), lambda i: (l, 0, 0)),
            full(router_wt), full(router_b), full(ut),
        ],
        out_specs=[tok(D_MODEL), tok(D_MODEL), per_tok(4), per_tok(2), pl.BlockSpec((N_EXPERTS, 1), lambda i: (0, 0))],
        out_shape=[
            jax.ShapeDtypeStruct((N_TOK, D_MODEL), F32),
            jax.ShapeDtypeStruct((N_TOK, D_MODEL), F32),
            jax.ShapeDtypeStruct((4, N_TOK), jnp.int32),
            jax.ShapeDtypeStruct((2, N_TOK), F32),
            jax.ShapeDtypeStruct((N_EXPERTS, 1), jnp.int32),
        ],
        scratch_shapes=[pltpu.VMEM((N_EXPERTS, 1), F32)],
        compiler_params=_cparams("arbitrary"),
        name="outproj_router",
    )(x, mix, mod, gain, w_out, router_wt, router_b, ut)


def _dispatch_kernel(pos_ref, h2_ref, buf_in_ref, sorted_ref, sem):
    del buf_in_ref

    def row_copy(r, k):
        return pltpu.make_async_copy(h2_ref.at[pl.ds(r, 1), :], sorted_ref.at[pl.ds(pos_ref[0, k * TOK_TILE + r], 1), :], sem)

    def start(r, c):
        row_copy(r, 0).start()
        row_copy(r, 1).start()
        return c

    def wait(r, c):
        row_copy(r, 0).wait()
        row_copy(r, 1).wait()
        return c

    lax.fori_loop(0, TOK_TILE, start, 0, unroll=ROW_DMA_UNROLL)
    lax.fori_loop(0, TOK_TILE, wait, 0, unroll=ROW_DMA_UNROLL)


def _dispatch_call(pos_tiles, h2, zeros_sorted):
    return pl.pallas_call(
        _dispatch_kernel,
        grid=(N_TOK // TOK_TILE,),
        in_specs=[
            pl.BlockSpec((None, 1, TOP_K * TOK_TILE), lambda i: (i, 0, 0), memory_space=pltpu.SMEM),
            pl.BlockSpec((TOK_TILE, D_MODEL), lambda i: (i, 0)),
            pl.BlockSpec(memory_space=pl.ANY),
        ],
        out_specs=pl.BlockSpec(memory_space=pl.ANY),
        out_shape=jax.ShapeDtypeStruct((N_SORTED, D_MODEL), F32),
        scratch_shapes=[pltpu.SemaphoreType.DMA(())],
        input_output_aliases={2: 0},
        compiler_params=_cparams("arbitrary"),
        name="moe_dispatch",
    )(pos_tiles, h2, zeros_sorted)


def _experts_kernel(te_ref, nu_ref, x_ref, wg_ref, wu_ref, wd_ref, y_ref, wg_s, wu_s, wd_s):
    j = pl.program_id(0)
    used = j < nu_ref[0]

    @pl.when((j == 0) | (te_ref[j] != te_ref[jnp.maximum(j - 1, 0)]))
    def _():
        wg_s[...] = wg_ref[...].astype(BF16)
        wu_s[...] = wu_ref[...].astype(BF16)
        wd_s[...] = wd_ref[...].astype(BF16)

    @pl.when(used)
    def _():
        x = x_ref[...].astype(BF16)
        hid = _silu(jnp.dot(x, wg_s[...], preferred_element_type=F32)) * jnp.dot(x, wu_s[...], preferred_element_type=F32)
        y_ref[...] = jnp.dot(hid.astype(BF16), wd_s[...], preferred_element_type=F32)

    @pl.when(jnp.logical_not(used))
    def _():
        y_ref[...] = jnp.zeros_like(y_ref)


def _experts_call(l, tile_expert, n_used, xs, w_gate, w_up, w_down):
    row = lambda j, te, nu: (jnp.maximum(jnp.minimum(j, nu[0] - 1), 0), 0)
    grid_spec = pltpu.PrefetchScalarGridSpec(
        num_scalar_prefetch=2,
        grid=(N_EXP_TILES,),
        in_specs=[
            pl.BlockSpec((EXP_TILE, D_MODEL), row),
            pl.BlockSpec((None, None, D_MODEL, D_FF), lambda j, te, nu: (l, te[j], 0, 0)),
            pl.BlockSpec((None, None, D_MODEL, D_FF), lambda j, te, nu: (l, te[j], 0, 0)),
            pl.BlockSpec((None, None, D_FF, D_MODEL), lambda j, te, nu: (l, te[j], 0, 0)),
        ],
        out_specs=pl.BlockSpec((EXP_TILE, D_MODEL), lambda j, te, nu: (j, 0)),
        scratch_shapes=[pltpu.VMEM((D_MODEL, D_FF), BF16), pltpu.VMEM((D_MODEL, D_FF), BF16),
                        pltpu.VMEM((D_FF, D_MODEL), BF16)],
    )
    return pl.pallas_call(
        _experts_kernel,
        grid_spec=grid_spec,
        out_shape=jax.ShapeDtypeStruct((N_SORTED, D_MODEL), F32),
        compiler_params=_cparams("arbitrary"),
        name="moe_experts",
    )(tile_expert, n_used, xs, w_gate, w_up, w_down)


def _combine_kernel(pos_ref, x1_ref, rg_ref, mod_ref, y_ref, out_ref, buf_s, sem):
    def row_copy(r, k):
        return pltpu.make_async_copy(y_ref.at[pl.ds(pos_ref[0, k * TOK_TILE + r], 1), :], buf_s.at[k, pl.ds(r, 1), :], sem)

    def start(r, c):
        row_copy(r, 0).start()
        row_copy(r, 1).start()
        return c

    def wait(r, c):
        row_copy(r, 0).wait()
        row_copy(r, 1).wait()
        return c

    lax.fori_loop(0, TOK_TILE, start, 0, unroll=ROW_DMA_UNROLL)
    lax.fori_loop(0, TOK_TILE, wait, 0, unroll=ROW_DMA_UNROLL)
    gate2 = mod_ref[:, 5 * D_MODEL:6 * D_MODEL]
    moe = buf_s[0] * rg_ref[:, 0:1] + buf_s[1] * rg_ref[:, 1:2]
    out_ref[...] = x1_ref[...] + gate2 * moe


def _combine_call(l, pos_tiles, x1, rg, mod, y_sorted):
    tok = lambda w: pl.BlockSpec((TOK_TILE, w), lambda i: (i, 0))
    return pl.pallas_call(
        _combine_kernel,
        grid=(N_TOK // TOK_TILE,),
        in_specs=[
            pl.BlockSpec((None, 1, TOP_K * TOK_TILE), lambda i: (i, 0, 0), memory_space=pltpu.SMEM),
            tok(D_MODEL), tok(2),
            pl.BlockSpec((None, None, 1, 6 * D_MODEL), lambda i: (l, _mod_row(i), 0, 0)),
            pl.BlockSpec(memory_space=pl.ANY),
        ],
        out_specs=tok(D_MODEL),
        out_shape=jax.ShapeDtypeStruct((N_TOK, D_MODEL), F32),
        scratch_shapes=[pltpu.VMEM((TOP_K, TOK_TILE, D_MODEL), F32), pltpu.SemaphoreType.DMA(())],
        compiler_params=_cparams("arbitrary"),
        name="moe_combine",
    )(pos_tiles, x1, rg, mod, y_sorted)


def _rope_tables():
    pos = jnp.arange(DEC_SEQ)
    r = (pos // GRID_W).astype(F32)
    c = (pos % GRID_W).astype(F32)
    inv = ROPE_BASE ** (-jnp.arange(ROPE_PAIRS, dtype=F32) / ROPE_PAIRS)
    ar, ac = r[:, None] * inv, c[:, None] * inv
    cos = jnp.concatenate([jnp.cos(ar), jnp.cos(ar), jnp.cos(ac), jnp.cos(ac)], axis=-1)
    sin = jnp.concatenate([-jnp.sin(ar), jnp.sin(ar), -jnp.sin(ac), jnp.sin(ac)], axis=-1)
    return jnp.tile(cos, (1, N_HEADS)), jnp.tile(sin, (1, N_HEADS))


def _routing_layout(ri, counts):
    counts = counts.reshape(N_EXPERTS)
    padded = ((counts + EXP_TILE - 1) // EXP_TILE) * EXP_TILE
    ends = jnp.cumsum(padded)
    offs = ends - padded
    eids = jnp.arange(N_EXPERTS, dtype=jnp.int32)[:, None, None]
    pos = jnp.sum(jnp.where(ri[None, 0:2] == eids, offs[:, None, None], 0), axis=0) + ri[2:4]
    n_used = (ends[-1] // EXP_TILE).astype(jnp.int32)
    tile_start = jnp.arange(N_EXP_TILES, dtype=jnp.int32) * EXP_TILE
    tile_expert = jnp.sum((tile_start[:, None] >= ends[None, :]).astype(jnp.int32), axis=1)
    last = jnp.sum((jnp.maximum(ends[-1] - EXP_TILE, 0) >= ends).astype(jnp.int32))
    tile_expert = jnp.minimum(jnp.where(tile_start < ends[-1], tile_expert, last), N_EXPERTS - 1).astype(jnp.int32)
    pos_tiles = pos.reshape(TOP_K, N_TOK // TOK_TILE, TOK_TILE).transpose(1, 0, 2).reshape(
        N_TOK // TOK_TILE, 1, TOP_K * TOK_TILE).astype(jnp.int32)
    return pos_tiles, tile_expert, n_used.reshape(1)


def kernel(x_prompt, x_sample, cache_k, cache_v, state_dn, c, c_ctx, w_ada, b_ada, norm_attn, norm_ffn,
           w_in, conv_w, a_log, dt_bias, q_norm, k_norm, sink, o_norm, w_out, router_w, router_bias,
           w_gate, w_up, w_down):
    x = jnp.concatenate([x_prompt.reshape(N_CTX_TOK, D_MODEL), x_sample.reshape(N_LAT_TOK, D_MODEL)], axis=0)
    cond = jnp.concatenate([c_ctx[None, :], c, jnp.zeros((N_COND - 1 - DEC_BATCH, D_MODEL), F32)], axis=0)
    mod = _modulation_call(cond, w_ada, b_ada).reshape(DEPTH, N_COND, 1, 6 * D_MODEL)

    w_in_b = w_in.astype(BF16)
    w_out_b = w_out.astype(BF16)
    seg = jnp.arange(ATT_Q) // HEAD_DIM
    bd = jnp.where(seg[:, None] == seg[None, :], 1.0 / HEAD_DIM, 0.0).astype(BF16)
    qg = jnp.tile(q_norm, (1, N_HEADS)).reshape(DEPTH, 1, ATT_Q)
    kg = jnp.tile(k_norm, (1, N_KV)).reshape(DEPTH, 1, ATT_KV)
    cos, sin = _rope_tables()
    gain1 = norm_attn.reshape(DEPTH, 1, D_MODEL)
    gain2 = norm_ffn.reshape(DEPTH, 1, D_MODEL)
    pad8 = lambda a: jnp.pad(a.reshape(DEPTH, 1, N_DIR * DN_HEADS), ((0, 0), (0, 0), (0, N_GATE_COLS - N_DIR * DN_HEADS)))
    alog16, dtb16 = pad8(a_log), pad8(dt_bias)
    o_gain = o_norm.reshape(DEPTH, 1, DV)
    ck = cache_k.reshape(DEC_BATCH, DEPTH, PAST_LEN, ATT_KV)
    cv = cache_v.reshape(DEC_BATCH, DEPTH, PAST_LEN, ATT_KV)
    router_wt = router_w.T
    rb = router_bias.reshape(N_EXPERTS, 1)
    tri = jnp.arange(TOK_TILE)
    ut = (tri[:, None] < tri[None, :]).astype(BF16)
    sorted_buf = jnp.zeros((N_SORTED, D_MODEL), F32)

    k_list, v_list, new_state = [], [], None
    for l in range(DEPTH):
        q, kv, qkvd, z, ab = _inproj_call(l, x, mod, gain1, w_in_b, bd, qg, kg, cos, sin)
        k_list.append(kv[:N_CTX_TOK, 0:ATT_KV].reshape(BATCH, SEQ, N_KV, HEAD_DIM))
        v_list.append(kv[:N_CTX_TOK, ATT_KV:].reshape(BATCH, SEQ, N_KV, HEAD_DIM))
        mix = _attn_ctx_call(sink[l], q, kv)
        mix = _attn_lat_call(l, sink[l], q, kv, ck, cv, mix)
        mix, new_state = _gdn_call(l, qkvd, z, ab, conv_w, alog16, dtb16, o_gain, state_dn, mix, new_state, latent=False)
        (mix,) = _gdn_call(l, qkvd, z, ab, conv_w, alog16, dtb16, o_gain, state_dn, mix, latent=True)
        x1, h2, ri, rg, counts = _outproj_call(l, x, mix, mod, gain2, w_out_b, router_wt, rb, ut)
        pos_tiles, tile_expert, n_used = _routing_layout(ri, counts)
        xs_sorted = _dispatch_call(pos_tiles, h2, sorted_buf)
        sorted_buf = xs_sorted
        y_sorted = _experts_call(l, tile_expert, n_used, xs_sorted, w_gate, w_up, w_down)
        x = _combine_call(l, pos_tiles, x1, rg.T, mod, y_sorted)

    y_prompt = x[:N_CTX_TOK].reshape(BATCH, SEQ, D_MODEL)
    y_sample = x[N_CTX_TOK:].reshape(DEC_BATCH, DEC_SEQ, D_MODEL)
    return (y_prompt, y_sample, jnp.stack(k_list, axis=1), jnp.stack(v_list, axis=1), new_state)
```

```python
import functools

import jax
import jax.numpy as jnp
from jax import lax
from jax.experimental import pallas as pl
from jax.experimental.pallas import tpu as pltpu

D_MODEL = 1024
BATCH = 32
SEQ = 256
DEPTH = 4
DEC_BATCH = 4
DEC_SEQ = 1024
PAST_LEN = 256
GRID_W = 64
N_HEADS = 8
N_KV = 2
HEAD_DIM = 64
WINDOW = 128
BLOCK = 128
ATT_SCALE = HEAD_DIM ** -0.5
ROPE_BASE = 10000.0
ROPE_PAIRS = HEAD_DIM // 4
DN_HEADS = 4
DK = 128
DV = 128
DN_CONV = 5
CHUNK = 64
N_DIR = 2
ATT_Q = N_HEADS * HEAD_DIM
ATT_KV = N_KV * HEAD_DIM
DN_QK = DN_HEADS * DK
DN_V = DN_HEADS * DV
DN_CONV_CH = 2 * DN_QK + DN_V
N_GATE_COLS = 2 * N_DIR * DN_HEADS
IN_DIM = ATT_Q + 2 * ATT_KV + DN_CONV_CH + DN_V + N_GATE_COLS
N_EXPERTS = 16
N_GROUPS = 4
EXPERTS_PER_GROUP = 4
TOP_K = 2
D_FF = 512
EPS = 1e-6

N_CTX_TOK = BATCH * SEQ
N_LAT_TOK = DEC_BATCH * DEC_SEQ
N_TOK = N_CTX_TOK + N_LAT_TOK
N_COND = 8

LANES = 128
TOK_TILE = 512
EXP_TILE = 512
N_EXP_TILES = (N_TOK * TOP_K) // EXP_TILE + N_EXPERTS
N_SORTED = N_EXP_TILES * EXP_TILE
VMEM_LIMIT = 56 * 1024 * 1024
LAT_Q_ROWS = 256
ROW_DMA_UNROLL = 8

F32 = jnp.float32
BF16 = jnp.bfloat16


def _cparams(*sem):
    return pltpu.CompilerParams(dimension_semantics=sem, vmem_limit_bytes=VMEM_LIMIT)


def _dot(a, b):
    return jnp.dot(a.astype(BF16), b.astype(BF16), preferred_element_type=F32)


def _dot_nt(a, b):
    return lax.dot_general(a.astype(BF16), b.astype(BF16), (((1,), (1,)), ((), ())), preferred_element_type=F32)


def _dot_tn(a, b):
    return lax.dot_general(a.astype(BF16), b.astype(BF16), (((0,), (0,)), ((), ())), preferred_element_type=F32)


def _dot_f32(a, b):
    return jnp.dot(a, b, preferred_element_type=F32, precision=lax.Precision.HIGHEST)


def _dot_split(mat01, x):
    hi = x.astype(BF16)
    lo = (x - hi.astype(F32)).astype(BF16)
    return (jnp.dot(mat01, hi, preferred_element_type=F32) + jnp.dot(mat01, lo, preferred_element_type=F32))


def _sigmoid(x):
    return 1.0 / (1.0 + jnp.exp(-x))


def _silu(x):
    return x * _sigmoid(x)


def _softplus(x):
    return jnp.maximum(x, 0.0) + jnp.log(1.0 + jnp.exp(-jnp.abs(x)))


MOD_TN = 1536


def _mod_kernel(cond_ref, w_ref, b_ref, o_ref):
    c = _silu(cond_ref[...])
    o_ref[...] = _dot(c, w_ref[...]) + b_ref[...]


def _modulation_call(cond, w_ada, b_ada):
    n_col = 6 * D_MODEL
    return pl.pallas_call(
        _mod_kernel,
        grid=(DEPTH, n_col // MOD_TN),
        in_specs=[
            pl.BlockSpec((N_COND, D_MODEL), lambda l, j: (0, 0)),
            pl.BlockSpec((None, D_MODEL, MOD_TN), lambda l, j: (l, 0, j)),
            pl.BlockSpec((None, 1, MOD_TN), lambda l, j: (l, 0, j)),
        ],
        out_specs=pl.BlockSpec((None, N_COND, MOD_TN), lambda l, j: (l, 0, j)),
        out_shape=jax.ShapeDtypeStruct((DEPTH, N_COND, n_col), F32),
        compiler_params=_cparams("arbitrary", "arbitrary"),
        name="modulation",
    )(cond, w_ada, b_ada.reshape(DEPTH, 1, n_col))


def _mod_row(i):
    n_ctx = N_CTX_TOK // TOK_TILE
    per_seq = DEC_SEQ // TOK_TILE
    return jnp.where(i < n_ctx, 0, 1 + (jnp.maximum(i - n_ctx, 0)) // per_seq)


def _adaln(x, gain, shift, scale):
    ms = jnp.mean(x * x, axis=-1, keepdims=True)
    return (x * lax.rsqrt(ms + EPS) * gain) * (1.0 + scale) + shift


def _seg_rms(x, bd, gain):
    x2 = x * x
    hi = x2.astype(BF16)
    lo = (x2 - hi.astype(F32)).astype(BF16)
    ms = jnp.dot(hi, bd, preferred_element_type=F32) + jnp.dot(lo, bd, preferred_element_type=F32)
    return x * lax.rsqrt(ms + EPS) * gain


def _rope(x, cos, sin_signed):
    w = x.shape[-1]
    lane = lax.broadcasted_iota(jnp.int32, x.shape, 1)
    first = (lane % (2 * ROPE_PAIRS)) < ROPE_PAIRS
    partner = jnp.where(first, pltpu.roll(x, w - ROPE_PAIRS, axis=1), pltpu.roll(x, ROPE_PAIRS, axis=1))
    return x * cos + partner * sin_signed


def _inproj_kernel(x_ref, mod_ref, gain_ref, w_ref, bd_ref, qg_ref, kg_ref, cos_ref, sin_ref,
                   q_ref, kv_ref, qkvd_ref, z_ref, ab_ref):
    i = pl.program_id(0)
    is_lat = i >= N_CTX_TOK // TOK_TILE
    shift = mod_ref[:, 0:D_MODEL]
    scale = mod_ref[:, D_MODEL:2 * D_MODEL]
    h = _adaln(x_ref[...], gain_ref[...], shift, scale).astype(BF16)

    qa = jnp.dot(h, w_ref[:, 0:ATT_Q], preferred_element_type=F32)
    qn = _seg_rms(qa, bd_ref[...], qg_ref[...])
    ka = jnp.dot(h, w_ref[:, ATT_Q:ATT_Q + ATT_KV], preferred_element_type=F32)
    kn = _seg_rms(ka, bd_ref[0:ATT_KV, 0:ATT_KV], kg_ref[...])

    @pl.when(is_lat)
    def _():
        q_ref[...] = (_rope(qn, cos_ref[...], sin_ref[...]) * ATT_SCALE).astype(BF16)
        kv_ref[:, 0:ATT_KV] = _rope(kn, cos_ref[:, 0:ATT_KV], sin_ref[:, 0:ATT_KV])

    @pl.when(jnp.logical_not(is_lat))
    def _():
        q_ref[...] = (qn * ATT_SCALE).astype(BF16)
        kv_ref[:, 0:ATT_KV] = kn

    c0 = ATT_Q + ATT_KV
    kv_ref[:, ATT_KV:2 * ATT_KV] = jnp.dot(h, w_ref[:, c0:c0 + ATT_KV], preferred_element_type=F32)
    c0 += ATT_KV
    for j in range(DN_CONV_CH // 512):
        qkvd_ref[:, j * 512:(j + 1) * 512] = jnp.dot(
            h, w_ref[:, c0 + j * 512:c0 + (j + 1) * 512], preferred_element_type=F32).astype(BF16)
    c0 += DN_CONV_CH
    z_ref[...] = jnp.dot(h, w_ref[:, c0:c0 + DN_V], preferred_element_type=F32).astype(BF16)
    c0 += DN_V
    ab_ref[...] = jnp.dot(h, w_ref[:, c0:c0 + N_GATE_COLS], preferred_element_type=F32)


def _inproj_call(l, x, mod, gain, w_in, bd, qg, kg, cos, sin):
    n_ctx = N_CTX_TOK // TOK_TILE
    per_seq = DEC_SEQ // TOK_TILE

    def pos_map(i):
        return (jnp.maximum(i - n_ctx, 0) % per_seq, 0)

    tok = lambda w: pl.BlockSpec((TOK_TILE, w), lambda i: (i, 0))
    full = lambda a: pl.BlockSpec(a.shape, lambda i: (0,) * a.ndim)
    return pl.pallas_call(
        _inproj_kernel,
        grid=(N_TOK // TOK_TILE,),
        in_specs=[
            tok(D_MODEL),
            pl.BlockSpec((None, None, 1, 6 * D_MODEL), lambda i: (l, _mod_row(i), 0, 0)),
            pl.BlockSpec((None, 1, D_MODEL), lambda i: (l, 0, 0)),
            pl.BlockSpec((None, D_MODEL, IN_DIM), lambda i: (l, 0, 0)),
            full(bd),
            pl.BlockSpec((None, 1, ATT_Q), lambda i: (l, 0, 0)),
            pl.BlockSpec((None, 1, ATT_KV), lambda i: (l, 0, 0)),
            pl.BlockSpec((TOK_TILE, ATT_Q), pos_map),
            pl.BlockSpec((TOK_TILE, ATT_Q), pos_map),
        ],
        out_specs=[tok(ATT_Q), tok(2 * ATT_KV), tok(DN_CONV_CH), tok(DN_V), tok(N_GATE_COLS)],
        out_shape=[
            jax.ShapeDtypeStruct((N_TOK, ATT_Q), BF16),
            jax.ShapeDtypeStruct((N_TOK, 2 * ATT_KV), F32),
            jax.ShapeDtypeStruct((N_TOK, DN_CONV_CH), BF16),
            jax.ShapeDtypeStruct((N_TOK, DN_V), BF16),
            jax.ShapeDtypeStruct((N_TOK, N_GATE_COLS), F32),
        ],
        compiler_params=_cparams("arbitrary"),
        name="inproj",
    )(x, mod, gain, w_in, bd, qg, kg, cos, sin)


def _dup_half(x, g):
    lane = lax.broadcasted_iota(jnp.int32, x.shape, 1)
    lo = lane < HEAD_DIM
    xr = pltpu.roll(x, HEAD_DIM, axis=1)
    return jnp.where(lo, x, xr) if g == 0 else jnp.where(lo, xr, x)


def _attend(sink_ref, q_ref, o_ref, key_sets):
    rows = q_ref.shape[0]
    lane = lax.broadcasted_iota(jnp.int32, (rows, LANES), 1)
    lo = lane < HEAD_DIM
    for g in range(N_KV):
        ks = [(_dup_half(k, g).astype(BF16), _dup_half(v, g).astype(BF16), valid) for k, v, valid in key_sets]
        for jj in range(2):
            j = g * 2 + jj
            qp = q_ref[:, j * LANES:(j + 1) * LANES]
            outs = []
            for e in range(2):
                qm = jnp.where(lo if e == 0 else jnp.logical_not(lo), qp, jnp.zeros_like(qp))
                sk = sink_ref[2 * j + e]
                scores = []
                m = jnp.full((rows, 1), sk, F32)
                for k, _, valid in ks:
                    s = _dot_nt(qm, k)
                    if valid is not None:
                        s = jnp.where(valid, s, -jnp.inf)
                    scores.append(s)
                    m = jnp.maximum(m, jnp.max(s, axis=-1, keepdims=True))
                den = jnp.exp(sk - m)
                acc = jnp.zeros((rows, LANES), F32)
                for s, (_, v, _) in zip(scores, ks):
                    p = jnp.exp(s - m)
                    den = den + jnp.sum(p, axis=-1, keepdims=True)
                    acc = acc + _dot(p, v)
                outs.append(acc / den)
            o_ref[:, j * LANES:(j + 1) * LANES] = jnp.where(lo, outs[0], outs[1]).astype(o_ref.dtype)


def _attn_ctx_kernel(sink_ref, q_ref, kv_ref, o_ref):
    _attend(sink_ref, q_ref, o_ref, [(kv_ref[:, 0:ATT_KV], kv_ref[:, ATT_KV:2 * ATT_KV], None)])


def _attn_lat_kernel(sink_ref, q_ref, kv_ref, ck_ref, cv_ref, mix_in_ref, o_ref):
    del mix_in_ref
    n = pl.program_id(1)
    span = LAT_Q_ROWS + 2 * WINDOW
    start = pl.multiple_of(jnp.clip(n * LAT_Q_ROWS - WINDOW, 0, DEC_SEQ - span), BLOCK)
    kw = kv_ref[pl.ds(start, span), 0:ATT_KV]
    vw = kv_ref[pl.ds(start, span), ATT_KV:2 * ATT_KV]
    qpos = n * LAT_Q_ROWS + lax.broadcasted_iota(jnp.int32, (LAT_Q_ROWS, span), 0)
    kpos = start + lax.broadcasted_iota(jnp.int32, (LAT_Q_ROWS, span), 1)
    valid = jnp.abs(kpos - qpos) <= WINDOW
    _attend(sink_ref, q_ref, o_ref, [(kw, vw, valid), (ck_ref[...], cv_ref[...], None)])


def _attn_ctx_call(sink_l, q, kv):
    return pl.pallas_call(
        _attn_ctx_kernel,
        grid=(BATCH,),
        in_specs=[
            pl.BlockSpec(memory_space=pltpu.SMEM),
            pl.BlockSpec((SEQ, ATT_Q), lambda b: (b, 0)),
            pl.BlockSpec((SEQ, 2 * ATT_KV), lambda b: (b, 0)),
        ],
        out_specs=pl.BlockSpec((SEQ, ATT_Q), lambda b: (b, 0)),
        out_shape=jax.ShapeDtypeStruct((N_TOK, ATT_Q + DN_V), BF16),
        compiler_params=_cparams("arbitrary"),
        name="attn_ctx",
    )(sink_l, q, kv)


def _attn_lat_call(l, sink_l, q, kv, cache_k, cache_v, mix):
    nb = DEC_SEQ // LAT_Q_ROWS
    q0 = N_CTX_TOK // LAT_Q_ROWS
    s0 = N_CTX_TOK // DEC_SEQ
    return pl.pallas_call(
        _attn_lat_kernel,
        grid=(DEC_BATCH, nb),
        in_specs=[
            pl.BlockSpec(memory_space=pltpu.SMEM),
            pl.BlockSpec((LAT_Q_ROWS, ATT_Q), lambda b, n: (q0 + b * nb + n, 0)),
            pl.BlockSpec((DEC_SEQ, 2 * ATT_KV), lambda b, n: (s0 + b, 0)),
            pl.BlockSpec((None, None, PAST_LEN, ATT_KV), lambda b, n: (b, l, 0, 0)),
            pl.BlockSpec((None, None, PAST_LEN, ATT_KV), lambda b, n: (b, l, 0, 0)),
            pl.BlockSpec(memory_space=pl.ANY),
        ],
        out_specs=pl.BlockSpec((LAT_Q_ROWS, ATT_Q), lambda b, n: (q0 + b * nb + n, 0)),
        out_shape=jax.ShapeDtypeStruct((N_TOK, ATT_Q + DN_V), BF16),
        input_output_aliases={5: 0},
        compiler_params=_cparams("arbitrary", "arbitrary"),
        name="attn_lat",
    )(sink_l, q, kv, cache_k, cache_v, mix)


INV_PASSES = 1
GDN_UNROLL = 4
GDN_CTX_SEQS = 4


def _split_bf16(x):
    hi = x.astype(BF16)
    return hi, (x - hi.astype(F32)).astype(BF16)


def _block_diag(x, width):
    n = x.shape[1] // width
    blk = lax.broadcasted_iota(jnp.int32, x.shape, 1) // width
    zero = jnp.zeros_like(x)
    return jnp.concatenate([jnp.where(blk == h, x, zero) for h in range(n)], axis=0)


def _mm_heads(ts, xs, width):
    if INV_PASSES == 1:
        bds = [_block_diag(x.astype(BF16), width) for x in xs]
        return [jnp.dot(t.astype(BF16), bd, preferred_element_type=F32) for t, bd in zip(ts, bds)]
    tsp = [_split_bf16(t) for t in ts]
    xsp = [_split_bf16(x) for x in xs]
    bd_hi = [_block_diag(x_hi, width) for x_hi, _ in xsp]
    bd_lo = [_block_diag(x_lo, width) for _, x_lo in xsp]
    return [jnp.dot(t_hi, bh, preferred_element_type=F32) + jnp.dot(t_lo, bh, preferred_element_type=F32)
            + jnp.dot(t_hi, bl, preferred_element_type=F32) for (t_hi, t_lo), bh, bl in zip(tsp, bd_hi, bd_lo)]


def _unit_tri_inverse(mats, row, col):
    eye = (row == col).astype(F32)
    a8 = [jnp.where((row // 8) == (col // 8), a, 0.0) for a in mats]
    a8_2 = _mm_heads(a8, a8, CHUNK)
    a8_4 = _mm_heads(a8_2, a8_2, CHUNK)
    ts = [eye - a for a in a8]
    ts = [t + p for t, p in zip(ts, _mm_heads(ts, a8_2, CHUNK))]
    ts = [t + p for t, p in zip(ts, _mm_heads(ts, a8_4, CHUNK))]
    b = 8
    while b < CHUNK:
        level = ((row // (2 * b)) == (col // (2 * b))) & ((row // b) != (col // b))
        off = [jnp.where(level, a, 0.0) for a in mats]
        ts = [t - p for t, p in zip(ts, _mm_heads(_mm_heads(ts, off, CHUNK), ts, CHUNK))]
        b *= 2
    return ts


def _bcast_cols(x, first, width):
    rows = x.shape[0]
    if width == LANES:
        return jnp.concatenate([jnp.broadcast_to(x[:, first + h:first + h + 1], (rows, width)) for h in range(DN_HEADS)], axis=1)
    blk = lax.broadcasted_iota(jnp.int32, (rows, DN_HEADS * width), 1) // width
    out = jnp.broadcast_to(x[:, first:first + 1], (rows, DN_HEADS * width))
    for h in range(1, DN_HEADS):
        out = jnp.where(blk == h, jnp.broadcast_to(x[:, first + h:first + h + 1], (rows, DN_HEADS * width)), out)
    return out


def _gdn_kernel(*refs, seq_len, n_seq, has_s0, write_state, n_alias):
    qkvd_ref, z_ref, ab_ref, convw_ref, alog_ref, dtb_ref, ogain_ref = refs[:7]
    pos = 7
    s0_ref = None
    if has_s0:
        s0_ref = refs[pos]
        pos += 1
    pos += n_alias
    dn_ref = refs[pos]
    pos += 1
    sfin_ref = None
    if write_state:
        sfin_ref = refs[pos]
        pos += 1
    qn_s, kn_s, vn_s, gb_s, u0_s, w_s, qd_s, kd_s, qk_s, gt_s, o_s, st_s, xpad_s = refs[pos:]
    L = seq_len
    n_chunks = L // CHUNK
    n_pairs = DN_HEADS // 2
    pair_w = 2 * DK
    pad = (DN_CONV - 1) // 2

    halo = 8

    def finish(y, kind):
        y = _silu(y)
        if kind == 0:
            return y * lax.rsqrt(jnp.sum(y * y, axis=-1, keepdims=True) + EPS) * (DK ** -0.5)
        if kind == 1:
            return y * lax.rsqrt(jnp.sum(y * y, axis=-1, keepdims=True) + EPS)
        return y

    xpad_s[0:halo, :] = jnp.zeros((halo, LANES), F32)
    xpad_s[halo + L:2 * halo + L, :] = jnp.zeros((halo, LANES), F32)
    for kind, dst in enumerate((qn_s, kn_s, vn_s)):
        def conv_head(head, carry, kind=kind, dst=dst):
            cs = pl.ds(pl.multiple_of((kind * DN_HEADS + head) * LANES, LANES), LANES)
            hs = pl.ds(pl.multiple_of(head * DK, DK), DK)
            for s in range(n_seq):
                r0 = s * L
                for rb in range(0, L, SEQ):
                    xpad_s[halo + rb:halo + rb + SEQ, :] = qkvd_ref[r0 + rb:r0 + rb + SEQ, cs].astype(F32)
                for rb in range(0, L, SEQ):
                    acc = None
                    for i in range(DN_CONV):
                        lo = halo + rb + i - pad
                        term = xpad_s[lo:lo + SEQ, :] * convw_ref[i:i + 1, cs]
                        acc = term if acc is None else acc + term
                    dst[r0 + rb:r0 + rb + SEQ, hs] = finish(acc, kind)
            return carry

        lax.fori_loop(0, DN_HEADS, conv_head, 0)

    ab = ab_ref[...]
    glane = lax.broadcasted_iota(jnp.int32, ab.shape, 1)
    g = -jnp.exp(alog_ref[...]) * _softplus(ab + dtb_ref[...])
    gb_s[...] = jnp.where(glane < N_DIR * DN_HEADS, g, _sigmoid(ab))

    for s in range(n_seq):
        for d in range(N_DIR):
            for p in range(n_pairs):
                idx = (s * N_DIR + d) * n_pairs + p
                st_s[idx] = jnp.zeros((pair_w, pair_w), F32)
                if has_s0:
                    st_s[idx, 0:DK, 0:DV] = s0_ref[s, d, 2 * p]
                    st_s[idx, DK:2 * DK, DV:2 * DV] = s0_ref[s, d, 2 * p + 1]

    row = lax.broadcasted_iota(jnp.int32, (CHUNK, DN_HEADS * CHUNK), 0)
    col = lax.broadcasted_iota(jnp.int32, (CHUNK, DN_HEADS * CHUNK), 1) % CHUNK
    before_incl = [col <= row, col >= row]
    before_strict = [col < row, col > row]
    after_strict01 = [m.astype(F32) for m in before_strict]
    r64 = lax.broadcasted_iota(jnp.int32, (CHUNK, CHUNK), 0)
    c64 = lax.broadcasted_iota(jnp.int32, (CHUNK, CHUNK), 1)
    tri01 = [(c64 <= r64).astype(BF16), (c64 >= r64).astype(BF16)]

    def prep_step(n, carry):
        a_mats, v_betas, kb_egcs, slots = [], [], [], []
        for j in range(GDN_UNROLL):
            c = n * GDN_UNROLL + j
            rs = pl.ds(pl.multiple_of(c * CHUNK, CHUNK), CHUNK)
            gbc = gb_s[rs, :]
            q = qn_s[rs, :]
            k = kn_s[rs, :]
            v = vn_s[rs, :]
            k_bd = _block_diag(k.astype(BF16), DK)
            betas = [_bcast_cols(gbc, N_DIR * DN_HEADS + d * DN_HEADS, DK) for d in range(N_DIR)]
            kbs = [k * beta for beta in betas]
            kk_qk = _dot_nt(jnp.concatenate(kbs + [q], axis=0), k_bd)
            qk_raw = kk_qk[N_DIR * CHUNK:(N_DIR + 1) * CHUNK]
            for d in range(N_DIR):
                first = d * DN_HEADS
                gc_all = _dot_split(tri01[d], gbc)
                last = CHUNK - 1 if d == 0 else 0
                gc_last = gc_all[last:last + 1, :]
                gt_s[d, pl.ds(c, 1), :] = jnp.exp(gc_last)
                e_gc = _bcast_cols(jnp.exp(gc_all), first, DK)
                e_kd = _bcast_cols(jnp.exp(gc_last - gc_all), first, DK)
                gdiff = _dot_split(tri01[d], _bcast_cols(gbc, first, CHUNK) * after_strict01[d])
                decay = jnp.where(before_incl[d], jnp.exp(gdiff), 0.0)
                a_mats.append(jnp.where(before_strict[d], kk_qk[d * CHUNK:(d + 1) * CHUNK] * decay, 0.0))
                qk_s[d, rs, :] = jnp.where(before_incl[d], qk_raw * decay, 0.0).astype(BF16)
                qd_s[d, rs, :] = (q * e_gc).astype(BF16)
                kd_s[d, rs, :] = (k * e_kd).astype(BF16)
                v_betas.append(v * betas[d])
                kb_egcs.append(kbs[d] * e_gc)
                slots.append((d, rs))
        t_invs = _unit_tri_inverse(a_mats, row, col)
        for (d, rs), u0, w in zip(slots, _mm_heads(t_invs, v_betas, DV), _mm_heads(t_invs, kb_egcs, DK)):
            u0_s[d, rs, :] = u0
            w_s[d, rs, :] = w.astype(BF16)
        return carry

    lax.fori_loop(0, n_seq * n_chunks // GDN_UNROLL, prep_step, 0)

    plane = lax.broadcasted_iota(jnp.int32, (1, pair_w), 1)
    srow = lax.broadcasted_iota(jnp.int32, (pair_w, pair_w), 0) // DK
    scol = lax.broadcasted_iota(jnp.int32, (pair_w, pair_w), 1) // DV
    same_head = srow == scol

    def scan_step(n, carry):
        probs = []
        for s in range(n_seq):
            for d in range(N_DIR):
                c = s * n_chunks + (n if d == 0 else n_chunks - 1 - n)
                rs = pl.ds(pl.multiple_of(c * CHUNK, CHUNK), CHUNK)
                gt = gt_s[d, pl.ds(c, 1), :]
                for p in range(n_pairs):
                    c0 = d * DN_HEADS + 2 * p
                    g_tot = jnp.where(plane < DV, gt[:, c0:c0 + 1], gt[:, c0 + 1:c0 + 2])
                    probs.append((d, rs, p, (s * N_DIR + d) * n_pairs + p, slice(p * pair_w, (p + 1) * pair_w), g_tot))
        s_prev = [st_s[idx] for _, _, _, idx, _, _ in probs]
        s_b = [s.astype(BF16) for s in s_prev]
        ws = [jnp.dot(w_s[d, rs, ps], sb, preferred_element_type=F32) for (d, rs, _, _, ps, _), sb in zip(probs, s_b)]
        u_b = [(u0_s[d, rs, ps] - w).astype(BF16) for (d, rs, _, _, ps, _), w in zip(probs, ws)]
        upd = [lax.dot_general(kd_s[d, rs, ps], u, (((0,), (0,)), ((), ())), preferred_element_type=F32)
               for (d, rs, _, _, ps, _), u in zip(probs, u_b)]
        for (_, _, _, idx, _, g_tot), s, up in zip(probs, s_prev, upd):
            st_s[idx] = s * g_tot + jnp.where(same_head, up, 0.0)
        for (d, rs, p, _, ps, _), sb, u in zip(probs, s_b, u_b):
            o_s[d, rs, ps] = (jnp.dot(qd_s[d, rs, ps], sb, preferred_element_type=F32)
                              + jnp.dot(qk_s[d, rs, p * 2 * CHUNK:(p + 1) * 2 * CHUNK], _block_diag(u, DV),
                                        preferred_element_type=F32))
        return carry

    lax.fori_loop(0, n_chunks, scan_step, 0)

    for h in range(DN_HEADS):
        hs = slice(h * DV, (h + 1) * DV)
        o = o_s[0, :, hs] + o_s[1, :, hs]
        y = o * lax.rsqrt(jnp.mean(o * o, axis=-1, keepdims=True) + EPS) * ogain_ref[...]
        dn_ref[:, hs] = (y * _silu(z_ref[:, hs].astype(F32))).astype(dn_ref.dtype)
    if write_state:
        for s in range(n_seq):
            for d in range(N_DIR):
                for p in range(n_pairs):
                    idx = (s * N_DIR + d) * n_pairs + p
                    sfin_ref[s, d, 2 * p] = st_s[idx, 0:DK, 0:DV]
                    sfin_ref[s, d, 2 * p + 1] = st_s[idx, DK:2 * DK, DV:2 * DV]


def _gdn_call(l, qkvd, z, ab, conv_w, alog16, dtb16, o_gain, state_dn, mix, new_state=None, *, latent):
    L = DEC_SEQ if latent else SEQ
    n_seq = 1 if latent else GDN_CTX_SEQS
    n_steps = (DEC_BATCH if latent else BATCH) // n_seq
    rows = n_seq * L
    base = N_CTX_TOK // rows if latent else 0
    seq = lambda w: pl.BlockSpec((rows, w), lambda b: (base + b, 0))
    in_specs = [
        seq(DN_CONV_CH), seq(DN_V), seq(N_GATE_COLS),
        pl.BlockSpec((None, DN_CONV, DN_CONV_CH), lambda b: (l, 0, 0)),
        pl.BlockSpec((None, 1, N_GATE_COLS), lambda b: (l, 0, 0)),
        pl.BlockSpec((None, 1, N_GATE_COLS), lambda b: (l, 0, 0)),
        pl.BlockSpec((None, 1, DV), lambda b: (l, 0, 0)),
    ]
    args = [qkvd, z, ab, conv_w, alog16, dtb16, o_gain]
    out_specs = [pl.BlockSpec((rows, DN_V), lambda b: (base + b, 1))]
    out_shape = [jax.ShapeDtypeStruct((N_TOK, ATT_Q + DN_V), BF16)]
    if latent:
        in_specs.append(pl.BlockSpec((n_seq, None, N_DIR, DN_HEADS, DK, DV), lambda b: (b, l, 0, 0, 0, 0)))
        args.append(state_dn)
    in_specs.append(pl.BlockSpec(memory_space=pl.ANY))
    args.append(mix)
    aliases = {len(args) - 1: 0}
    if not latent:
        out_specs.append(pl.BlockSpec((n_seq, None, N_DIR, DN_HEADS, DK, DV), lambda b: (b, l, 0, 0, 0, 0)))
        out_shape.append(jax.ShapeDtypeStruct((BATCH, DEPTH, N_DIR, DN_HEADS, DK, DV), F32))
        if new_state is not None:
            in_specs.append(pl.BlockSpec(memory_space=pl.ANY))
            args.append(new_state)
            aliases[len(args) - 1] = 1
    n_chunks = rows // CHUNK
    return pl.pallas_call(
        functools.partial(_gdn_kernel, seq_len=L, n_seq=n_seq, has_s0=latent, write_state=not latent,
                          n_alias=len(aliases)),
        grid=(n_steps,),
        in_specs=in_specs,
        out_specs=out_specs,
        out_shape=out_shape,
        scratch_shapes=[
            pltpu.VMEM((rows, DN_QK), F32), pltpu.VMEM((rows, DN_QK), F32), pltpu.VMEM((rows, DN_V), F32),
            pltpu.VMEM((rows, N_GATE_COLS), F32),
            pltpu.VMEM((N_DIR, rows, DN_V), F32),
            pltpu.VMEM((N_DIR, rows, DN_QK), BF16),
            pltpu.VMEM((N_DIR, rows, DN_QK), BF16),
            pltpu.VMEM((N_DIR, rows, DN_QK), BF16),
            pltpu.VMEM((N_DIR, rows, DN_HEADS * CHUNK), BF16),
            pltpu.VMEM((N_DIR, n_chunks, N_GATE_COLS), F32),
            pltpu.VMEM((N_DIR, rows, DN_V), F32),
            pltpu.VMEM((n_seq * N_DIR * (DN_HEADS // 2), 2 * DK, 2 * DV), F32),
            pltpu.VMEM((L + 16, LANES), F32),
        ],
        input_output_aliases=aliases,
        compiler_params=_cparams("arbitrary"),
        name="gdn_lat" if latent else "gdn_ctx",
    )(*args)


def _top2_of4(vals):
    m1 = jnp.maximum(jnp.maximum(vals[0], vals[1]), jnp.maximum(vals[2], vals[3]))
    i1 = jnp.where(vals[0] == m1, 0, jnp.where(vals[1] == m1, 1, jnp.where(vals[2] == m1, 2, 3)))
    rest = [jnp.where(i1 == j, -jnp.inf, vals[j]) for j in range(4)]
    m2 = jnp.maximum(jnp.maximum(rest[0], rest[1]), jnp.maximum(rest[2], rest[3]))
    i2 = jnp.where(rest[0] == m2, 0, jnp.where(rest[1] == m2, 1, jnp.where(rest[2] == m2, 2, 3)))
    return m1, i1, m2, i2


def _outproj_kernel(x_ref, mix_ref, mod_ref, gain_ref, wo_ref, rwt_ref, rb_ref, ut_ref,
                    x1_ref, h2_ref, ri_ref, rg_ref, cnt_ref, carry_s):
    i = pl.program_id(0)

    @pl.when(i == 0)
    def _():
        carry_s[...] = jnp.zeros_like(carry_s)

    gate1 = mod_ref[:, 2 * D_MODEL:3 * D_MODEL]
    shift2 = mod_ref[:, 3 * D_MODEL:4 * D_MODEL]
    scale2 = mod_ref[:, 4 * D_MODEL:5 * D_MODEL]
    x1 = x_ref[...] + gate1 * jnp.dot(mix_ref[...], wo_ref[...], preferred_element_type=F32)
    x1_ref[...] = x1
    h2 = _adaln(x1, gain_ref[...], shift2, scale2)
    h2_ref[...] = h2

    h_hi, h_lo = _split_bf16(h2)
    r_hi, r_lo = _split_bf16(rwt_ref[...])
    nt = (((1,), (1,)), ((), ()))
    logits = (lax.dot_general(r_hi, h_hi, nt, preferred_element_type=F32)
              + lax.dot_general(r_hi, h_lo, nt, preferred_element_type=F32)
              + lax.dot_general(r_lo, h_hi, nt, preferred_element_type=F32))
    scores = _sigmoid(logits)
    biased = scores + rb_ref[...]
    rows = [biased[e:e + 1, :] for e in range(N_EXPERTS)]
    per_group = [_top2_of4(rows[g * 4:(g + 1) * 4]) for g in range(N_GROUPS)]
    gs = [pg[0] + pg[2] for pg in per_group]
    _, gsel, _, _ = _top2_of4(gs)
    pick = lambda k: jnp.where(gsel == 0, per_group[0][k], jnp.where(gsel == 1, per_group[1][k],
                               jnp.where(gsel == 2, per_group[2][k], per_group[3][k])))
    e1 = gsel * EXPERTS_PER_GROUP + pick(1)
    e2 = gsel * EXPERTS_PER_GROUP + pick(3)
    erow = lax.broadcasted_iota(jnp.int32, scores.shape, 0)
    oh1 = erow == e1
    oh2 = erow == e2
    s1 = jnp.sum(jnp.where(oh1, scores, 0.0), axis=0, keepdims=True)
    s2 = jnp.sum(jnp.where(oh2, scores, 0.0), axis=0, keepdims=True)
    tot = s1 + s2
    grow = lax.broadcasted_iota(jnp.int32, rg_ref.shape, 0)
    rg_ref[...] = jnp.where(grow == 0, s1 / tot, s2 / tot)

    sel = (oh1 | oh2).astype(BF16)
    before = jnp.dot(sel, ut_ref[...], preferred_element_type=F32) + carry_s[...]
    r1 = jnp.sum(jnp.where(oh1, before, 0.0), axis=0, keepdims=True)
    r2 = jnp.sum(jnp.where(oh2, before, 0.0), axis=0, keepdims=True)
    carry_s[...] = carry_s[...] + jnp.sum(sel.astype(F32), axis=1, keepdims=True)
    cnt_ref[...] = carry_s[...].astype(jnp.int32)
    irow = lax.broadcasted_iota(jnp.int32, ri_ref.shape, 0)
    ri_ref[...] = jnp.where(irow == 0, e1, jnp.where(irow == 1, e2, jnp.where(
        irow == 2, r1.astype(jnp.int32), r2.astype(jnp.int32))))


def _outproj_call(l, x, mix, mod, gain, w_out, router_wt, router_b, ut):
    tok = lambda w: pl.BlockSpec((TOK_TILE, w), lambda i: (i, 0))
    per_tok = lambda r: pl.BlockSpec((r, TOK_TILE), lambda i: (0, i))
    full = lambda a: pl.BlockSpec(a.shape, lambda i: (0,) * a.ndim)
    return pl.pallas_call(
        _outproj_kernel,
        grid=(N_TOK // TOK_TILE,),
        in_specs=[
            tok(D_MODEL), tok(ATT_Q + DN_V),
            pl.BlockSpec((None, None, 1, 6 * D_MODEL), lambda i: (l, _mod_row(i), 0, 0)),
            pl.BlockSpec((None, 1, D_MODEL), lambda i: (l, 0, 0)),
            pl.BlockSpec((None, ATT_Q + DN_V, D_MODEL), lambda i: (l, 0, 0)),
            full(router_wt), full(router_b), full(ut),
        ],
        out_specs=[tok(D_MODEL), tok(D_MODEL), per_tok(4), per_tok(2), pl.BlockSpec((N_EXPERTS, 1), lambda i: (0, 0))],
        out_shape=[
            jax.ShapeDtypeStruct((N_TOK, D_MODEL), F32),
            jax.ShapeDtypeStruct((N_TOK, D_MODEL), F32),
            jax.ShapeDtypeStruct((4, N_TOK), jnp.int32),
            jax.ShapeDtypeStruct((2, N_TOK), F32),
            jax.ShapeDtypeStruct((N_EXPERTS, 1), jnp.int32),
        ],
        scratch_shapes=[pltpu.VMEM((N_EXPERTS, 1), F32)],
        compiler_params=_cparams("arbitrary"),
        name="outproj_router",
    )(x, mix, mod, gain, w_out, router_wt, router_b, ut)


def _dispatch_kernel(pos_ref, h2_ref, buf_in_ref, sorted_ref, sem):
    del buf_in_ref

    def row_copy(r, k):
        return pltpu.make_async_copy(h2_ref.at[pl.ds(r, 1), :], sorted_ref.at[pl.ds(pos_ref[0, k * TOK_TILE + r], 1), :], sem)

    def start(r, c):
        row_copy(r, 0).start()
        row_copy(r, 1).start()
        return c

    def wait(r, c):
        row_copy(r, 0).wait()
        row_copy(r, 1).wait()
        return c

    lax.fori_loop(0, TOK_TILE, start, 0, unroll=ROW_DMA_UNROLL)
    lax.fori_loop(0, TOK_TILE, wait, 0, unroll=ROW_DMA_UNROLL)


def _dispatch_call(pos_tiles, h2, zeros_sorted):
    return pl.pallas_call(
        _dispatch_kernel,
        grid=(N_TOK // TOK_TILE,),
        in_specs=[
            pl.BlockSpec((None, 1, TOP_K * TOK_TILE), lambda i: (i, 0, 0), memory_space=pltpu.SMEM),
            pl.BlockSpec((TOK_TILE, D_MODEL), lambda i: (i, 0)),
            pl.BlockSpec(memory_space=pl.ANY),
        ],
        out_specs=pl.BlockSpec(memory_space=pl.ANY),
        out_shape=jax.ShapeDtypeStruct((N_SORTED, D_MODEL), F32),
        scratch_shapes=[pltpu.SemaphoreType.DMA(())],
        input_output_aliases={2: 0},
        compiler_params=_cparams("arbitrary"),
        name="moe_dispatch",
    )(pos_tiles, h2, zeros_sorted)


def _experts_kernel(te_ref, nu_ref, x_ref, wg_ref, wu_ref, wd_ref, y_ref, wg_s, wu_s, wd_s):
    j = pl.program_id(0)
    used = j < nu_ref[0]

    @pl.when((j == 0) | (te_ref[j] != te_ref[jnp.maximum(j - 1, 0)]))
    def _():
        wg_s[...] = wg_ref[...].astype(BF16)
        wu_s[...] = wu_ref[...].astype(BF16)
        wd_s[...] = wd_ref[...].astype(BF16)

    @pl.when(used)
    def _():
        x = x_ref[...].astype(BF16)
        hid = _silu(jnp.dot(x, wg_s[...], preferred_element_type=F32)) * jnp.dot(x, wu_s[...], preferred_element_type=F32)
        y_ref[...] = jnp.dot(hid.astype(BF16), wd_s[...], preferred_element_type=F32)

    @pl.when(jnp.logical_not(used))
    def _():
        y_ref[...] = jnp.zeros_like(y_ref)


def _experts_call(l, tile_expert, n_used, xs, w_gate, w_up, w_down):
    row = lambda j, te, nu: (jnp.maximum(jnp.minimum(j, nu[0] - 1), 0), 0)
    grid_spec = pltpu.PrefetchScalarGridSpec(
        num_scalar_prefetch=2,
        grid=(N_EXP_TILES,),
        in_specs=[
            pl.BlockSpec((EXP_TILE, D_MODEL), row),
            pl.BlockSpec((None, None, D_MODEL, D_FF), lambda j, te, nu: (l, te[j], 0, 0)),
            pl.BlockSpec((None, None, D_MODEL, D_FF), lambda j, te, nu: (l, te[j], 0, 0)),
            pl.BlockSpec((None, None, D_FF, D_MODEL), lambda j, te, nu: (l, te[j], 0, 0)),
        ],
        out_specs=pl.BlockSpec((EXP_TILE, D_MODEL), lambda j, te, nu: (j, 0)),
        scratch_shapes=[pltpu.VMEM((D_MODEL, D_FF), BF16), pltpu.VMEM((D_MODEL, D_FF), BF16),
                        pltpu.VMEM((D_FF, D_MODEL), BF16)],
    )
    return pl.pallas_call(
        _experts_kernel,
        grid_spec=grid_spec,
        out_shape=jax.ShapeDtypeStruct((N_SORTED, D_MODEL), F32),
        compiler_params=_cparams("arbitrary"),
        name="moe_experts",
    )(tile_expert, n_used, xs, w_gate, w_up, w_down)


def _combine_kernel(pos_ref, x1_ref, rg_ref, mod_ref, y_ref, out_ref, buf_s, sem):
    def row_copy(r, k):
        return pltpu.make_async_copy(y_ref.at[pl.ds(pos_ref[0, k * TOK_TILE + r], 1), :], buf_s.at[k, pl.ds(r, 1), :], sem)

    def start(r, c):
        row_copy(r, 0).start()
        row_copy(r, 1).start()
        return c

    def wait(r, c):
        row_copy(r, 0).wait()
        row_copy(r, 1).wait()
        return c

    lax.fori_loop(0, TOK_TILE, start, 0, unroll=ROW_DMA_UNROLL)
    lax.fori_loop(0, TOK_TILE, wait, 0, unroll=ROW_DMA_UNROLL)
    gate2 = mod_ref[:, 5 * D_MODEL:6 * D_MODEL]
    moe = buf_s[0] * rg_ref[:, 0:1] + buf_s[1] * rg_ref[:, 1:2]
    out_ref[...] = x1_ref[...] + gate2 * moe


def _combine_call(l, pos_tiles, x1, rg, mod, y_sorted):
    tok = lambda w: pl.BlockSpec((TOK_TILE, w), lambda i: (i, 0))
    return pl.pallas_call(
        _combine_kernel,
        grid=(N_TOK // TOK_TILE,),
        in_specs=[
            pl.BlockSpec((None, 1, TOP_K * TOK_TILE), lambda i: (i, 0, 0), memory_space=pltpu.SMEM),
            tok(D_MODEL), tok(2),
            pl.BlockSpec((None, None, 1, 6 * D_MODEL), lambda i: (l, _mod_row(i), 0, 0)),
            pl.BlockSpec(memory_space=pl.ANY),
        ],
        out_specs=tok(D_MODEL),
        out_shape=jax.ShapeDtypeStruct((N_TOK, D_MODEL), F32),
        scratch_shapes=[pltpu.VMEM((TOP_K, TOK_TILE, D_MODEL), F32), pltpu.SemaphoreType.DMA(())],
        compiler_params=_cparams("arbitrary"),
        name="moe_combine",
    )(pos_tiles, x1, rg, mod, y_sorted)


def _rope_tables():
    pos = jnp.arange(DEC_SEQ)
    r = (pos // GRID_W).astype(F32)
    c = (pos % GRID_W).astype(F32)
    inv = ROPE_BASE ** (-jnp.arange(ROPE_PAIRS, dtype=F32) / ROPE_PAIRS)
    ar, ac = r[:, None] * inv, c[:, None] * inv
    cos = jnp.concatenate([jnp.cos(ar), jnp.cos(ar), jnp.cos(ac), jnp.cos(ac)], axis=-1)
    sin = jnp.concatenate([-jnp.sin(ar), jnp.sin(ar), -jnp.sin(ac), jnp.sin(ac)], axis=-1)
    return jnp.tile(cos, (1, N_HEADS)), jnp.tile(sin, (1, N_HEADS))


def _routing_layout(ri, counts):
    counts = counts.reshape(N_EXPERTS)
    padded = ((counts + EXP_TILE - 1) // EXP_TILE) * EXP_TILE
    ends = jnp.cumsum(padded)
    offs = ends - padded
    eids = jnp.arange(N_EXPERTS, dtype=jnp.int32)[:, None, None]
    pos = jnp.sum(jnp.where(ri[None, 0:2] == eids, offs[:, None, None], 0), axis=0) + ri[2:4]
    n_used = (ends[-1] // EXP_TILE).astype(jnp.int32)
    tile_start = jnp.arange(N_EXP_TILES, dtype=jnp.int32) * EXP_TILE
    tile_expert = jnp.sum((tile_start[:, None] >= ends[None, :]).astype(jnp.int32), axis=1)
    last = jnp.sum((jnp.maximum(ends[-1] - EXP_TILE, 0) >= ends).astype(jnp.int32))
    tile_expert = jnp.minimum(jnp.where(tile_start < ends[-1], tile_expert, last), N_EXPERTS - 1).astype(jnp.int32)
    pos_tiles = pos.reshape(TOP_K, N_TOK // TOK_TILE, TOK_TILE).transpose(1, 0, 2).reshape(
        N_TOK // TOK_TILE, 1, TOP_K * TOK_TILE).astype(jnp.int32)
    return pos_tiles, tile_expert, n_used.reshape(1)


def kernel(x_prompt, x_sample, cache_k, cache_v, state_dn, c, c_ctx, w_ada, b_ada, norm_attn, norm_ffn,
           w_in, conv_w, a_log, dt_bias, q_norm, k_norm, sink, o_norm, w_out, router_w, router_bias,
           w_gate, w_up, w_down):
    x = jnp.concatenate([x_prompt.reshape(N_CTX_TOK, D_MODEL), x_sample.reshape(N_LAT_TOK, D_MODEL)], axis=0)
    cond = jnp.concatenate([c_ctx[None, :], c, jnp.zeros((N_COND - 1 - DEC_BATCH, D_MODEL), F32)], axis=0)
    mod = _modulation_call(cond, w_ada, b_ada).reshape(DEPTH, N_COND, 1, 6 * D_MODEL)

    w_in_b = w_in.astype(BF16)
    w_out_b = w_out.astype(BF16)
    seg = jnp.arange(ATT_Q) // HEAD_DIM
    bd = jnp.where(seg[:, None] == seg[None, :], 1.0 / HEAD_DIM, 0.0).astype(BF16)
    qg = jnp.tile(q_norm, (1, N_HEADS)).reshape(DEPTH, 1, ATT_Q)
    kg = jnp.tile(k_norm, (1, N_KV)).reshape(DEPTH, 1, ATT_KV)
    cos, sin = _rope_tables()
    gain1 = norm_attn.reshape(DEPTH, 1, D_MODEL)
    gain2 = norm_ffn.reshape(DEPTH, 1, D_MODEL)
    pad8 = lambda a: jnp.pad(a.reshape(DEPTH, 1, N_DIR * DN_HEADS), ((0, 0), (0, 0), (0, N_GATE_COLS - N_DIR * DN_HEADS)))
    alog16, dtb16 = pad8(a_log), pad8(dt_bias)
    o_gain = o_norm.reshape(DEPTH, 1, DV)
    ck = cache_k.reshape(DEC_BATCH, DEPTH, PAST_LEN, ATT_KV)
    cv = cache_v.reshape(DEC_BATCH, DEPTH, PAST_LEN, ATT_KV)
    router_wt = router_w.T
    rb = router_bias.reshape(N_EXPERTS, 1)
    tri = jnp.arange(TOK_TILE)
    ut = (tri[:, None] < tri[None, :]).astype(BF16)
    sorted_buf = jnp.zeros((N_SORTED, D_MODEL), F32)

    k_list, v_list, new_state = [], [], None
    for l in range(DEPTH):
        q, kv, qkvd, z, ab = _inproj_call(l, x, mod, gain1, w_in_b, bd, qg, kg, cos, sin)
        k_list.append(kv[:N_CTX_TOK, 0:ATT_KV].reshape(BATCH, SEQ, N_KV, HEAD_DIM))
        v_list.append(kv[:N_CTX_TOK, ATT_KV:].reshape(BATCH, SEQ, N_KV, HEAD_DIM))
        mix = _attn_ctx_call(sink[l], q, kv)
        mix = _attn_lat_call(l, sink[l], q, kv, ck, cv, mix)
        mix, new_state = _gdn_call(l, qkvd, z, ab, conv_w, alog16, dtb16, o_gain, state_dn, mix, new_state, latent=False)
        (mix,) = _gdn_call(l, qkvd, z, ab, conv_w, alog16, dtb16, o_gain, state_dn, mix, latent=True)
        x1, h2, ri, rg, counts = _outproj_call(l, x, mix, mod, gain2, w_out_b, router_wt, rb, ut)
        pos_tiles, tile_expert, n_used = _routing_layout(ri, counts)
        xs_sorted = _dispatch_call(pos_tiles, h2, sorted_buf)
        sorted_buf = xs_sorted
        y_sorted = _experts_call(l, tile_expert, n_used, xs_sorted, w_gate, w_up, w_down)
        x = _combine_call(l, pos_tiles, x1, rg.T, mod, y_sorted)

    y_prompt = x[:N_CTX_TOK].reshape(BATCH, SEQ, D_MODEL)
    y_sample = x[N_CTX_TOK:].reshape(DEC_BATCH, DEC_SEQ, D_MODEL)
    return (y_prompt, y_sample, jnp.stack(k_list, axis=1), jnp.stack(v_list, axis=1), new_state)
```

```python
import functools

import jax
import jax.numpy as jnp
from jax import lax
from jax.experimental import pallas as pl
from jax.experimental.pallas import tpu as pltpu

D_MODEL = 1024
BATCH = 32
SEQ = 256
DEPTH = 4
DEC_BATCH = 4
DEC_SEQ = 1024
PAST_LEN = 256
GRID_W = 64
N_HEADS = 8
N_KV = 2
HEAD_DIM = 64
WINDOW = 128
BLOCK = 128
ATT_SCALE = HEAD_DIM ** -0.5
ROPE_BASE = 10000.0
ROPE_PAIRS = HEAD_DIM // 4
DN_HEADS = 4
DK = 128
DV = 128
DN_CONV = 5
CHUNK = 64
N_DIR = 2
ATT_Q = N_HEADS * HEAD_DIM
ATT_KV = N_KV * HEAD_DIM
DN_QK = DN_HEADS * DK
DN_V = DN_HEADS * DV
DN_CONV_CH = 2 * DN_QK + DN_V
N_GATE_COLS = 2 * N_DIR * DN_HEADS
IN_DIM = ATT_Q + 2 * ATT_KV + DN_CONV_CH + DN_V + N_GATE_COLS
N_EXPERTS = 16
N_GROUPS = 4
EXPERTS_PER_GROUP = 4
TOP_K = 2
D_FF = 512
EPS = 1e-6

N_CTX_TOK = BATCH * SEQ
N_LAT_TOK = DEC_BATCH * DEC_SEQ
N_TOK = N_CTX_TOK + N_LAT_TOK
N_COND = 8

LANES = 128
SUBLANES = 8
TOK_TILE = 512
EXP_TILE = 512
N_EXP_TILES = (N_TOK * TOP_K) // EXP_TILE + N_EXPERTS
N_SORTED = N_EXP_TILES * EXP_TILE
VMEM_LIMIT = 56 * 1024 * 1024
LAT_Q_ROWS = 256

F32 = jnp.float32
BF16 = jnp.bfloat16


def _cparams(*sem):
    return pltpu.CompilerParams(dimension_semantics=sem, vmem_limit_bytes=VMEM_LIMIT)


def _dot(a, b):
    return jnp.dot(a.astype(BF16), b.astype(BF16), preferred_element_type=F32)


def _dot_nt(a, b):
    return lax.dot_general(a.astype(BF16), b.astype(BF16), (((1,), (1,)), ((), ())), preferred_element_type=F32)


def _dot_tn(a, b):
    return lax.dot_general(a.astype(BF16), b.astype(BF16), (((0,), (0,)), ((), ())), preferred_element_type=F32)


def _dot_f32(a, b):
    return jnp.dot(a, b, preferred_element_type=F32, precision=lax.Precision.HIGHEST)


def _dot_split(mat01, x):
    hi = x.astype(BF16)
    lo = (x - hi.astype(F32)).astype(BF16)
    return (jnp.dot(mat01, hi, preferred_element_type=F32) + jnp.dot(mat01, lo, preferred_element_type=F32))


def _sigmoid(x):
    return 1.0 / (1.0 + jnp.exp(-x))


def _silu(x):
    return x * _sigmoid(x)


def _softplus(x):
    return jnp.maximum(x, 0.0) + jnp.log(1.0 + jnp.exp(-jnp.abs(x)))


MOD_TN = 1536


def _mod_kernel(cond_ref, w_ref, b_ref, o_ref):
    c = _silu(cond_ref[...])
    o_ref[...] = _dot(c, w_ref[...]) + b_ref[...]


def _modulation_call(cond, w_ada, b_ada):
    n_col = 6 * D_MODEL
    return pl.pallas_call(
        _mod_kernel,
        grid=(DEPTH, n_col // MOD_TN),
        in_specs=[
            pl.BlockSpec((N_COND, D_MODEL), lambda l, j: (0, 0)),
            pl.BlockSpec((None, D_MODEL, MOD_TN), lambda l, j: (l, 0, j)),
            pl.BlockSpec((None, 1, MOD_TN), lambda l, j: (l, 0, j)),
        ],
        out_specs=pl.BlockSpec((None, N_COND, MOD_TN), lambda l, j: (l, 0, j)),
        out_shape=jax.ShapeDtypeStruct((DEPTH, N_COND, n_col), F32),
        compiler_params=_cparams("arbitrary", "arbitrary"),
        name="modulation",
    )(cond, w_ada, b_ada.reshape(DEPTH, 1, n_col))


def _mod_row(i):
    n_ctx = N_CTX_TOK // TOK_TILE
    per_seq = DEC_SEQ // TOK_TILE
    return jnp.where(i < n_ctx, 0, 1 + (jnp.maximum(i - n_ctx, 0)) // per_seq)


def _adaln(x, gain, shift, scale):
    ms = jnp.mean(x * x, axis=-1, keepdims=True)
    return (x * lax.rsqrt(ms + EPS) * gain) * (1.0 + scale) + shift


def _seg_rms(x, bd, gain):
    x2 = x * x
    hi = x2.astype(BF16)
    lo = (x2 - hi.astype(F32)).astype(BF16)
    ms = jnp.dot(hi, bd, preferred_element_type=F32) + jnp.dot(lo, bd, preferred_element_type=F32)
    return x * lax.rsqrt(ms + EPS) * gain


def _rope(x, cos, sin_signed):
    w = x.shape[-1]
    lane = lax.broadcasted_iota(jnp.int32, x.shape, 1)
    first = (lane % (2 * ROPE_PAIRS)) < ROPE_PAIRS
    partner = jnp.where(first, pltpu.roll(x, w - ROPE_PAIRS, axis=1), pltpu.roll(x, ROPE_PAIRS, axis=1))
    return x * cos + partner * sin_signed


def _inproj_kernel(x_ref, mod_ref, gain_ref, w_ref, bd_ref, qg_ref, kg_ref, cos_ref, sin_ref,
                   q_ref, kv_ref, qkvd_ref, z_ref, ab_ref):
    i = pl.program_id(0)
    is_lat = i >= N_CTX_TOK // TOK_TILE
    shift = mod_ref[:, 0:D_MODEL]
    scale = mod_ref[:, D_MODEL:2 * D_MODEL]
    h = _adaln(x_ref[...], gain_ref[...], shift, scale).astype(BF16)

    qa = jnp.dot(h, w_ref[:, 0:ATT_Q], preferred_element_type=F32)
    qn = _seg_rms(qa, bd_ref[...], qg_ref[...])
    ka = jnp.dot(h, w_ref[:, ATT_Q:ATT_Q + ATT_KV], preferred_element_type=F32)
    kn = _seg_rms(ka, bd_ref[0:ATT_KV, 0:ATT_KV], kg_ref[...])

    @pl.when(is_lat)
    def _():
        q_ref[...] = (_rope(qn, cos_ref[...], sin_ref[...]) * ATT_SCALE).astype(BF16)
        kv_ref[:, 0:ATT_KV] = _rope(kn, cos_ref[:, 0:ATT_KV], sin_ref[:, 0:ATT_KV])

    @pl.when(jnp.logical_not(is_lat))
    def _():
        q_ref[...] = (qn * ATT_SCALE).astype(BF16)
        kv_ref[:, 0:ATT_KV] = kn

    c0 = ATT_Q + ATT_KV
    kv_ref[:, ATT_KV:2 * ATT_KV] = jnp.dot(h, w_ref[:, c0:c0 + ATT_KV], preferred_element_type=F32)
    c0 += ATT_KV
    for j in range(DN_CONV_CH // 512):
        qkvd_ref[:, j * 512:(j + 1) * 512] = jnp.dot(
            h, w_ref[:, c0 + j * 512:c0 + (j + 1) * 512], preferred_element_type=F32).astype(BF16)
    c0 += DN_CONV_CH
    z_ref[...] = jnp.dot(h, w_ref[:, c0:c0 + DN_V], preferred_element_type=F32).astype(BF16)
    c0 += DN_V
    ab_ref[...] = jnp.dot(h, w_ref[:, c0:c0 + N_GATE_COLS], preferred_element_type=F32)


def _inproj_call(l, x, mod, gain, w_in, bd, qg, kg, cos, sin):
    n_ctx = N_CTX_TOK // TOK_TILE
    per_seq = DEC_SEQ // TOK_TILE

    def pos_map(i):
        return (jnp.maximum(i - n_ctx, 0) % per_seq, 0)

    tok = lambda w: pl.BlockSpec((TOK_TILE, w), lambda i: (i, 0))
    full = lambda a: pl.BlockSpec(a.shape, lambda i: (0,) * a.ndim)
    return pl.pallas_call(
        _inproj_kernel,
        grid=(N_TOK // TOK_TILE,),
        in_specs=[
            tok(D_MODEL),
            pl.BlockSpec((None, None, 1, 6 * D_MODEL), lambda i: (l, _mod_row(i), 0, 0)),
            pl.BlockSpec((None, 1, D_MODEL), lambda i: (l, 0, 0)),
            pl.BlockSpec((None, D_MODEL, IN_DIM), lambda i: (l, 0, 0)),
            full(bd),
            pl.BlockSpec((None, 1, ATT_Q), lambda i: (l, 0, 0)),
            pl.BlockSpec((None, 1, ATT_KV), lambda i: (l, 0, 0)),
            pl.BlockSpec((TOK_TILE, ATT_Q), pos_map),
            pl.BlockSpec((TOK_TILE, ATT_Q), pos_map),
        ],
        out_specs=[tok(ATT_Q), tok(2 * ATT_KV), tok(DN_CONV_CH), tok(DN_V), tok(N_GATE_COLS)],
        out_shape=[
            jax.ShapeDtypeStruct((N_TOK, ATT_Q), BF16),
            jax.ShapeDtypeStruct((N_TOK, 2 * ATT_KV), F32),
            jax.ShapeDtypeStruct((N_TOK, DN_CONV_CH), BF16),
            jax.ShapeDtypeStruct((N_TOK, DN_V), BF16),
            jax.ShapeDtypeStruct((N_TOK, N_GATE_COLS), F32),
        ],
        compiler_params=_cparams("arbitrary"),
        name="inproj",
    )(x, mod, gain, w_in, bd, qg, kg, cos, sin)


def _dup_half(x, g):
    lane = lax.broadcasted_iota(jnp.int32, x.shape, 1)
    lo = lane < HEAD_DIM
    xr = pltpu.roll(x, HEAD_DIM, axis=1)
    return jnp.where(lo, x, xr) if g == 0 else jnp.where(lo, xr, x)


def _attend(sink_ref, q_ref, o_ref, key_sets):
    rows = q_ref.shape[0]
    lane = lax.broadcasted_iota(jnp.int32, (rows, LANES), 1)
    lo = lane < HEAD_DIM
    for g in range(N_KV):
        ks = [(_dup_half(k, g).astype(BF16), _dup_half(v, g).astype(BF16), valid) for k, v, valid in key_sets]
        for jj in range(2):
            j = g * 2 + jj
            qp = q_ref[:, j * LANES:(j + 1) * LANES]
            outs = []
            for e in range(2):
                qm = jnp.where(lo if e == 0 else jnp.logical_not(lo), qp, jnp.zeros_like(qp))
                sk = sink_ref[2 * j + e]
                scores = []
                m = jnp.full((rows, 1), sk, F32)
                for k, _, valid in ks:
                    s = _dot_nt(qm, k)
                    if valid is not None:
                        s = jnp.where(valid, s, -jnp.inf)
                    scores.append(s)
                    m = jnp.maximum(m, jnp.max(s, axis=-1, keepdims=True))
                den = jnp.exp(sk - m)
                acc = jnp.zeros((rows, LANES), F32)
                for s, (_, v, _) in zip(scores, ks):
                    p = jnp.exp(s - m)
                    den = den + jnp.sum(p, axis=-1, keepdims=True)
                    acc = acc + _dot(p, v)
                outs.append(acc / den)
            o_ref[:, j * LANES:(j + 1) * LANES] = jnp.where(lo, outs[0], outs[1]).astype(o_ref.dtype)


def _attn_ctx_kernel(sink_ref, q_ref, kv_ref, o_ref):
    _attend(sink_ref, q_ref, o_ref, [(kv_ref[:, 0:ATT_KV], kv_ref[:, ATT_KV:2 * ATT_KV], None)])


def _attn_lat_kernel(sink_ref, q_ref, kv_ref, ck_ref, cv_ref, mix_in_ref, o_ref):
    del mix_in_ref
    n = pl.program_id(1)
    span = LAT_Q_ROWS + 2 * WINDOW
    start = pl.multiple_of(jnp.clip(n * LAT_Q_ROWS - WINDOW, 0, DEC_SEQ - span), BLOCK)
    kw = kv_ref[pl.ds(start, span), 0:ATT_KV]
    vw = kv_ref[pl.ds(start, span), ATT_KV:2 * ATT_KV]
    qpos = n * LAT_Q_ROWS + lax.broadcasted_iota(jnp.int32, (LAT_Q_ROWS, span), 0)
    kpos = start + lax.broadcasted_iota(jnp.int32, (LAT_Q_ROWS, span), 1)
    valid = jnp.abs(kpos - qpos) <= WINDOW
    _attend(sink_ref, q_ref, o_ref, [(kw, vw, valid), (ck_ref[...], cv_ref[...], None)])


def _attn_ctx_call(sink_l, q, kv):
    return pl.pallas_call(
        _attn_ctx_kernel,
        grid=(BATCH,),
        in_specs=[
            pl.BlockSpec(memory_space=pltpu.SMEM),
            pl.BlockSpec((SEQ, ATT_Q), lambda b: (b, 0)),
            pl.BlockSpec((SEQ, 2 * ATT_KV), lambda b: (b, 0)),
        ],
        out_specs=pl.BlockSpec((SEQ, ATT_Q), lambda b: (b, 0)),
        out_shape=jax.ShapeDtypeStruct((N_TOK, ATT_Q + DN_V), BF16),
        compiler_params=_cparams("arbitrary"),
        name="attn_ctx",
    )(sink_l, q, kv)


def _attn_lat_call(l, sink_l, q, kv, cache_k, cache_v, mix):
    nb = DEC_SEQ // LAT_Q_ROWS
    q0 = N_CTX_TOK // LAT_Q_ROWS
    s0 = N_CTX_TOK // DEC_SEQ
    return pl.pallas_call(
        _attn_lat_kernel,
        grid=(DEC_BATCH, nb),
        in_specs=[
            pl.BlockSpec(memory_space=pltpu.SMEM),
            pl.BlockSpec((LAT_Q_ROWS, ATT_Q), lambda b, n: (q0 + b * nb + n, 0)),
            pl.BlockSpec((DEC_SEQ, 2 * ATT_KV), lambda b, n: (s0 + b, 0)),
            pl.BlockSpec((None, None, PAST_LEN, ATT_KV), lambda b, n: (b, l, 0, 0)),
            pl.BlockSpec((None, None, PAST_LEN, ATT_KV), lambda b, n: (b, l, 0, 0)),
            pl.BlockSpec(memory_space=pl.ANY),
        ],
        out_specs=pl.BlockSpec((LAT_Q_ROWS, ATT_Q), lambda b, n: (q0 + b * nb + n, 0)),
        out_shape=jax.ShapeDtypeStruct((N_TOK, ATT_Q + DN_V), BF16),
        input_output_aliases={5: 0},
        compiler_params=_cparams("arbitrary", "arbitrary"),
        name="attn_lat",
    )(sink_l, q, kv, cache_k, cache_v, mix)


INV_PASSES = 1
GDN_UNROLL = 4
GDN_CTX_SEQS = 4


def _split_bf16(x):
    hi = x.astype(BF16)
    return hi, (x - hi.astype(F32)).astype(BF16)


def _block_diag(x, width):
    n = x.shape[1] // width
    blk = lax.broadcasted_iota(jnp.int32, x.shape, 1) // width
    zero = jnp.zeros_like(x)
    return jnp.concatenate([jnp.where(blk == h, x, zero) for h in range(n)], axis=0)


def _mm_heads(ts, xs, width):
    if INV_PASSES == 1:
        bds = [_block_diag(x.astype(BF16), width) for x in xs]
        return [jnp.dot(t.astype(BF16), bd, preferred_element_type=F32) for t, bd in zip(ts, bds)]
    tsp = [_split_bf16(t) for t in ts]
    xsp = [_split_bf16(x) for x in xs]
    bd_hi = [_block_diag(x_hi, width) for x_hi, _ in xsp]
    bd_lo = [_block_diag(x_lo, width) for _, x_lo in xsp]
    return [jnp.dot(t_hi, bh, preferred_element_type=F32) + jnp.dot(t_lo, bh, preferred_element_type=F32)
            + jnp.dot(t_hi, bl, preferred_element_type=F32) for (t_hi, t_lo), bh, bl in zip(tsp, bd_hi, bd_lo)]


def _unit_tri_inverse(mats, row, col):
    eye = (row == col).astype(F32)
    a8 = [jnp.where((row // 8) == (col // 8), a, 0.0) for a in mats]
    a8_2 = _mm_heads(a8, a8, CHUNK)
    a8_4 = _mm_heads(a8_2, a8_2, CHUNK)
    ts = [eye - a for a in a8]
    ts = [t + p for t, p in zip(ts, _mm_heads(ts, a8_2, CHUNK))]
    ts = [t + p for t, p in zip(ts, _mm_heads(ts, a8_4, CHUNK))]
    b = 8
    while b < CHUNK:
        level = ((row // (2 * b)) == (col // (2 * b))) & ((row // b) != (col // b))
        off = [jnp.where(level, a, 0.0) for a in mats]
        ts = [t - p for t, p in zip(ts, _mm_heads(_mm_heads(ts, off, CHUNK), ts, CHUNK))]
        b *= 2
    return ts


def _bcast_cols(x, first, width):
    rows = x.shape[0]
    if width == LANES:
        return jnp.concatenate([jnp.broadcast_to(x[:, first + h:first + h + 1], (rows, width)) for h in range(DN_HEADS)], axis=1)
    blk = lax.broadcasted_iota(jnp.int32, (rows, DN_HEADS * width), 1) // width
    out = jnp.broadcast_to(x[:, first:first + 1], (rows, DN_HEADS * width))
    for h in range(1, DN_HEADS):
        out = jnp.where(blk == h, jnp.broadcast_to(x[:, first + h:first + h + 1], (rows, DN_HEADS * width)), out)
    return out


def _gdn_kernel(*refs, seq_len, n_seq, has_s0, write_state, n_alias):
    qkvd_ref, z_ref, ab_ref, convw_ref, alog_ref, dtb_ref, ogain_ref = refs[:7]
    pos = 7
    s0_ref = None
    if has_s0:
        s0_ref = refs[pos]
        pos += 1
    pos += n_alias
    dn_ref = refs[pos]
    pos += 1
    sfin_ref = None
    if write_state:
        sfin_ref = refs[pos]
        pos += 1
    qn_s, kn_s, vn_s, gb_s, u0_s, w_s, qd_s, kd_s, qk_s, gt_s, o_s, st_s, xpad_s = refs[pos:]
    L = seq_len
    n_chunks = L // CHUNK
    n_pairs = DN_HEADS // 2
    pair_w = 2 * DK
    pad = (DN_CONV - 1) // 2

    halo = 8

    def finish(y, kind):
        y = _silu(y)
        if kind == 0:
            return y * lax.rsqrt(jnp.sum(y * y, axis=-1, keepdims=True) + EPS) * (DK ** -0.5)
        if kind == 1:
            return y * lax.rsqrt(jnp.sum(y * y, axis=-1, keepdims=True) + EPS)
        return y

    xpad_s[0:halo, :] = jnp.zeros((halo, LANES), F32)
    xpad_s[halo + L:2 * halo + L, :] = jnp.zeros((halo, LANES), F32)
    for kind, dst in enumerate((qn_s, kn_s, vn_s)):
        def conv_head(head, carry, kind=kind, dst=dst):
            cs = pl.ds(pl.multiple_of((kind * DN_HEADS + head) * LANES, LANES), LANES)
            hs = pl.ds(pl.multiple_of(head * DK, DK), DK)
            for s in range(n_seq):
                r0 = s * L
                for rb in range(0, L, SEQ):
                    xpad_s[halo + rb:halo + rb + SEQ, :] = qkvd_ref[r0 + rb:r0 + rb + SEQ, cs].astype(F32)
                for rb in range(0, L, SEQ):
                    acc = None
                    for i in range(DN_CONV):
                        lo = halo + rb + i - pad
                        term = xpad_s[lo:lo + SEQ, :] * convw_ref[i:i + 1, cs]
                        acc = term if acc is None else acc + term
                    dst[r0 + rb:r0 + rb + SEQ, hs] = finish(acc, kind)
            return carry

        lax.fori_loop(0, DN_HEADS, conv_head, 0)

    ab = ab_ref[...]
    glane = lax.broadcasted_iota(jnp.int32, ab.shape, 1)
    g = -jnp.exp(alog_ref[...]) * _softplus(ab + dtb_ref[...])
    gb_s[...] = jnp.where(glane < N_DIR * DN_HEADS, g, _sigmoid(ab))

    for s in range(n_seq):
        for d in range(N_DIR):
            for p in range(n_pairs):
                idx = (s * N_DIR + d) * n_pairs + p
                st_s[idx] = jnp.zeros((pair_w, pair_w), F32)
                if has_s0:
                    st_s[idx, 0:DK, 0:DV] = s0_ref[s, d, 2 * p]
                    st_s[idx, DK:2 * DK, DV:2 * DV] = s0_ref[s, d, 2 * p + 1]

    row = lax.broadcasted_iota(jnp.int32, (CHUNK, DN_HEADS * CHUNK), 0)
    col = lax.broadcasted_iota(jnp.int32, (CHUNK, DN_HEADS * CHUNK), 1) % CHUNK
    before_incl = [col <= row, col >= row]
    before_strict = [col < row, col > row]
    after_strict01 = [m.astype(F32) for m in before_strict]
    r64 = lax.broadcasted_iota(jnp.int32, (CHUNK, CHUNK), 0)
    c64 = lax.broadcasted_iota(jnp.int32, (CHUNK, CHUNK), 1)
    tri01 = [(c64 <= r64).astype(BF16), (c64 >= r64).astype(BF16)]

    def prep_step(n, carry):
        a_mats, v_betas, kb_egcs, slots = [], [], [], []
        for j in range(GDN_UNROLL):
            c = n * GDN_UNROLL + j
            rs = pl.ds(pl.multiple_of(c * CHUNK, CHUNK), CHUNK)
            gbc = gb_s[rs, :]
            q = qn_s[rs, :]
            k = kn_s[rs, :]
            v = vn_s[rs, :]
            k_bd = _block_diag(k.astype(BF16), DK)
            betas = [_bcast_cols(gbc, N_DIR * DN_HEADS + d * DN_HEADS, DK) for d in range(N_DIR)]
            kbs = [k * beta for beta in betas]
            kk_qk = _dot_nt(jnp.concatenate(kbs + [q], axis=0), k_bd)
            qk_raw = kk_qk[N_DIR * CHUNK:(N_DIR + 1) * CHUNK]
            for d in range(N_DIR):
                first = d * DN_HEADS
                gc_all = _dot_split(tri01[d], gbc)
                last = CHUNK - 1 if d == 0 else 0
                gc_last = gc_all[last:last + 1, :]
                gt_s[d, pl.ds(c, 1), :] = jnp.exp(gc_last)
                e_gc = _bcast_cols(jnp.exp(gc_all), first, DK)
                e_kd = _bcast_cols(jnp.exp(gc_last - gc_all), first, DK)
                gdiff = _dot_split(tri01[d], _bcast_cols(gbc, first, CHUNK) * after_strict01[d])
                decay = jnp.where(before_incl[d], jnp.exp(gdiff), 0.0)
                a_mats.append(jnp.where(before_strict[d], kk_qk[d * CHUNK:(d + 1) * CHUNK] * decay, 0.0))
                qk_s[d, rs, :] = jnp.where(before_incl[d], qk_raw * decay, 0.0).astype(BF16)
                qd_s[d, rs, :] = (q * e_gc).astype(BF16)
                kd_s[d, rs, :] = (k * e_kd).astype(BF16)
                v_betas.append(v * betas[d])
                kb_egcs.append(kbs[d] * e_gc)
                slots.append((d, rs))
        t_invs = _unit_tri_inverse(a_mats, row, col)
        for (d, rs), u0, w in zip(slots, _mm_heads(t_invs, v_betas, DV), _mm_heads(t_invs, kb_egcs, DK)):
            u0_s[d, rs, :] = u0
            w_s[d, rs, :] = w.astype(BF16)
        return carry

    lax.fori_loop(0, n_seq * n_chunks // GDN_UNROLL, prep_step, 0)

    plane = lax.broadcasted_iota(jnp.int32, (1, pair_w), 1)
    srow = lax.broadcasted_iota(jnp.int32, (pair_w, pair_w), 0) // DK
    scol = lax.broadcasted_iota(jnp.int32, (pair_w, pair_w), 1) // DV
    same_head = srow == scol

    def scan_step(n, carry):
        probs = []
        for s in range(n_seq):
            for d in range(N_DIR):
                c = s * n_chunks + (n if d == 0 else n_chunks - 1 - n)
                rs = pl.ds(pl.multiple_of(c * CHUNK, CHUNK), CHUNK)
                gt = gt_s[d, pl.ds(c, 1), :]
                for p in range(n_pairs):
                    c0 = d * DN_HEADS + 2 * p
                    g_tot = jnp.where(plane < DV, gt[:, c0:c0 + 1], gt[:, c0 + 1:c0 + 2])
                    probs.append((d, rs, p, (s * N_DIR + d) * n_pairs + p, slice(p * pair_w, (p + 1) * pair_w), g_tot))
        s_prev = [st_s[idx] for _, _, _, idx, _, _ in probs]
        s_b = [s.astype(BF16) for s in s_prev]
        ws = [jnp.dot(w_s[d, rs, ps], sb, preferred_element_type=F32) for (d, rs, _, _, ps, _), sb in zip(probs, s_b)]
        u_b = [(u0_s[d, rs, ps] - w).astype(BF16) for (d, rs, _, _, ps, _), w in zip(probs, ws)]
        upd = [lax.dot_general(kd_s[d, rs, ps], u, (((0,), (0,)), ((), ())), preferred_element_type=F32)
               for (d, rs, _, _, ps, _), u in zip(probs, u_b)]
        for (_, _, _, idx, _, g_tot), s, up in zip(probs, s_prev, upd):
            st_s[idx] = s * g_tot + jnp.where(same_head, up, 0.0)
        for (d, rs, p, _, ps, _), sb, u in zip(probs, s_b, u_b):
            o_s[d, rs, ps] = (jnp.dot(qd_s[d, rs, ps], sb, preferred_element_type=F32)
                              + jnp.dot(qk_s[d, rs, p * 2 * CHUNK:(p + 1) * 2 * CHUNK], _block_diag(u, DV),
                                        preferred_element_type=F32))
        return carry

    lax.fori_loop(0, n_chunks, scan_step, 0)

    for h in range(DN_HEADS):
        hs = slice(h * DV, (h + 1) * DV)
        o = o_s[0, :, hs] + o_s[1, :, hs]
        y = o * lax.rsqrt(jnp.mean(o * o, axis=-1, keepdims=True) + EPS) * ogain_ref[...]
        dn_ref[:, hs] = (y * _silu(z_ref[:, hs].astype(F32))).astype(dn_ref.dtype)
    if write_state:
        for s in range(n_seq):
            for d in range(N_DIR):
                for p in range(n_pairs):
                    idx = (s * N_DIR + d) * n_pairs + p
                    sfin_ref[s, d, 2 * p] = st_s[idx, 0:DK, 0:DV]
                    sfin_ref[s, d, 2 * p + 1] = st_s[idx, DK:2 * DK, DV:2 * DV]


def _gdn_call(l, qkvd, z, ab, conv_w, alog16, dtb16, o_gain, state_dn, mix, new_state=None, *, latent):
    L = DEC_SEQ if latent else SEQ
    n_seq = 1 if latent else GDN_CTX_SEQS
    n_steps = (DEC_BATCH if latent else BATCH) // n_seq
    rows = n_seq * L
    base = N_CTX_TOK // rows if latent else 0
    seq = lambda w: pl.BlockSpec((rows, w), lambda b: (base + b, 0))
    in_specs = [
        seq(DN_CONV_CH), seq(DN_V), seq(N_GATE_COLS),
        pl.BlockSpec((None, DN_CONV, DN_CONV_CH), lambda b: (l, 0, 0)),
        pl.BlockSpec((None, 1, N_GATE_COLS), lambda b: (l, 0, 0)),
        pl.BlockSpec((None, 1, N_GATE_COLS), lambda b: (l, 0, 0)),
        pl.BlockSpec((None, 1, DV), lambda b: (l, 0, 0)),
    ]
    args = [qkvd, z, ab, conv_w, alog16, dtb16, o_gain]
    out_specs = [pl.BlockSpec((rows, DN_V), lambda b: (base + b, 1))]
    out_shape = [jax.ShapeDtypeStruct((N_TOK, ATT_Q + DN_V), BF16)]
    if latent:
        in_specs.append(pl.BlockSpec((n_seq, None, N_DIR, DN_HEADS, DK, DV), lambda b: (b, l, 0, 0, 0, 0)))
        args.append(state_dn)
    in_specs.append(pl.BlockSpec(memory_space=pl.ANY))
    args.append(mix)
    aliases = {len(args) - 1: 0}
    if not latent:
        out_specs.append(pl.BlockSpec((n_seq, None, N_DIR, DN_HEADS, DK, DV), lambda b: (b, l, 0, 0, 0, 0)))
        out_shape.append(jax.ShapeDtypeStruct((BATCH, DEPTH, N_DIR, DN_HEADS, DK, DV), F32))
        if new_state is not None:
            in_specs.append(pl.BlockSpec(memory_space=pl.ANY))
            args.append(new_state)
            aliases[len(args) - 1] = 1
    n_chunks = rows // CHUNK
    return pl.pallas_call(
        functools.partial(_gdn_kernel, seq_len=L, n_seq=n_seq, has_s0=latent, write_state=not latent,
                          n_alias=len(aliases)),
        grid=(n_steps,),
        in_specs=in_specs,
        out_specs=out_specs,
        out_shape=out_shape,
        scratch_shapes=[
            pltpu.VMEM((rows, DN_QK), F32), pltpu.VMEM((rows, DN_QK), F32), pltpu.VMEM((rows, DN_V), F32),
            pltpu.VMEM((rows, N_GATE_COLS), F32),
            pltpu.VMEM((N_DIR, rows, DN_V), F32),
            pltpu.VMEM((N_DIR, rows, DN_QK), BF16),
            pltpu.VMEM((N_DIR, rows, DN_QK), BF16),
            pltpu.VMEM((N_DIR, rows, DN_QK), BF16),
            pltpu.VMEM((N_DIR, rows, DN_HEADS * CHUNK), BF16),
            pltpu.VMEM((N_DIR, n_chunks, N_GATE_COLS), F32),
            pltpu.VMEM((N_DIR, rows, DN_V), F32),
            pltpu.VMEM((n_seq * N_DIR * (DN_HEADS // 2), 2 * DK, 2 * DV), F32),
            pltpu.VMEM((L + 16, LANES), F32),
        ],
        input_output_aliases=aliases,
        compiler_params=_cparams("arbitrary"),
        name="gdn_lat" if latent else "gdn_ctx",
    )(*args)


def _top2_of4(vals):
    m1 = jnp.maximum(jnp.maximum(vals[0], vals[1]), jnp.maximum(vals[2], vals[3]))
    i1 = jnp.where(vals[0] == m1, 0, jnp.where(vals[1] == m1, 1, jnp.where(vals[2] == m1, 2, 3)))
    rest = [jnp.where(i1 == j, -jnp.inf, vals[j]) for j in range(4)]
    m2 = jnp.maximum(jnp.maximum(rest[0], rest[1]), jnp.maximum(rest[2], rest[3]))
    i2 = jnp.where(rest[0] == m2, 0, jnp.where(rest[1] == m2, 1, jnp.where(rest[2] == m2, 2, 3)))
    return m1, i1, m2, i2


def _outproj_kernel(x_ref, mix_ref, mod_ref, gain_ref, wo_ref, rwt_ref, rb_ref, ut_ref,
                    x1_ref, h2_ref, ri_ref, rg_ref, cnt_ref, carry_s):
    i = pl.program_id(0)

    @pl.when(i == 0)
    def _():
        carry_s[...] = jnp.zeros_like(carry_s)

    gate1 = mod_ref[:, 2 * D_MODEL:3 * D_MODEL]
    shift2 = mod_ref[:, 3 * D_MODEL:4 * D_MODEL]
    scale2 = mod_ref[:, 4 * D_MODEL:5 * D_MODEL]
    x1 = x_ref[...] + gate1 * jnp.dot(mix_ref[...], wo_ref[...], preferred_element_type=F32)
    x1_ref[...] = x1
    h2 = _adaln(x1, gain_ref[...], shift2, scale2)
    h2_ref[...] = h2

    h_hi, h_lo = _split_bf16(h2)
    r_hi, r_lo = _split_bf16(rwt_ref[...])
    nt = (((1,), (1,)), ((), ()))
    logits = (lax.dot_general(r_hi, h_hi, nt, preferred_element_type=F32)
              + lax.dot_general(r_hi, h_lo, nt, preferred_element_type=F32)
              + lax.dot_general(r_lo, h_hi, nt, preferred_element_type=F32))
    scores = _sigmoid(logits)
    biased = scores + rb_ref[...]
    rows = [biased[e:e + 1, :] for e in range(N_EXPERTS)]
    per_group = [_top2_of4(rows[g * 4:(g + 1) * 4]) for g in range(N_GROUPS)]
    gs = [pg[0] + pg[2] for pg in per_group]
    _, gsel, _, _ = _top2_of4(gs)
    pick = lambda k: jnp.where(gsel == 0, per_group[0][k], jnp.where(gsel == 1, per_group[1][k],
                               jnp.where(gsel == 2, per_group[2][k], per_group[3][k])))
    e1 = gsel * EXPERTS_PER_GROUP + pick(1)
    e2 = gsel * EXPERTS_PER_GROUP + pick(3)
    erow = lax.broadcasted_iota(jnp.int32, scores.shape, 0)
    oh1 = erow == e1
    oh2 = erow == e2
    s1 = jnp.sum(jnp.where(oh1, scores, 0.0), axis=0, keepdims=True)
    s2 = jnp.sum(jnp.where(oh2, scores, 0.0), axis=0, keepdims=True)
    tot = s1 + s2
    grow = lax.broadcasted_iota(jnp.int32, rg_ref.shape, 0)
    rg_ref[...] = jnp.where(grow == 0, s1 / tot, s2 / tot)

    sel = (oh1 | oh2).astype(BF16)
    before = jnp.dot(sel, ut_ref[...], preferred_element_type=F32) + carry_s[...]
    r1 = jnp.sum(jnp.where(oh1, before, 0.0), axis=0, keepdims=True)
    r2 = jnp.sum(jnp.where(oh2, before, 0.0), axis=0, keepdims=True)
    carry_s[...] = carry_s[...] + jnp.sum(sel.astype(F32), axis=1, keepdims=True)
    cnt_ref[...] = carry_s[...].astype(jnp.int32)
    irow = lax.broadcasted_iota(jnp.int32, ri_ref.shape, 0)
    ri_ref[...] = jnp.where(irow == 0, e1, jnp.where(irow == 1, e2, jnp.where(
        irow == 2, r1.astype(jnp.int32), r2.astype(jnp.int32))))


def _outproj_call(l, x, mix, mod, gain, w_out, router_wt, router_b, ut):
    tok = lambda w: pl.BlockSpec((TOK_TILE, w), lambda i: (i, 0))
    per_tok = lambda r: pl.BlockSpec((r, TOK_TILE), lambda i: (0, i))
    full = lambda a: pl.BlockSpec(a.shape, lambda i: (0,) * a.ndim)
    return pl.pallas_call(
        _outproj_kernel,
        grid=(N_TOK // TOK_TILE,),
        in_specs=[
            tok(D_MODEL), tok(ATT_Q + DN_V),
            pl.BlockSpec((None, None, 1, 6 * D_MODEL), lambda i: (l, _mod_row(i), 0, 0)),
            pl.BlockSpec((None, 1, D_MODEL), lambda i: (l, 0, 0)),
            pl.BlockSpec((None, ATT_Q + DN_V, D_MODEL), lambda i: (l, 0, 0)),
            full(router_wt), full(router_b), full(ut),
        ],
        out_specs=[tok(D_MODEL), tok(D_MODEL), per_tok(4), per_tok(2), pl.BlockSpec((N_EXPERTS, 1), lambda i: (0, 0))],
        out_shape=[
            jax.ShapeDtypeStruct((N_TOK, D_MODEL), F32),
            jax.ShapeDtypeStruct((N_TOK, D_MODEL), F32),
            jax.ShapeDtypeStruct((4, N_TOK), jnp.int32),
            jax.ShapeDtypeStruct((2, N_TOK), F32),
            jax.ShapeDtypeStruct((N_EXPERTS, 1), jnp.int32),
        ],
        scratch_shapes=[pltpu.VMEM((N_EXPERTS, 1), F32)],
        compiler_params=_cparams("arbitrary"),
        name="outproj_router",
    )(x, mix, mod, gain, w_out, router_wt, router_b, ut)


def _row_dma_loops(row_copy):
    def start(g, c):
        for j in range(SUBLANES):
            row_copy(g, j, 0).start()
            row_copy(g, j, 1).start()
        return c

    def wait(g, c):
        for j in range(SUBLANES):
            row_copy(g, j, 0).wait()
            row_copy(g, j, 1).wait()
        return c

    lax.fori_loop(0, TOK_TILE // SUBLANES, start, 0)
    lax.fori_loop(0, TOK_TILE // SUBLANES, wait, 0)


def _dispatch_kernel(pos_ref, h2_ref, buf_in_ref, sorted_ref, sem):
    del buf_in_ref

    def row_copy(g, j, k):
        p = pos_ref[0, k * TOK_TILE + g * SUBLANES + j]
        return pltpu.make_async_copy(h2_ref.at[g, pl.ds(j, 1), :], sorted_ref.at[pl.ds(p, 1), :], sem)

    _row_dma_loops(row_copy)


def _dispatch_call(pos_tiles, h2, zeros_sorted):
    return pl.pallas_call(
        _dispatch_kernel,
        grid=(N_TOK // TOK_TILE,),
        in_specs=[
            pl.BlockSpec((None, 1, TOP_K * TOK_TILE), lambda i: (i, 0, 0), memory_space=pltpu.SMEM),
            pl.BlockSpec((TOK_TILE // SUBLANES, SUBLANES, D_MODEL), lambda i: (i, 0, 0)),
            pl.BlockSpec(memory_space=pl.ANY),
        ],
        out_specs=pl.BlockSpec(memory_space=pl.ANY),
        out_shape=jax.ShapeDtypeStruct((N_SORTED, D_MODEL), F32),
        scratch_shapes=[pltpu.SemaphoreType.DMA(())],
        input_output_aliases={2: 0},
        compiler_params=_cparams("arbitrary"),
        name="moe_dispatch",
    )(pos_tiles, h2, zeros_sorted)


def _experts_kernel(te_ref, nu_ref, x_ref, wg_ref, wu_ref, wd_ref, y_ref, wg_s, wu_s, wd_s):
    j = pl.program_id(0)
    used = j < nu_ref[0]

    @pl.when((j == 0) | (te_ref[j] != te_ref[jnp.maximum(j - 1, 0)]))
    def _():
        wg_s[...] = wg_ref[...].astype(BF16)
        wu_s[...] = wu_ref[...].astype(BF16)
        wd_s[...] = wd_ref[...].astype(BF16)

    @pl.when(used)
    def _():
        x = x_ref[...].astype(BF16)
        hid = _silu(jnp.dot(x, wg_s[...], preferred_element_type=F32)) * jnp.dot(x, wu_s[...], preferred_element_type=F32)
        y_ref[...] = jnp.dot(hid.astype(BF16), wd_s[...], preferred_element_type=F32)

    @pl.when(jnp.logical_not(used))
    def _():
        y_ref[...] = jnp.zeros_like(y_ref)


def _experts_call(l, tile_expert, n_used, xs, w_gate, w_up, w_down):
    row = lambda j, te, nu: (jnp.maximum(jnp.minimum(j, nu[0] - 1), 0), 0)
    grid_spec = pltpu.PrefetchScalarGridSpec(
        num_scalar_prefetch=2,
        grid=(N_EXP_TILES,),
        in_specs=[
            pl.BlockSpec((EXP_TILE, D_MODEL), row),
            pl.BlockSpec((None, None, D_MODEL, D_FF), lambda j, te, nu: (l, te[j], 0, 0)),
            pl.BlockSpec((None, None, D_MODEL, D_FF), lambda j, te, nu: (l, te[j], 0, 0)),
            pl.BlockSpec((None, None, D_FF, D_MODEL), lambda j, te, nu: (l, te[j], 0, 0)),
        ],
        out_specs=pl.BlockSpec((EXP_TILE, D_MODEL), lambda j, te, nu: (j, 0)),
        scratch_shapes=[pltpu.VMEM((D_MODEL, D_FF), BF16), pltpu.VMEM((D_MODEL, D_FF), BF16),
                        pltpu.VMEM((D_FF, D_MODEL), BF16)],
    )
    return pl.pallas_call(
        _experts_kernel,
        grid_spec=grid_spec,
        out_shape=jax.ShapeDtypeStruct((N_SORTED, D_MODEL), F32),
        compiler_params=_cparams("arbitrary"),
        name="moe_experts",
    )(tile_expert, n_used, xs, w_gate, w_up, w_down)


def _combine_kernel(pos_ref, x1_ref, rg_ref, mod_ref, y_ref, out_ref, buf_s, sem):
    def row_copy(g, j, k):
        p = pos_ref[0, k * TOK_TILE + g * SUBLANES + j]
        return pltpu.make_async_copy(y_ref.at[pl.ds(p, 1), :], buf_s.at[k, g, pl.ds(j, 1), :], sem)

    _row_dma_loops(row_copy)
    gate2 = mod_ref[:, 5 * D_MODEL:6 * D_MODEL]
    moe = (buf_s[0].reshape(TOK_TILE, D_MODEL) * rg_ref[:, 0:1] + buf_s[1].reshape(TOK_TILE, D_MODEL) * rg_ref[:, 1:2])
    out_ref[...] = x1_ref[...] + gate2 * moe


def _combine_call(l, pos_tiles, x1, rg, mod, y_sorted):
    tok = lambda w: pl.BlockSpec((TOK_TILE, w), lambda i: (i, 0))
    return pl.pallas_call(
        _combine_kernel,
        grid=(N_TOK // TOK_TILE,),
        in_specs=[
            pl.BlockSpec((None, 1, TOP_K * TOK_TILE), lambda i: (i, 0, 0), memory_space=pltpu.SMEM),
            tok(D_MODEL), tok(2),
            pl.BlockSpec((None, None, 1, 6 * D_MODEL), lambda i: (l, _mod_row(i), 0, 0)),
            pl.BlockSpec(memory_space=pl.ANY),
        ],
        out_specs=tok(D_MODEL),
        out_shape=jax.ShapeDtypeStruct((N_TOK, D_MODEL), F32),
        scratch_shapes=[pltpu.VMEM((TOP_K, TOK_TILE // SUBLANES, SUBLANES, D_MODEL), F32), pltpu.SemaphoreType.DMA(())],
        compiler_params=_cparams("arbitrary"),
        name="moe_combine",
    )(pos_tiles, x1, rg, mod, y_sorted)


def _rope_tables():
    pos = jnp.arange(DEC_SEQ)
    r = (pos // GRID_W).astype(F32)
    c = (pos % GRID_W).astype(F32)
    inv = ROPE_BASE ** (-jnp.arange(ROPE_PAIRS, dtype=F32) / ROPE_PAIRS)
    ar, ac = r[:, None] * inv, c[:, None] * inv
    cos = jnp.concatenate([jnp.cos(ar), jnp.cos(ar), jnp.cos(ac), jnp.cos(ac)], axis=-1)
    sin = jnp.concatenate([-jnp.sin(ar), jnp.sin(ar), -jnp.sin(ac), jnp.sin(ac)], axis=-1)
    return jnp.tile(cos, (1, N_HEADS)), jnp.tile(sin, (1, N_HEADS))


def _routing_layout(ri, counts):
    counts = counts.reshape(N_EXPERTS)
    padded = ((counts + EXP_TILE - 1) // EXP_TILE) * EXP_TILE
    ends = jnp.cumsum(padded)
    offs = ends - padded
    eids = jnp.arange(N_EXPERTS, dtype=jnp.int32)[:, None, None]
    pos = jnp.sum(jnp.where(ri[None, 0:2] == eids, offs[:, None, None], 0), axis=0) + ri[2:4]
    n_used = (ends[-1] // EXP_TILE).astype(jnp.int32)
    tile_start = jnp.arange(N_EXP_TILES, dtype=jnp.int32) * EXP_TILE
    tile_expert = jnp.sum((tile_start[:, None] >= ends[None, :]).astype(jnp.int32), axis=1)
    last = jnp.sum((jnp.maximum(ends[-1] - EXP_TILE, 0) >= ends).astype(jnp.int32))
    tile_expert = jnp.minimum(jnp.where(tile_start < ends[-1], tile_expert, last), N_EXPERTS - 1).astype(jnp.int32)
    pos_tiles = pos.reshape(TOP_K, N_TOK // TOK_TILE, TOK_TILE).transpose(1, 0, 2).reshape(
        N_TOK // TOK_TILE, 1, TOP_K * TOK_TILE).astype(jnp.int32)
    return pos_tiles, tile_expert, n_used.reshape(1)


def kernel(x_prompt, x_sample, cache_k, cache_v, state_dn, c, c_ctx, w_ada, b_ada, norm_attn, norm_ffn,
           w_in, conv_w, a_log, dt_bias, q_norm, k_norm, sink, o_norm, w_out, router_w, router_bias,
           w_gate, w_up, w_down):
    x = jnp.concatenate([x_prompt.reshape(N_CTX_TOK, D_MODEL), x_sample.reshape(N_LAT_TOK, D_MODEL)], axis=0)
    cond = jnp.concatenate([c_ctx[None, :], c, jnp.zeros((N_COND - 1 - DEC_BATCH, D_MODEL), F32)], axis=0)
    mod = _modulation_call(cond, w_ada, b_ada).reshape(DEPTH, N_COND, 1, 6 * D_MODEL)

    w_in_b = w_in.astype(BF16)
    w_out_b = w_out.astype(BF16)
    seg = jnp.arange(ATT_Q) // HEAD_DIM
    bd = jnp.where(seg[:, None] == seg[None, :], 1.0 / HEAD_DIM, 0.0).astype(BF16)
    qg = jnp.tile(q_norm, (1, N_HEADS)).reshape(DEPTH, 1, ATT_Q)
    kg = jnp.tile(k_norm, (1, N_KV)).reshape(DEPTH, 1, ATT_KV)
    cos, sin = _rope_tables()
    gain1 = norm_attn.reshape(DEPTH, 1, D_MODEL)
    gain2 = norm_ffn.reshape(DEPTH, 1, D_MODEL)
    pad8 = lambda a: jnp.pad(a.reshape(DEPTH, 1, N_DIR * DN_HEADS), ((0, 0), (0, 0), (0, N_GATE_COLS - N_DIR * DN_HEADS)))
    alog16, dtb16 = pad8(a_log), pad8(dt_bias)
    o_gain = o_norm.reshape(DEPTH, 1, DV)
    ck = cache_k.reshape(DEC_BATCH, DEPTH, PAST_LEN, ATT_KV)
    cv = cache_v.reshape(DEC_BATCH, DEPTH, PAST_LEN, ATT_KV)
    router_wt = router_w.T
    rb = router_bias.reshape(N_EXPERTS, 1)
    tri = jnp.arange(TOK_TILE)
    ut = (tri[:, None] < tri[None, :]).astype(BF16)
    sorted_buf = jnp.zeros((N_SORTED, D_MODEL), F32)

    k_list, v_list, new_state = [], [], None
    for l in range(DEPTH):
        q, kv, qkvd, z, ab = _inproj_call(l, x, mod, gain1, w_in_b, bd, qg, kg, cos, sin)
        k_list.append(kv[:N_CTX_TOK, 0:ATT_KV].reshape(BATCH, SEQ, N_KV, HEAD_DIM))
        v_list.append(kv[:N_CTX_TOK, ATT_KV:].reshape(BATCH, SEQ, N_KV, HEAD_DIM))
        mix = _attn_ctx_call(sink[l], q, kv)
        mix = _attn_lat_call(l, sink[l], q, kv, ck, cv, mix)
        mix, new_state = _gdn_call(l, qkvd, z, ab, conv_w, alog16, dtb16, o_gain, state_dn, mix, new_state, latent=False)
        (mix,) = _gdn_call(l, qkvd, z, ab, conv_w, alog16, dtb16, o_gain, state_dn, mix, latent=True)
        x1, h2, ri, rg, counts = _outproj_call(l, x, mix, mod, gain2, w_out_b, router_wt, rb, ut)
        pos_tiles, tile_expert, n_used = _routing_layout(ri, counts)
        xs_sorted = _dispatch_call(pos_tiles, h2.reshape(N_TOK // SUBLANES, SUBLANES, D_MODEL), sorted_buf)
        sorted_buf = xs_sorted
        y_sorted = _experts_call(l, tile_expert, n_used, xs_sorted, w_gate, w_up, w_down)
        x = _combine_call(l, pos_tiles, x1, rg.T, mod, y_sorted)

    y_prompt = x[:N_CTX_TOK].reshape(BATCH, SEQ, D_MODEL)
    y_sample = x[N_CTX_TOK:].reshape(DEC_BATCH, DEC_SEQ, D_MODEL)
    return (y_prompt, y_sample, jnp.stack(k_list, axis=1), jnp.stack(v_list, axis=1), new_state)
```

```python
import functools

import jax
import jax.numpy as jnp
from jax import lax
from jax.experimental import pallas as pl
from jax.experimental.pallas import tpu as pltpu

D_MODEL = 1024
BATCH = 32
SEQ = 256
DEPTH = 4
DEC_BATCH = 4
DEC_SEQ = 1024
PAST_LEN = 256
GRID_W = 64
N_HEADS = 8
N_KV = 2
HEAD_DIM = 64
WINDOW = 128
BLOCK = 128
ATT_SCALE = HEAD_DIM ** -0.5
ROPE_BASE = 10000.0
ROPE_PAIRS = HEAD_DIM // 4
DN_HEADS = 4
DK = 128
DV = 128
DN_CONV = 5
CHUNK = 64
N_DIR = 2
ATT_Q = N_HEADS * HEAD_DIM
ATT_KV = N_KV * HEAD_DIM
DN_QK = DN_HEADS * DK
DN_V = DN_HEADS * DV
DN_CONV_CH = 2 * DN_QK + DN_V
N_GATE_COLS = 2 * N_DIR * DN_HEADS
IN_DIM = ATT_Q + 2 * ATT_KV + DN_CONV_CH + DN_V + N_GATE_COLS
N_EXPERTS = 16
N_GROUPS = 4
EXPERTS_PER_GROUP = 4
TOP_K = 2
D_FF = 512
EPS = 1e-6

N_CTX_TOK = BATCH * SEQ
N_LAT_TOK = DEC_BATCH * DEC_SEQ
N_TOK = N_CTX_TOK + N_LAT_TOK
N_COND = 8

LANES = 128
SUBLANES = 8
TOK_TILE = 512
EXP_TILE = 512
N_EXP_TILES = (N_TOK * TOP_K) // EXP_TILE + N_EXPERTS
N_SORTED = N_EXP_TILES * EXP_TILE
VMEM_LIMIT = 56 * 1024 * 1024
ATT_CTX_SEQS = 2
LAT_Q_ROWS = 256

F32 = jnp.float32
BF16 = jnp.bfloat16


def _cparams(*sem):
    return pltpu.CompilerParams(dimension_semantics=sem, vmem_limit_bytes=VMEM_LIMIT)


def _dot(a, b):
    return jnp.dot(a.astype(BF16), b.astype(BF16), preferred_element_type=F32)


def _dot_nt(a, b):
    return lax.dot_general(a.astype(BF16), b.astype(BF16), (((1,), (1,)), ((), ())), preferred_element_type=F32)


def _dot_split(mat01, x):
    hi = x.astype(BF16)
    lo = (x - hi.astype(F32)).astype(BF16)
    return (jnp.dot(mat01, hi, preferred_element_type=F32) + jnp.dot(mat01, lo, preferred_element_type=F32))


def _sigmoid(x):
    return 1.0 / (1.0 + jnp.exp(-x))


def _silu(x):
    return x * _sigmoid(x)


def _softplus(x):
    return jnp.maximum(x, 0.0) + jnp.log(1.0 + jnp.exp(-jnp.abs(x)))


MOD_TN = 1536


def _mod_kernel(cond_ref, w_ref, b_ref, o_ref):
    c = _silu(cond_ref[...])
    o_ref[...] = _dot(c, w_ref[...]) + b_ref[...]


def _modulation_call(cond, w_ada, b_ada):
    n_col = 6 * D_MODEL
    return pl.pallas_call(
        _mod_kernel,
        grid=(DEPTH, n_col // MOD_TN),
        in_specs=[
            pl.BlockSpec((N_COND, D_MODEL), lambda l, j: (0, 0)),
            pl.BlockSpec((None, D_MODEL, MOD_TN), lambda l, j: (l, 0, j)),
            pl.BlockSpec((None, 1, MOD_TN), lambda l, j: (l, 0, j)),
        ],
        out_specs=pl.BlockSpec((None, N_COND, MOD_TN), lambda l, j: (l, 0, j)),
        out_shape=jax.ShapeDtypeStruct((DEPTH, N_COND, n_col), F32),
        compiler_params=_cparams("arbitrary", "arbitrary"),
        name="modulation",
    )(cond, w_ada, b_ada.reshape(DEPTH, 1, n_col))


def _mod_row(i):
    n_ctx = N_CTX_TOK // TOK_TILE
    per_seq = DEC_SEQ // TOK_TILE
    return jnp.where(i < n_ctx, 0, 1 + (jnp.maximum(i - n_ctx, 0)) // per_seq)


def _adaln(x, gain, shift, scale):
    ms = jnp.mean(x * x, axis=-1, keepdims=True)
    return (x * lax.rsqrt(ms + EPS) * gain) * (1.0 + scale) + shift


def _seg_rms(x, bd, gain):
    x2 = x * x
    hi = x2.astype(BF16)
    lo = (x2 - hi.astype(F32)).astype(BF16)
    ms = jnp.dot(hi, bd, preferred_element_type=F32) + jnp.dot(lo, bd, preferred_element_type=F32)
    return x * lax.rsqrt(ms + EPS) * gain


def _rope(x, cos, sin_signed):
    w = x.shape[-1]
    lane = lax.broadcasted_iota(jnp.int32, x.shape, 1)
    first = (lane % (2 * ROPE_PAIRS)) < ROPE_PAIRS
    partner = jnp.where(first, pltpu.roll(x, w - ROPE_PAIRS, axis=1), pltpu.roll(x, ROPE_PAIRS, axis=1))
    return x * cos + partner * sin_signed


def _inproj_kernel(x_ref, mod_ref, gain_ref, w_ref, bd_ref, qg_ref, kg_ref, cos_ref, sin_ref,
                   q_ref, kv_ref, qkvd_ref, z_ref, ab_ref):
    i = pl.program_id(0)
    is_lat = i >= N_CTX_TOK // TOK_TILE
    shift = mod_ref[:, 0:D_MODEL]
    scale = mod_ref[:, D_MODEL:2 * D_MODEL]
    h = _adaln(x_ref[...], gain_ref[...], shift, scale).astype(BF16)

    qa = jnp.dot(h, w_ref[:, 0:ATT_Q], preferred_element_type=F32)
    qn = _seg_rms(qa, bd_ref[...], qg_ref[...])
    ka = jnp.dot(h, w_ref[:, ATT_Q:ATT_Q + ATT_KV], preferred_element_type=F32)
    kn = _seg_rms(ka, bd_ref[0:ATT_KV, 0:ATT_KV], kg_ref[...])

    @pl.when(is_lat)
    def _():
        q_ref[...] = (_rope(qn, cos_ref[...], sin_ref[...]) * ATT_SCALE).astype(BF16)
        kv_ref[:, 0:ATT_KV] = _rope(kn, cos_ref[:, 0:ATT_KV], sin_ref[:, 0:ATT_KV])

    @pl.when(jnp.logical_not(is_lat))
    def _():
        q_ref[...] = (qn * ATT_SCALE).astype(BF16)
        kv_ref[:, 0:ATT_KV] = kn

    c0 = ATT_Q + ATT_KV
    kv_ref[:, ATT_KV:2 * ATT_KV] = jnp.dot(h, w_ref[:, c0:c0 + ATT_KV], preferred_element_type=F32)
    c0 += ATT_KV
    for j in range(DN_CONV_CH // 512):
        qkvd_ref[:, j * 512:(j + 1) * 512] = jnp.dot(
            h, w_ref[:, c0 + j * 512:c0 + (j + 1) * 512], preferred_element_type=F32).astype(BF16)
    c0 += DN_CONV_CH
    z_ref[...] = jnp.dot(h, w_ref[:, c0:c0 + DN_V], preferred_element_type=F32).astype(BF16)
    c0 += DN_V
    ab_ref[...] = jnp.dot(h, w_ref[:, c0:c0 + N_GATE_COLS], preferred_element_type=F32)


def _inproj_call(l, x, mod, gain, w_in, bd, qg, kg, cos, sin):
    n_ctx = N_CTX_TOK // TOK_TILE
    per_seq = DEC_SEQ // TOK_TILE

    def pos_map(i):
        return (jnp.maximum(i - n_ctx, 0) % per_seq, 0)

    tok = lambda w: pl.BlockSpec((TOK_TILE, w), lambda i: (i, 0))
    full = lambda a: pl.BlockSpec(a.shape, lambda i: (0,) * a.ndim)
    return pl.pallas_call(
        _inproj_kernel,
        grid=(N_TOK // TOK_TILE,),
        in_specs=[
            tok(D_MODEL),
            pl.BlockSpec((None, None, 1, 6 * D_MODEL), lambda i: (l, _mod_row(i), 0, 0)),
            pl.BlockSpec((None, 1, D_MODEL), lambda i: (l, 0, 0)),
            pl.BlockSpec((None, D_MODEL, IN_DIM), lambda i: (l, 0, 0)),
            full(bd),
            pl.BlockSpec((None, 1, ATT_Q), lambda i: (l, 0, 0)),
            pl.BlockSpec((None, 1, ATT_KV), lambda i: (l, 0, 0)),
            pl.BlockSpec((TOK_TILE, ATT_Q), pos_map),
            pl.BlockSpec((TOK_TILE, ATT_Q), pos_map),
        ],
        out_specs=[tok(ATT_Q), tok(2 * ATT_KV), tok(DN_CONV_CH), tok(DN_V), tok(N_GATE_COLS)],
        out_shape=[
            jax.ShapeDtypeStruct((N_TOK, ATT_Q), BF16),
            jax.ShapeDtypeStruct((N_TOK, 2 * ATT_KV), F32),
            jax.ShapeDtypeStruct((N_TOK, DN_CONV_CH), BF16),
            jax.ShapeDtypeStruct((N_TOK, DN_V), BF16),
            jax.ShapeDtypeStruct((N_TOK, N_GATE_COLS), F32),
        ],
        compiler_params=_cparams("arbitrary"),
        name="inproj",
    )(x, mod, gain, w_in, bd, qg, kg, cos, sin)


def _dup_half(x, g):
    lane = lax.broadcasted_iota(jnp.int32, x.shape, 1)
    lo = lane < HEAD_DIM
    xr = pltpu.roll(x, HEAD_DIM, axis=1)
    return jnp.where(lo, x, xr) if g == 0 else jnp.where(lo, xr, x)


def _attend(sink_ref, q_ref, o_ref, key_sets):
    rows = q_ref.shape[0]
    lane = lax.broadcasted_iota(jnp.int32, (rows, LANES), 1)
    lo = lane < HEAD_DIM
    for g in range(N_KV):
        ks = [(_dup_half(k, g).astype(BF16), _dup_half(v, g).astype(BF16), valid) for k, v, valid in key_sets]
        for jj in range(2):
            j = g * 2 + jj
            qp = q_ref[:, j * LANES:(j + 1) * LANES]
            outs = []
            for e in range(2):
                qm = jnp.where(lo if e == 0 else jnp.logical_not(lo), qp, jnp.zeros_like(qp))
                sk = sink_ref[2 * j + e]
                scores = []
                m = jnp.full((rows, 1), sk, F32)
                for k, _, valid in ks:
                    s = _dot_nt(qm, k)
                    if valid is not None:
                        s = jnp.where(valid, s, -jnp.inf)
                    scores.append(s)
                    m = jnp.maximum(m, jnp.max(s, axis=-1, keepdims=True))
                den = jnp.exp(sk - m)
                acc = jnp.zeros((rows, LANES), F32)
                for s, (_, v, _) in zip(scores, ks):
                    p = jnp.exp(s - m)
                    den = den + jnp.sum(p, axis=-1, keepdims=True)
                    acc = acc + _dot(p, v)
                outs.append(acc / den)
            o_ref[:, j * LANES:(j + 1) * LANES] = jnp.where(lo, outs[0], outs[1]).astype(o_ref.dtype)


def _attn_ctx_kernel(sink_ref, q_ref, kv_ref, o_ref):
    for s in range(ATT_CTX_SEQS):
        rows = pl.ds(s * SEQ, SEQ)
        _attend(sink_ref, q_ref.at[rows], o_ref.at[rows],
                [(kv_ref[s * SEQ:(s + 1) * SEQ, 0:ATT_KV], kv_ref[s * SEQ:(s + 1) * SEQ, ATT_KV:2 * ATT_KV], None)])


def _attn_lat_kernel(sink_ref, q_ref, kv_ref, ck_ref, cv_ref, mix_in_ref, o_ref):
    del mix_in_ref
    n = pl.program_id(1)
    span = LAT_Q_ROWS + 2 * WINDOW
    start = pl.multiple_of(jnp.clip(n * LAT_Q_ROWS - WINDOW, 0, DEC_SEQ - span), BLOCK)
    kw = kv_ref[pl.ds(start, span), 0:ATT_KV]
    vw = kv_ref[pl.ds(start, span), ATT_KV:2 * ATT_KV]
    qpos = n * LAT_Q_ROWS + lax.broadcasted_iota(jnp.int32, (LAT_Q_ROWS, span), 0)
    kpos = start + lax.broadcasted_iota(jnp.int32, (LAT_Q_ROWS, span), 1)
    valid = jnp.abs(kpos - qpos) <= WINDOW
    _attend(sink_ref, q_ref, o_ref, [(kw, vw, valid), (ck_ref[...], cv_ref[...], None)])


def _attn_ctx_call(sink_l, q, kv):
    return pl.pallas_call(
        _attn_ctx_kernel,
        grid=(BATCH // ATT_CTX_SEQS,),
        in_specs=[
            pl.BlockSpec(memory_space=pltpu.SMEM),
            pl.BlockSpec((ATT_CTX_SEQS * SEQ, ATT_Q), lambda b: (b, 0)),
            pl.BlockSpec((ATT_CTX_SEQS * SEQ, 2 * ATT_KV), lambda b: (b, 0)),
        ],
        out_specs=pl.BlockSpec((ATT_CTX_SEQS * SEQ, ATT_Q), lambda b: (b, 0)),
        out_shape=jax.ShapeDtypeStruct((N_TOK, ATT_Q + DN_V), BF16),
        compiler_params=_cparams("arbitrary"),
        name="attn_ctx",
    )(sink_l, q, kv)


def _attn_lat_call(l, sink_l, q, kv, cache_k, cache_v, mix):
    nb = DEC_SEQ // LAT_Q_ROWS
    q0 = N_CTX_TOK // LAT_Q_ROWS
    s0 = N_CTX_TOK // DEC_SEQ
    return pl.pallas_call(
        _attn_lat_kernel,
        grid=(DEC_BATCH, nb),
        in_specs=[
            pl.BlockSpec(memory_space=pltpu.SMEM),
            pl.BlockSpec((LAT_Q_ROWS, ATT_Q), lambda b, n: (q0 + b * nb + n, 0)),
            pl.BlockSpec((DEC_SEQ, 2 * ATT_KV), lambda b, n: (s0 + b, 0)),
            pl.BlockSpec((None, None, PAST_LEN, ATT_KV), lambda b, n: (b, l, 0, 0)),
            pl.BlockSpec((None, None, PAST_LEN, ATT_KV), lambda b, n: (b, l, 0, 0)),
            pl.BlockSpec(memory_space=pl.ANY),
        ],
        out_specs=pl.BlockSpec((LAT_Q_ROWS, ATT_Q), lambda b, n: (q0 + b * nb + n, 0)),
        out_shape=jax.ShapeDtypeStruct((N_TOK, ATT_Q + DN_V), BF16),
        input_output_aliases={5: 0},
        compiler_params=_cparams("arbitrary", "arbitrary"),
        name="attn_lat",
    )(sink_l, q, kv, cache_k, cache_v, mix)


INV_PASSES = 1
GDN_UNROLL = 4
GDN_CTX_SEQS = 4


def _split_bf16(x):
    hi = x.astype(BF16)
    return hi, (x - hi.astype(F32)).astype(BF16)


def _block_diag(x, width):
    n = x.shape[1] // width
    blk = lax.broadcasted_iota(jnp.int32, x.shape, 1) // width
    zero = jnp.zeros_like(x)
    return jnp.concatenate([jnp.where(blk == h, x, zero) for h in range(n)], axis=0)


def _mm_heads(ts, xs, width):
    if INV_PASSES == 1:
        bds = [_block_diag(x.astype(BF16), width) for x in xs]
        return [jnp.dot(t.astype(BF16), bd, preferred_element_type=F32) for t, bd in zip(ts, bds)]
    tsp = [_split_bf16(t) for t in ts]
    xsp = [_split_bf16(x) for x in xs]
    bd_hi = [_block_diag(x_hi, width) for x_hi, _ in xsp]
    bd_lo = [_block_diag(x_lo, width) for _, x_lo in xsp]
    return [jnp.dot(t_hi, bh, preferred_element_type=F32) + jnp.dot(t_lo, bh, preferred_element_type=F32)
            + jnp.dot(t_hi, bl, preferred_element_type=F32) for (t_hi, t_lo), bh, bl in zip(tsp, bd_hi, bd_lo)]


def _unit_tri_inverse(mats, row, col):
    eye = (row == col).astype(F32)
    a8 = [jnp.where((row // 8) == (col // 8), a, 0.0) for a in mats]
    a8_2 = _mm_heads(a8, a8, CHUNK)
    a8_4 = _mm_heads(a8_2, a8_2, CHUNK)
    ts = [eye - a for a in a8]
    ts = [t + p for t, p in zip(ts, _mm_heads(ts, a8_2, CHUNK))]
    ts = [t + p for t, p in zip(ts, _mm_heads(ts, a8_4, CHUNK))]
    b = 8
    while b < CHUNK:
        level = ((row // (2 * b)) == (col // (2 * b))) & ((row // b) != (col // b))
        off = [jnp.where(level, a, 0.0) for a in mats]
        ts = [t - p for t, p in zip(ts, _mm_heads(_mm_heads(ts, off, CHUNK), ts, CHUNK))]
        b *= 2
    return ts


def _bcast_cols(x, first, width):
    rows = x.shape[0]
    if width == LANES:
        return jnp.concatenate([jnp.broadcast_to(x[:, first + h:first + h + 1], (rows, width)) for h in range(DN_HEADS)], axis=1)
    blk = lax.broadcasted_iota(jnp.int32, (rows, DN_HEADS * width), 1) // width
    out = jnp.broadcast_to(x[:, first:first + 1], (rows, DN_HEADS * width))
    for h in range(1, DN_HEADS):
        out = jnp.where(blk == h, jnp.broadcast_to(x[:, first + h:first + h + 1], (rows, DN_HEADS * width)), out)
    return out


def _gdn_kernel(*refs, seq_len, n_seq, has_s0, write_state, n_alias):
    qkvd_ref, z_ref, ab_ref, convw_ref, alog_ref, dtb_ref, ogain_ref = refs[:7]
    pos = 7
    s0_ref = None
    if has_s0:
        s0_ref = refs[pos]
        pos += 1
    pos += n_alias
    dn_ref = refs[pos]
    pos += 1
    sfin_ref = None
    if write_state:
        sfin_ref = refs[pos]
        pos += 1
    qn_s, kn_s, vn_s, gb_s, u0_s, w_s, qd_s, kd_s, qk_s, gt_s, o_s, st_s, xpad_s = refs[pos:]
    L = seq_len
    n_chunks = L // CHUNK
    n_pairs = DN_HEADS // 2
    pair_w = 2 * DK
    pad = (DN_CONV - 1) // 2

    halo = 8

    def finish(y, kind):
        y = _silu(y)
        if kind == 0:
            return y * lax.rsqrt(jnp.sum(y * y, axis=-1, keepdims=True) + EPS) * (DK ** -0.5)
        if kind == 1:
            return y * lax.rsqrt(jnp.sum(y * y, axis=-1, keepdims=True) + EPS)
        return y

    xpad_s[0:halo, :] = jnp.zeros((halo, LANES), F32)
    xpad_s[halo + L:2 * halo + L, :] = jnp.zeros((halo, LANES), F32)
    for kind, dst in enumerate((qn_s, kn_s, vn_s)):
        def conv_head(head, carry, kind=kind, dst=dst):
            cs = pl.ds(pl.multiple_of((kind * DN_HEADS + head) * LANES, LANES), LANES)
            hs = pl.ds(pl.multiple_of(head * DK, DK), DK)
            for s in range(n_seq):
                r0 = s * L
                for rb in range(0, L, SEQ):
                    xpad_s[halo + rb:halo + rb + SEQ, :] = qkvd_ref[r0 + rb:r0 + rb + SEQ, cs].astype(F32)
                for rb in range(0, L, SEQ):
                    acc = None
                    for i in range(DN_CONV):
                        lo = halo + rb + i - pad
                        term = xpad_s[lo:lo + SEQ, :] * convw_ref[i:i + 1, cs]
                        acc = term if acc is None else acc + term
                    dst[r0 + rb:r0 + rb + SEQ, hs] = finish(acc, kind)
            return carry

        lax.fori_loop(0, DN_HEADS, conv_head, 0)

    ab = ab_ref[...]
    glane = lax.broadcasted_iota(jnp.int32, ab.shape, 1)
    g = -jnp.exp(alog_ref[...]) * _softplus(ab + dtb_ref[...])
    gb_s[...] = jnp.where(glane < N_DIR * DN_HEADS, g, _sigmoid(ab))

    for s in range(n_seq):
        for d in range(N_DIR):
            for p in range(n_pairs):
                idx = (s * N_DIR + d) * n_pairs + p
                st_s[idx] = jnp.zeros((pair_w, pair_w), F32)
                if has_s0:
                    st_s[idx, 0:DK, 0:DV] = s0_ref[s, d, 2 * p]
                    st_s[idx, DK:2 * DK, DV:2 * DV] = s0_ref[s, d, 2 * p + 1]

    row = lax.broadcasted_iota(jnp.int32, (CHUNK, DN_HEADS * CHUNK), 0)
    col = lax.broadcasted_iota(jnp.int32, (CHUNK, DN_HEADS * CHUNK), 1) % CHUNK
    before_incl = [col <= row, col >= row]
    before_strict = [col < row, col > row]
    after_strict01 = [m.astype(F32) for m in before_strict]
    r64 = lax.broadcasted_iota(jnp.int32, (CHUNK, CHUNK), 0)
    c64 = lax.broadcasted_iota(jnp.int32, (CHUNK, CHUNK), 1)
    tri01 = [(c64 <= r64).astype(BF16), (c64 >= r64).astype(BF16)]

    def prep_step(n, carry):
        a_mats, v_betas, kb_egcs, slots = [], [], [], []
        for j in range(GDN_UNROLL):
            c = n * GDN_UNROLL + j
            rs = pl.ds(pl.multiple_of(c * CHUNK, CHUNK), CHUNK)
            gbc = gb_s[rs, :]
            q = qn_s[rs, :]
            k = kn_s[rs, :]
            v = vn_s[rs, :]
            k_bd = _block_diag(k.astype(BF16), DK)
            betas = [_bcast_cols(gbc, N_DIR * DN_HEADS + d * DN_HEADS, DK) for d in range(N_DIR)]
            kbs = [k * beta for beta in betas]
            kk_qk = _dot_nt(jnp.concatenate(kbs + [q], axis=0), k_bd)
            qk_raw = kk_qk[N_DIR * CHUNK:(N_DIR + 1) * CHUNK]
            for d in range(N_DIR):
                first = d * DN_HEADS
                gc_all = _dot_split(tri01[d], gbc)
                last = CHUNK - 1 if d == 0 else 0
                gc_last = gc_all[last:last + 1, :]
                gt_s[d, pl.ds(c, 1), :] = jnp.exp(gc_last)
                e_gc = _bcast_cols(jnp.exp(gc_all), first, DK)
                e_kd = _bcast_cols(jnp.exp(gc_last - gc_all), first, DK)
                gdiff = _dot_split(tri01[d], _bcast_cols(gbc, first, CHUNK) * after_strict01[d])
                decay = jnp.where(before_incl[d], jnp.exp(gdiff), 0.0)
                a_mats.append(jnp.where(before_strict[d], kk_qk[d * CHUNK:(d + 1) * CHUNK] * decay, 0.0))
                qk_s[d, rs, :] = jnp.where(before_incl[d], qk_raw * decay, 0.0).astype(BF16)
                qd_s[d, rs, :] = (q * e_gc).astype(BF16)
                kd_s[d, rs, :] = (k * e_kd).astype(BF16)
                v_betas.append(v * betas[d])
                kb_egcs.append(kbs[d] * e_gc)
                slots.append((d, rs))
        t_invs = _unit_tri_inverse(a_mats, row, col)
        for (d, rs), u0, w in zip(slots, _mm_heads(t_invs, v_betas, DV), _mm_heads(t_invs, kb_egcs, DK)):
            u0_s[d, rs, :] = u0
            w_s[d, rs, :] = w.astype(BF16)
        return carry

    lax.fori_loop(0, n_seq * n_chunks // GDN_UNROLL, prep_step, 0)

    plane = lax.broadcasted_iota(jnp.int32, (1, pair_w), 1)
    srow = lax.broadcasted_iota(jnp.int32, (pair_w, pair_w), 0) // DK
    scol = lax.broadcasted_iota(jnp.int32, (pair_w, pair_w), 1) // DV
    same_head = srow == scol

    def scan_step(n, carry):
        probs = []
        for s in range(n_seq):
            for d in range(N_DIR):
                c = s * n_chunks + (n if d == 0 else n_chunks - 1 - n)
                rs = pl.ds(pl.multiple_of(c * CHUNK, CHUNK), CHUNK)
                gt = gt_s[d, pl.ds(c, 1), :]
                for p in range(n_pairs):
                    c0 = d * DN_HEADS + 2 * p
                    g_tot = jnp.where(plane < DV, gt[:, c0:c0 + 1], gt[:, c0 + 1:c0 + 2])
                    probs.append((d, rs, p, (s * N_DIR + d) * n_pairs + p, slice(p * pair_w, (p + 1) * pair_w), g_tot))
        s_prev = [st_s[idx] for _, _, _, idx, _, _ in probs]
        s_b = [s.astype(BF16) for s in s_prev]
        ws = [jnp.dot(w_s[d, rs, ps], sb, preferred_element_type=F32) for (d, rs, _, _, ps, _), sb in zip(probs, s_b)]
        u_b = [(u0_s[d, rs, ps] - w).astype(BF16) for (d, rs, _, _, ps, _), w in zip(probs, ws)]
        upd = [lax.dot_general(kd_s[d, rs, ps], u, (((0,), (0,)), ((), ())), preferred_element_type=F32)
               for (d, rs, _, _, ps, _), u in zip(probs, u_b)]
        for (_, _, _, idx, _, g_tot), s, up in zip(probs, s_prev, upd):
            st_s[idx] = s * g_tot + jnp.where(same_head, up, 0.0)
        for (d, rs, p, _, ps, _), sb, u in zip(probs, s_b, u_b):
            o_s[d, rs, ps] = (jnp.dot(qd_s[d, rs, ps], sb, preferred_element_type=F32)
                              + jnp.dot(qk_s[d, rs, p * 2 * CHUNK:(p + 1) * 2 * CHUNK], _block_diag(u, DV),
                                        preferred_element_type=F32))
        return carry

    lax.fori_loop(0, n_chunks, scan_step, 0)

    for h in range(DN_HEADS):
        hs = slice(h * DV, (h + 1) * DV)
        o = o_s[0, :, hs] + o_s[1, :, hs]
        y = o * lax.rsqrt(jnp.mean(o * o, axis=-1, keepdims=True) + EPS) * ogain_ref[...]
        dn_ref[:, hs] = (y * _silu(z_ref[:, hs].astype(F32))).astype(dn_ref.dtype)
    if write_state:
        for s in range(n_seq):
            for d in range(N_DIR):
                for p in range(n_pairs):
                    idx = (s * N_DIR + d) * n_pairs + p
                    sfin_ref[s, d, 2 * p] = st_s[idx, 0:DK, 0:DV]
                    sfin_ref[s, d, 2 * p + 1] = st_s[idx, DK:2 * DK, DV:2 * DV]


def _gdn_call(l, qkvd, z, ab, conv_w, alog16, dtb16, o_gain, state_dn, mix, new_state=None, *, latent):
    L = DEC_SEQ if latent else SEQ
    n_seq = 1 if latent else GDN_CTX_SEQS
    n_steps = (DEC_BATCH if latent else BATCH) // n_seq
    rows = n_seq * L
    base = N_CTX_TOK // rows if latent else 0
    seq = lambda w: pl.BlockSpec((rows, w), lambda b: (base + b, 0))
    in_specs = [
        seq(DN_CONV_CH), seq(DN_V), seq(N_GATE_COLS),
        pl.BlockSpec((None, DN_CONV, DN_CONV_CH), lambda b: (l, 0, 0)),
        pl.BlockSpec((None, 1, N_GATE_COLS), lambda b: (l, 0, 0)),
        pl.BlockSpec((None, 1, N_GATE_COLS), lambda b: (l, 0, 0)),
        pl.BlockSpec((None, 1, DV), lambda b: (l, 0, 0)),
    ]
    args = [qkvd, z, ab, conv_w, alog16, dtb16, o_gain]
    out_specs = [pl.BlockSpec((rows, DN_V), lambda b: (base + b, 1))]
    out_shape = [jax.ShapeDtypeStruct((N_TOK, ATT_Q + DN_V), BF16)]
    if latent:
        in_specs.append(pl.BlockSpec((n_seq, None, N_DIR, DN_HEADS, DK, DV), lambda b: (b, l, 0, 0, 0, 0)))
        args.append(state_dn)
    in_specs.append(pl.BlockSpec(memory_space=pl.ANY))
    args.append(mix)
    aliases = {len(args) - 1: 0}
    if not latent:
        out_specs.append(pl.BlockSpec((n_seq, None, N_DIR, DN_HEADS, DK, DV), lambda b: (b, l, 0, 0, 0, 0)))
        out_shape.append(jax.ShapeDtypeStruct((BATCH, DEPTH, N_DIR, DN_HEADS, DK, DV), F32))
        if new_state is not None:
            in_specs.append(pl.BlockSpec(memory_space=pl.ANY))
            args.append(new_state)
            aliases[len(args) - 1] = 1
    n_chunks = rows // CHUNK
    return pl.pallas_call(
        functools.partial(_gdn_kernel, seq_len=L, n_seq=n_seq, has_s0=latent, write_state=not latent,
                          n_alias=len(aliases)),
        grid=(n_steps,),
        in_specs=in_specs,
        out_specs=out_specs,
        out_shape=out_shape,
        scratch_shapes=[
            pltpu.VMEM((rows, DN_QK), F32), pltpu.VMEM((rows, DN_QK), F32), pltpu.VMEM((rows, DN_V), F32),
            pltpu.VMEM((rows, N_GATE_COLS), F32),
            pltpu.VMEM((N_DIR, rows, DN_V), F32),
            pltpu.VMEM((N_DIR, rows, DN_QK), BF16),
            pltpu.VMEM((N_DIR, rows, DN_QK), BF16),
            pltpu.VMEM((N_DIR, rows, DN_QK), BF16),
            pltpu.VMEM((N_DIR, rows, DN_HEADS * CHUNK), BF16),
            pltpu.VMEM((N_DIR, n_chunks, N_GATE_COLS), F32),
            pltpu.VMEM((N_DIR, rows, DN_V), F32),
            pltpu.VMEM((n_seq * N_DIR * (DN_HEADS // 2), 2 * DK, 2 * DV), F32),
            pltpu.VMEM((L + 16, LANES), F32),
        ],
        input_output_aliases=aliases,
        compiler_params=_cparams("arbitrary"),
        name="gdn_lat" if latent else "gdn_ctx",
    )(*args)


def _top2_of4(vals):
    m1 = jnp.maximum(jnp.maximum(vals[0], vals[1]), jnp.maximum(vals[2], vals[3]))
    i1 = jnp.where(vals[0] == m1, 0, jnp.where(vals[1] == m1, 1, jnp.where(vals[2] == m1, 2, 3)))
    rest = [jnp.where(i1 == j, -jnp.inf, vals[j]) for j in range(4)]
    m2 = jnp.maximum(jnp.maximum(rest[0], rest[1]), jnp.maximum(rest[2], rest[3]))
    i2 = jnp.where(rest[0] == m2, 0, jnp.where(rest[1] == m2, 1, jnp.where(rest[2] == m2, 2, 3)))
    return m1, i1, m2, i2


def _outproj_kernel(x_ref, mix_ref, mod_ref, gain_ref, wo_ref, rwt_ref, rb_ref, ut_ref,
                    x1_ref, h2_ref, ri_ref, rg_ref, cnt_ref, carry_s):
    i = pl.program_id(0)

    @pl.when(i == 0)
    def _():
        carry_s[...] = jnp.zeros_like(carry_s)

    gate1 = mod_ref[:, 2 * D_MODEL:3 * D_MODEL]
    shift2 = mod_ref[:, 3 * D_MODEL:4 * D_MODEL]
    scale2 = mod_ref[:, 4 * D_MODEL:5 * D_MODEL]
    x1 = x_ref[...] + gate1 * jnp.dot(mix_ref[...], wo_ref[...], preferred_element_type=F32)
    x1_ref[...] = x1
    h2 = _adaln(x1, gain_ref[...], shift2, scale2)
    h2_ref[...] = h2

    h_hi, h_lo = _split_bf16(h2)
    r_hi, r_lo = _split_bf16(rwt_ref[...])
    nt = (((1,), (1,)), ((), ()))
    logits = (lax.dot_general(r_hi, h_hi, nt, preferred_element_type=F32)
              + lax.dot_general(r_hi, h_lo, nt, preferred_element_type=F32)
              + lax.dot_general(r_lo, h_hi, nt, preferred_element_type=F32))
    scores = _sigmoid(logits)
    biased = scores + rb_ref[...]
    rows = [biased[e:e + 1, :] for e in range(N_EXPERTS)]
    per_group = [_top2_of4(rows[g * 4:(g + 1) * 4]) for g in range(N_GROUPS)]
    gs = [pg[0] + pg[2] for pg in per_group]
    _, gsel, _, _ = _top2_of4(gs)
    pick = lambda k: jnp.where(gsel == 0, per_group[0][k], jnp.where(gsel == 1, per_group[1][k],
                               jnp.where(gsel == 2, per_group[2][k], per_group[3][k])))
    e1 = gsel * EXPERTS_PER_GROUP + pick(1)
    e2 = gsel * EXPERTS_PER_GROUP + pick(3)
    erow = lax.broadcasted_iota(jnp.int32, scores.shape, 0)
    oh1 = erow == e1
    oh2 = erow == e2
    s1 = jnp.sum(jnp.where(oh1, scores, 0.0), axis=0, keepdims=True)
    s2 = jnp.sum(jnp.where(oh2, scores, 0.0), axis=0, keepdims=True)
    tot = s1 + s2
    grow = lax.broadcasted_iota(jnp.int32, rg_ref.shape, 0)
    rg_ref[...] = jnp.where(grow == 0, s1 / tot, s2 / tot)

    sel = (oh1 | oh2).astype(BF16)
    before = jnp.dot(sel, ut_ref[...], preferred_element_type=F32) + carry_s[...]
    r1 = jnp.sum(jnp.where(oh1, before, 0.0), axis=0, keepdims=True)
    r2 = jnp.sum(jnp.where(oh2, before, 0.0), axis=0, keepdims=True)
    carry_s[...] = carry_s[...] + jnp.sum(sel.astype(F32), axis=1, keepdims=True)
    cnt_ref[...] = carry_s[...].astype(jnp.int32)
    irow = lax.broadcasted_iota(jnp.int32, ri_ref.shape, 0)
    ri_ref[...] = jnp.where(irow == 0, e1, jnp.where(irow == 1, e2, jnp.where(
        irow == 2, r1.astype(jnp.int32), r2.astype(jnp.int32))))


def _outproj_call(l, x, mix, mod, gain, w_out, router_wt, router_b, ut):
    tok = lambda w: pl.BlockSpec((TOK_TILE, w), lambda i: (i, 0))
    per_tok = lambda r: pl.BlockSpec((r, TOK_TILE), lambda i: (0, i))
    full = lambda a: pl.BlockSpec(a.shape, lambda i: (0,) * a.ndim)
    return pl.pallas_call(
        _outproj_kernel,
        grid=(N_TOK // TOK_TILE,),
        in_specs=[
            tok(D_MODEL), tok(ATT_Q + DN_V),
            pl.BlockSpec((None, None, 1, 6 * D_MODEL), lambda i: (l, _mod_row(i), 0, 0)),
            pl.BlockSpec((None, 1, D_MODEL), lambda i: (l, 0, 0)),
            pl.BlockSpec((None, ATT_Q + DN_V, D_MODEL), lambda i: (l, 0, 0)),
            full(router_wt), full(router_b), full(ut),
        ],
        out_specs=[tok(D_MODEL), tok(D_MODEL), per_tok(4), per_tok(2), pl.BlockSpec((N_EXPERTS, 1), lambda i: (0, 0))],
        out_shape=[
            jax.ShapeDtypeStruct((N_TOK, D_MODEL), F32),
            jax.ShapeDtypeStruct((N_TOK, D_MODEL), F32),
            jax.ShapeDtypeStruct((4, N_TOK), jnp.int32),
            jax.ShapeDtypeStruct((2, N_TOK), F32),
            jax.ShapeDtypeStruct((N_EXPERTS, 1), jnp.int32),
        ],
        scratch_shapes=[pltpu.VMEM((N_EXPERTS, 1), F32)],
        compiler_params=_cparams("arbitrary"),
        name="outproj_router",
    )(x, mix, mod, gain, w_out, router_wt, router_b, ut)


def _row_dma_loops(row_copy):
    def start(g, c):
        for j in range(SUBLANES):
            row_copy(g, j, 0).start()
            row_copy(g, j, 1).start()
        return c

    def wait(g, c):
        for j in range(SUBLANES):
            row_copy(g, j, 0).wait()
            row_copy(g, j, 1).wait()
        return c

    lax.fori_loop(0, TOK_TILE // SUBLANES, start, 0)
    lax.fori_loop(0, TOK_TILE // SUBLANES, wait, 0)


def _dispatch_kernel(pos_ref, h2_ref, buf_in_ref, sorted_ref, sem):
    del buf_in_ref

    def row_copy(g, j, k):
        p = pos_ref[0, k * TOK_TILE + g * SUBLANES + j]
        return pltpu.make_async_copy(h2_ref.at[g, pl.ds(j, 1), :], sorted_ref.at[pl.ds(p, 1), :], sem)

    _row_dma_loops(row_copy)


def _dispatch_call(pos_tiles, h2, zeros_sorted):
    return pl.pallas_call(
        _dispatch_kernel,
        grid=(N_TOK // TOK_TILE,),
        in_specs=[
            pl.BlockSpec((None, 1, TOP_K * TOK_TILE), lambda i: (i, 0, 0), memory_space=pltpu.SMEM),
            pl.BlockSpec((TOK_TILE // SUBLANES, SUBLANES, D_MODEL), lambda i: (i, 0, 0)),
            pl.BlockSpec(memory_space=pl.ANY),
        ],
        out_specs=pl.BlockSpec(memory_space=pl.ANY),
        out_shape=jax.ShapeDtypeStruct((N_SORTED, D_MODEL), F32),
        scratch_shapes=[pltpu.SemaphoreType.DMA(())],
        input_output_aliases={2: 0},
        compiler_params=_cparams("arbitrary"),
        name="moe_dispatch",
    )(pos_tiles, h2, zeros_sorted)


def _experts_kernel(te_ref, nu_ref, x_ref, wg_ref, wu_ref, wd_ref, y_ref, wg_s, wu_s, wd_s):
    j = pl.program_id(0)
    used = j < nu_ref[0]

    @pl.when((j == 0) | (te_ref[j] != te_ref[jnp.maximum(j - 1, 0)]))
    def _():
        wg_s[...] = wg_ref[...].astype(BF16)
        wu_s[...] = wu_ref[...].astype(BF16)
        wd_s[...] = wd_ref[...].astype(BF16)

    @pl.when(used)
    def _():
        x = x_ref[...].astype(BF16)
        hid = _silu(jnp.dot(x, wg_s[...], preferred_element_type=F32)) * jnp.dot(x, wu_s[...], preferred_element_type=F32)
        y_ref[...] = jnp.dot(hid.astype(BF16), wd_s[...], preferred_element_type=F32)

    @pl.when(jnp.logical_not(used))
    def _():
        y_ref[...] = jnp.zeros_like(y_ref)


def _experts_call(l, tile_expert, n_used, xs, w_gate, w_up, w_down):
    row = lambda j, te, nu: (jnp.maximum(jnp.minimum(j, nu[0] - 1), 0), 0)
    grid_spec = pltpu.PrefetchScalarGridSpec(
        num_scalar_prefetch=2,
        grid=(N_EXP_TILES,),
        in_specs=[
            pl.BlockSpec((EXP_TILE, D_MODEL), row),
            pl.BlockSpec((None, None, D_MODEL, D_FF), lambda j, te, nu: (l, te[j], 0, 0)),
            pl.BlockSpec((None, None, D_MODEL, D_FF), lambda j, te, nu: (l, te[j], 0, 0)),
            pl.BlockSpec((None, None, D_FF, D_MODEL), lambda j, te, nu: (l, te[j], 0, 0)),
        ],
        out_specs=pl.BlockSpec((EXP_TILE, D_MODEL), lambda j, te, nu: (j, 0)),
        scratch_shapes=[pltpu.VMEM((D_MODEL, D_FF), BF16), pltpu.VMEM((D_MODEL, D_FF), BF16),
                        pltpu.VMEM((D_FF, D_MODEL), BF16)],
    )
    return pl.pallas_call(
        _experts_kernel,
        grid_spec=grid_spec,
        out_shape=jax.ShapeDtypeStruct((N_SORTED, D_MODEL), F32),
        compiler_params=_cparams("arbitrary"),
        name="moe_experts",
    )(tile_expert, n_used, xs, w_gate, w_up, w_down)


def _combine_kernel(pos_ref, x1_ref, rg_ref, mod_ref, y_ref, out_ref, buf_s, sem):
    def row_copy(g, j, k):
        p = pos_ref[0, k * TOK_TILE + g * SUBLANES + j]
        return pltpu.make_async_copy(y_ref.at[pl.ds(p, 1), :], buf_s.at[k, g, pl.ds(j, 1), :], sem)

    _row_dma_loops(row_copy)
    gate2 = mod_ref[:, 5 * D_MODEL:6 * D_MODEL]
    moe = (buf_s[0].reshape(TOK_TILE, D_MODEL) * rg_ref[:, 0:1] + buf_s[1].reshape(TOK_TILE, D_MODEL) * rg_ref[:, 1:2])
    out_ref[...] = x1_ref[...] + gate2 * moe


def _combine_call(l, pos_tiles, x1, rg, mod, y_sorted):
    tok = lambda w: pl.BlockSpec((TOK_TILE, w), lambda i: (i, 0))
    return pl.pallas_call(
        _combine_kernel,
        grid=(N_TOK // TOK_TILE,),
        in_specs=[
            pl.BlockSpec((None, 1, TOP_K * TOK_TILE), lambda i: (i, 0, 0), memory_space=pltpu.SMEM),
            tok(D_MODEL), tok(2),
            pl.BlockSpec((None, None, 1, 6 * D_MODEL), lambda i: (l, _mod_row(i), 0, 0)),
            pl.BlockSpec(memory_space=pl.ANY),
        ],
        out_specs=tok(D_MODEL),
        out_shape=jax.ShapeDtypeStruct((N_TOK, D_MODEL), F32),
        scratch_shapes=[pltpu.VMEM((TOP_K, TOK_TILE // SUBLANES, SUBLANES, D_MODEL), F32), pltpu.SemaphoreType.DMA(())],
        compiler_params=_cparams("arbitrary"),
        name="moe_combine",
    )(pos_tiles, x1, rg, mod, y_sorted)


def _rope_tables():
    pos = jnp.arange(DEC_SEQ)
    r = (pos // GRID_W).astype(F32)
    c = (pos % GRID_W).astype(F32)
    inv = ROPE_BASE ** (-jnp.arange(ROPE_PAIRS, dtype=F32) / ROPE_PAIRS)
    ar, ac = r[:, None] * inv, c[:, None] * inv
    cos = jnp.concatenate([jnp.cos(ar), jnp.cos(ar), jnp.cos(ac), jnp.cos(ac)], axis=-1)
    sin = jnp.concatenate([-jnp.sin(ar), jnp.sin(ar), -jnp.sin(ac), jnp.sin(ac)], axis=-1)
    return jnp.tile(cos, (1, N_HEADS)), jnp.tile(sin, (1, N_HEADS))


def _routing_layout(ri, counts):
    counts = counts.reshape(N_EXPERTS)
    padded = ((counts + EXP_TILE - 1) // EXP_TILE) * EXP_TILE
    ends = jnp.cumsum(padded)
    offs = ends - padded
    eids = jnp.arange(N_EXPERTS, dtype=jnp.int32)[:, None, None]
    pos = jnp.sum(jnp.where(ri[None, 0:2] == eids, offs[:, None, None], 0), axis=0) + ri[2:4]
    n_used = (ends[-1] // EXP_TILE).astype(jnp.int32)
    tile_start = jnp.arange(N_EXP_TILES, dtype=jnp.int32) * EXP_TILE
    tile_expert = jnp.sum((tile_start[:, None] >= ends[None, :]).astype(jnp.int32), axis=1)
    last = jnp.sum((jnp.maximum(ends[-1] - EXP_TILE, 0) >= ends).astype(jnp.int32))
    tile_expert = jnp.minimum(jnp.where(tile_start < ends[-1], tile_expert, last), N_EXPERTS - 1).astype(jnp.int32)
    pos_tiles = pos.reshape(TOP_K, N_TOK // TOK_TILE, TOK_TILE).transpose(1, 0, 2).reshape(
        N_TOK // TOK_TILE, 1, TOP_K * TOK_TILE).astype(jnp.int32)
    return pos_tiles, tile_expert, n_used.reshape(1)


def kernel(x_prompt, x_sample, cache_k, cache_v, state_dn, c, c_ctx, w_ada, b_ada, norm_attn, norm_ffn,
           w_in, conv_w, a_log, dt_bias, q_norm, k_norm, sink, o_norm, w_out, router_w, router_bias,
           w_gate, w_up, w_down):
    x = jnp.concatenate([x_prompt.reshape(N_CTX_TOK, D_MODEL), x_sample.reshape(N_LAT_TOK, D_MODEL)], axis=0)
    cond = jnp.concatenate([c_ctx[None, :], c, jnp.zeros((N_COND - 1 - DEC_BATCH, D_MODEL), F32)], axis=0)
    mod = _modulation_call(cond, w_ada, b_ada).reshape(DEPTH, N_COND, 1, 6 * D_MODEL)

    w_in_b = w_in.astype(BF16)
    w_out_b = w_out.astype(BF16)
    seg = jnp.arange(ATT_Q) // HEAD_DIM
    bd = jnp.where(seg[:, None] == seg[None, :], 1.0 / HEAD_DIM, 0.0).astype(BF16)
    qg = jnp.tile(q_norm, (1, N_HEADS)).reshape(DEPTH, 1, ATT_Q)
    kg = jnp.tile(k_norm, (1, N_KV)).reshape(DEPTH, 1, ATT_KV)
    cos, sin = _rope_tables()
    gain1 = norm_attn.reshape(DEPTH, 1, D_MODEL)
    gain2 = norm_ffn.reshape(DEPTH, 1, D_MODEL)
    pad8 = lambda a: jnp.pad(a.reshape(DEPTH, 1, N_DIR * DN_HEADS), ((0, 0), (0, 0), (0, N_GATE_COLS - N_DIR * DN_HEADS)))
    alog16, dtb16 = pad8(a_log), pad8(dt_bias)
    o_gain = o_norm.reshape(DEPTH, 1, DV)
    ck = cache_k.reshape(DEC_BATCH, DEPTH, PAST_LEN, ATT_KV)
    cv = cache_v.reshape(DEC_BATCH, DEPTH, PAST_LEN, ATT_KV)
    router_wt = router_w.T
    rb = router_bias.reshape(N_EXPERTS, 1)
    tri = jnp.arange(TOK_TILE)
    ut = (tri[:, None] < tri[None, :]).astype(BF16)
    sorted_buf = jnp.zeros((N_SORTED, D_MODEL), F32)

    k_list, v_list, new_state = [], [], None
    for l in range(DEPTH):
        q, kv, qkvd, z, ab = _inproj_call(l, x, mod, gain1, w_in_b, bd, qg, kg, cos, sin)
        k_list.append(kv[:N_CTX_TOK, 0:ATT_KV].reshape(BATCH, SEQ, N_KV, HEAD_DIM))
        v_list.append(kv[:N_CTX_TOK, ATT_KV:].reshape(BATCH, SEQ, N_KV, HEAD_DIM))
        mix = _attn_ctx_call(sink[l], q, kv)
        mix = _attn_lat_call(l, sink[l], q, kv, ck, cv, mix)
        mix, new_state = _gdn_call(l, qkvd, z, ab, conv_w, alog16, dtb16, o_gain, state_dn, mix, new_state, latent=False)
        (mix,) = _gdn_call(l, qkvd, z, ab, conv_w, alog16, dtb16, o_gain, state_dn, mix, latent=True)
        x1, h2, ri, rg, counts = _outproj_call(l, x, mix, mod, gain2, w_out_b, router_wt, rb, ut)
        pos_tiles, tile_expert, n_used = _routing_layout(ri, counts)
        xs_sorted = _dispatch_call(pos_tiles, h2.reshape(N_TOK // SUBLANES, SUBLANES, D_MODEL), sorted_buf)
        sorted_buf = xs_sorted
        y_sorted = _experts_call(l, tile_expert, n_used, xs_sorted, w_gate, w_up, w_down)
        x = _combine_call(l, pos_tiles, x1, rg.T, mod, y_sorted)

    y_prompt = x[:N_CTX_TOK].reshape(BATCH, SEQ, D_MODEL)
    y_sample = x[N_CTX_TOK:].reshape(DEC_BATCH, DEC_SEQ, D_MODEL)
    return (y_prompt, y_sample, jnp.stack(k_list, axis=1), jnp.stack(v_list, axis=1), new_state)
```

```python
import functools

import jax
import jax.numpy as jnp
from jax import lax
from jax.experimental import pallas as pl
from jax.experimental.pallas import tpu as pltpu

D_MODEL = 1024
BATCH = 32
SEQ = 256
DEPTH = 4
DEC_BATCH = 4
DEC_SEQ = 1024
PAST_LEN = 256
GRID_W = 64
N_HEADS = 8
N_KV = 2
HEAD_DIM = 64
WINDOW = 128
BLOCK = 128
ATT_SCALE = HEAD_DIM ** -0.5
ROPE_BASE = 10000.0
ROPE_PAIRS = HEAD_DIM // 4
DN_HEADS = 4
DK = 128
DV = 128
DN_CONV = 5
CHUNK = 64
N_DIR = 2
ATT_Q = N_HEADS * HEAD_DIM
ATT_KV = N_KV * HEAD_DIM
DN_QK = DN_HEADS * DK
DN_V = DN_HEADS * DV
DN_CONV_CH = 2 * DN_QK + DN_V
N_GATE_COLS = 2 * N_DIR * DN_HEADS
IN_DIM = ATT_Q + 2 * ATT_KV + DN_CONV_CH + DN_V + N_GATE_COLS
N_EXPERTS = 16
N_GROUPS = 4
EXPERTS_PER_GROUP = 4
TOP_K = 2
D_FF = 512
EPS = 1e-6

N_CTX_TOK = BATCH * SEQ
N_LAT_TOK = DEC_BATCH * DEC_SEQ
N_TOK = N_CTX_TOK + N_LAT_TOK
N_COND = 8

LANES = 128
SUBLANES = 8
TOK_TILE = 512
EXP_TILE = 512
N_EXP_TILES = (N_TOK * TOP_K) // EXP_TILE + N_EXPERTS
N_SORTED = N_EXP_TILES * EXP_TILE
VMEM_LIMIT = 56 * 1024 * 1024
ATT_CTX_SEQS = 4
LAT_Q_ROWS = 256

F32 = jnp.float32
BF16 = jnp.bfloat16


def _cparams(*sem):
    return pltpu.CompilerParams(dimension_semantics=sem, vmem_limit_bytes=VMEM_LIMIT)


def _dot(a, b):
    return jnp.dot(a.astype(BF16), b.astype(BF16), preferred_element_type=F32)


def _dot_nt(a, b):
    return lax.dot_general(a.astype(BF16), b.astype(BF16), (((1,), (1,)), ((), ())), preferred_element_type=F32)


def _dot_split(mat01, x):
    hi = x.astype(BF16)
    lo = (x - hi.astype(F32)).astype(BF16)
    return (jnp.dot(mat01, hi, preferred_element_type=F32) + jnp.dot(mat01, lo, preferred_element_type=F32))


def _sigmoid(x):
    return 1.0 / (1.0 + jnp.exp(-x))


def _silu(x):
    return x * _sigmoid(x)


def _softplus(x):
    return jnp.maximum(x, 0.0) + jnp.log(1.0 + jnp.exp(-jnp.abs(x)))


MOD_TN = 1536


def _mod_kernel(cond_ref, w_ref, b_ref, o_ref):
    c = _silu(cond_ref[...])
    o_ref[...] = _dot(c, w_ref[...]) + b_ref[...]


def _modulation_call(cond, w_ada, b_ada):
    n_col = 6 * D_MODEL
    return pl.pallas_call(
        _mod_kernel,
        grid=(DEPTH, n_col // MOD_TN),
        in_specs=[
            pl.BlockSpec((N_COND, D_MODEL), lambda l, j: (0, 0)),
            pl.BlockSpec((None, D_MODEL, MOD_TN), lambda l, j: (l, 0, j)),
            pl.BlockSpec((None, 1, MOD_TN), lambda l, j: (l, 0, j)),
        ],
        out_specs=pl.BlockSpec((None, N_COND, MOD_TN), lambda l, j: (l, 0, j)),
        out_shape=jax.ShapeDtypeStruct((DEPTH, N_COND, n_col), F32),
        compiler_params=_cparams("arbitrary", "arbitrary"),
        name="modulation",
    )(cond, w_ada, b_ada.reshape(DEPTH, 1, n_col))


def _mod_row(i):
    n_ctx = N_CTX_TOK // TOK_TILE
    per_seq = DEC_SEQ // TOK_TILE
    return jnp.where(i < n_ctx, 0, 1 + (jnp.maximum(i - n_ctx, 0)) // per_seq)


def _adaln(x, gain, shift, scale):
    ms = jnp.mean(x * x, axis=-1, keepdims=True)
    return (x * lax.rsqrt(ms + EPS) * gain) * (1.0 + scale) + shift


def _seg_rms(x, bd, gain):
    x2 = x * x
    hi = x2.astype(BF16)
    lo = (x2 - hi.astype(F32)).astype(BF16)
    ms = jnp.dot(hi, bd, preferred_element_type=F32) + jnp.dot(lo, bd, preferred_element_type=F32)
    return x * lax.rsqrt(ms + EPS) * gain


def _rope(x, cos, sin_signed):
    w = x.shape[-1]
    lane = lax.broadcasted_iota(jnp.int32, x.shape, 1)
    first = (lane % (2 * ROPE_PAIRS)) < ROPE_PAIRS
    partner = jnp.where(first, pltpu.roll(x, w - ROPE_PAIRS, axis=1), pltpu.roll(x, ROPE_PAIRS, axis=1))
    return x * cos + partner * sin_signed


def _inproj_kernel(x_ref, mod_ref, gain_ref, w_ref, bd_ref, qg_ref, kg_ref, cos_ref, sin_ref,
                   q_ref, kv_ref, qkvd_ref, z_ref, ab_ref):
    i = pl.program_id(0)
    is_lat = i >= N_CTX_TOK // TOK_TILE
    shift = mod_ref[:, 0:D_MODEL]
    scale = mod_ref[:, D_MODEL:2 * D_MODEL]
    h = _adaln(x_ref[...], gain_ref[...], shift, scale).astype(BF16)

    qa = jnp.dot(h, w_ref[:, 0:ATT_Q], preferred_element_type=F32)
    qn = _seg_rms(qa, bd_ref[...], qg_ref[...])
    ka = jnp.dot(h, w_ref[:, ATT_Q:ATT_Q + ATT_KV], preferred_element_type=F32)
    kn = _seg_rms(ka, bd_ref[0:ATT_KV, 0:ATT_KV], kg_ref[...])

    @pl.when(is_lat)
    def _():
        q_ref[...] = (_rope(qn, cos_ref[...], sin_ref[...]) * ATT_SCALE).astype(BF16)
        kv_ref[:, 0:ATT_KV] = _rope(kn, cos_ref[:, 0:ATT_KV], sin_ref[:, 0:ATT_KV])

    @pl.when(jnp.logical_not(is_lat))
    def _():
        q_ref[...] = (qn * ATT_SCALE).astype(BF16)
        kv_ref[:, 0:ATT_KV] = kn

    c0 = ATT_Q + ATT_KV
    kv_ref[:, ATT_KV:2 * ATT_KV] = jnp.dot(h, w_ref[:, c0:c0 + ATT_KV], preferred_element_type=F32)
    c0 += ATT_KV
    for j in range(DN_CONV_CH // 512):
        qkvd_ref[:, j * 512:(j + 1) * 512] = jnp.dot(
            h, w_ref[:, c0 + j * 512:c0 + (j + 1) * 512], preferred_element_type=F32).astype(BF16)
    c0 += DN_CONV_CH
    z_ref[...] = jnp.dot(h, w_ref[:, c0:c0 + DN_V], preferred_element_type=F32).astype(BF16)
    c0 += DN_V
    ab_ref[...] = jnp.dot(h, w_ref[:, c0:c0 + N_GATE_COLS], preferred_element_type=F32)


def _inproj_call(l, x, mod, gain, w_in, bd, qg, kg, cos, sin):
    n_ctx = N_CTX_TOK // TOK_TILE
    per_seq = DEC_SEQ // TOK_TILE

    def pos_map(i):
        return (jnp.maximum(i - n_ctx, 0) % per_seq, 0)

    tok = lambda w: pl.BlockSpec((TOK_TILE, w), lambda i: (i, 0))
    full = lambda a: pl.BlockSpec(a.shape, lambda i: (0,) * a.ndim)
    return pl.pallas_call(
        _inproj_kernel,
        grid=(N_TOK // TOK_TILE,),
        in_specs=[
            tok(D_MODEL),
            pl.BlockSpec((None, None, 1, 6 * D_MODEL), lambda i: (l, _mod_row(i), 0, 0)),
            pl.BlockSpec((None, 1, D_MODEL), lambda i: (l, 0, 0)),
            pl.BlockSpec((None, D_MODEL, IN_DIM), lambda i: (l, 0, 0)),
            full(bd),
            pl.BlockSpec((None, 1, ATT_Q), lambda i: (l, 0, 0)),
            pl.BlockSpec((None, 1, ATT_KV), lambda i: (l, 0, 0)),
            pl.BlockSpec((TOK_TILE, ATT_Q), pos_map),
            pl.BlockSpec((TOK_TILE, ATT_Q), pos_map),
        ],
        out_specs=[tok(ATT_Q), tok(2 * ATT_KV), tok(DN_CONV_CH), tok(DN_V), tok(N_GATE_COLS)],
        out_shape=[
            jax.ShapeDtypeStruct((N_TOK, ATT_Q), BF16),
            jax.ShapeDtypeStruct((N_TOK, 2 * ATT_KV), F32),
            jax.ShapeDtypeStruct((N_TOK, DN_CONV_CH), BF16),
            jax.ShapeDtypeStruct((N_TOK, DN_V), BF16),
            jax.ShapeDtypeStruct((N_TOK, N_GATE_COLS), F32),
        ],
        compiler_params=_cparams("arbitrary"),
        name="inproj",
    )(x, mod, gain, w_in, bd, qg, kg, cos, sin)


def _dup_half(x, g):
    lane = lax.broadcasted_iota(jnp.int32, x.shape, 1)
    lo = lane < HEAD_DIM
    xr = pltpu.roll(x, HEAD_DIM, axis=1)
    return jnp.where(lo, x, xr) if g == 0 else jnp.where(lo, xr, x)


def _attend(sink_ref, q_ref, o_ref, key_sets):
    rows = q_ref.shape[0]
    lane = lax.broadcasted_iota(jnp.int32, (rows, LANES), 1)
    lo = lane < HEAD_DIM
    for g in range(N_KV):
        ks = [(_dup_half(k, g).astype(BF16), _dup_half(v, g).astype(BF16), valid) for k, v, valid in key_sets]
        for jj in range(2):
            j = g * 2 + jj
            qp = q_ref[:, j * LANES:(j + 1) * LANES]
            outs = []
            for e in range(2):
                qm = jnp.where(lo if e == 0 else jnp.logical_not(lo), qp, jnp.zeros_like(qp))
                sk = sink_ref[2 * j + e]
                scores = []
                m = jnp.full((rows, 1), sk, F32)
                for k, _, valid in ks:
                    s = _dot_nt(qm, k)
                    if valid is not None:
                        s = jnp.where(valid, s, -jnp.inf)
                    scores.append(s)
                    m = jnp.maximum(m, jnp.max(s, axis=-1, keepdims=True))
                den = jnp.exp(sk - m)
                acc = jnp.zeros((rows, LANES), F32)
                for s, (_, v, _) in zip(scores, ks):
                    p = jnp.exp(s - m)
                    den = den + jnp.sum(p, axis=-1, keepdims=True)
                    acc = acc + _dot(p, v)
                outs.append(acc / den)
            o_ref[:, j * LANES:(j + 1) * LANES] = jnp.where(lo, outs[0], outs[1]).astype(o_ref.dtype)


def _attn_ctx_kernel(sink_ref, q_ref, kv_ref, o_ref):
    for s in range(ATT_CTX_SEQS):
        rows = pl.ds(s * SEQ, SEQ)
        _attend(sink_ref, q_ref.at[rows], o_ref.at[rows],
                [(kv_ref[s * SEQ:(s + 1) * SEQ, 0:ATT_KV], kv_ref[s * SEQ:(s + 1) * SEQ, ATT_KV:2 * ATT_KV], None)])


def _attn_lat_kernel(sink_ref, q_ref, kv_ref, ck_ref, cv_ref, mix_in_ref, o_ref):
    del mix_in_ref
    n = pl.program_id(1)
    span = LAT_Q_ROWS + 2 * WINDOW
    start = pl.multiple_of(jnp.clip(n * LAT_Q_ROWS - WINDOW, 0, DEC_SEQ - span), BLOCK)
    kw = kv_ref[pl.ds(start, span), 0:ATT_KV]
    vw = kv_ref[pl.ds(start, span), ATT_KV:2 * ATT_KV]
    qpos = n * LAT_Q_ROWS + lax.broadcasted_iota(jnp.int32, (LAT_Q_ROWS, span), 0)
    kpos = start + lax.broadcasted_iota(jnp.int32, (LAT_Q_ROWS, span), 1)
    valid = jnp.abs(kpos - qpos) <= WINDOW
    _attend(sink_ref, q_ref, o_ref, [(kw, vw, valid), (ck_ref[...], cv_ref[...], None)])


def _attn_ctx_call(sink_l, q, kv):
    return pl.pallas_call(
        _attn_ctx_kernel,
        grid=(BATCH // ATT_CTX_SEQS,),
        in_specs=[
            pl.BlockSpec(memory_space=pltpu.SMEM),
            pl.BlockSpec((ATT_CTX_SEQS * SEQ, ATT_Q), lambda b: (b, 0)),
            pl.BlockSpec((ATT_CTX_SEQS * SEQ, 2 * ATT_KV), lambda b: (b, 0)),
        ],
        out_specs=pl.BlockSpec((ATT_CTX_SEQS * SEQ, ATT_Q), lambda b: (b, 0)),
        out_shape=jax.ShapeDtypeStruct((N_TOK, ATT_Q + DN_V), BF16),
        compiler_params=_cparams("arbitrary"),
        name="attn_ctx",
    )(sink_l, q, kv)


def _attn_lat_call(l, sink_l, q, kv, cache_k, cache_v, mix):
    nb = DEC_SEQ // LAT_Q_ROWS
    q0 = N_CTX_TOK // LAT_Q_ROWS
    s0 = N_CTX_TOK // DEC_SEQ
    return pl.pallas_call(
        _attn_lat_kernel,
        grid=(DEC_BATCH, nb),
        in_specs=[
            pl.BlockSpec(memory_space=pltpu.SMEM),
            pl.BlockSpec((LAT_Q_ROWS, ATT_Q), lambda b, n: (q0 + b * nb + n, 0)),
            pl.BlockSpec((DEC_SEQ, 2 * ATT_KV), lambda b, n: (s0 + b, 0)),
            pl.BlockSpec((None, None, PAST_LEN, ATT_KV), lambda b, n: (b, l, 0, 0)),
            pl.BlockSpec((None, None, PAST_LEN, ATT_KV), lambda b, n: (b, l, 0, 0)),
            pl.BlockSpec(memory_space=pl.ANY),
        ],
        out_specs=pl.BlockSpec((LAT_Q_ROWS, ATT_Q), lambda b, n: (q0 + b * nb + n, 0)),
        out_shape=jax.ShapeDtypeStruct((N_TOK, ATT_Q + DN_V), BF16),
        input_output_aliases={5: 0},
        compiler_params=_cparams("arbitrary", "arbitrary"),
        name="attn_lat",
    )(sink_l, q, kv, cache_k, cache_v, mix)


INV_PASSES = 1
GDN_UNROLL = 4
GDN_CTX_SEQS = 4


def _split_bf16(x):
    hi = x.astype(BF16)
    return hi, (x - hi.astype(F32)).astype(BF16)


def _block_diag(x, width):
    n = x.shape[1] // width
    blk = lax.broadcasted_iota(jnp.int32, x.shape, 1) // width
    zero = jnp.zeros_like(x)
    return jnp.concatenate([jnp.where(blk == h, x, zero) for h in range(n)], axis=0)


def _mm_heads(ts, xs, width):
    if INV_PASSES == 1:
        bds = [_block_diag(x.astype(BF16), width) for x in xs]
        return [jnp.dot(t.astype(BF16), bd, preferred_element_type=F32) for t, bd in zip(ts, bds)]
    tsp = [_split_bf16(t) for t in ts]
    xsp = [_split_bf16(x) for x in xs]
    bd_hi = [_block_diag(x_hi, width) for x_hi, _ in xsp]
    bd_lo = [_block_diag(x_lo, width) for _, x_lo in xsp]
    return [jnp.dot(t_hi, bh, preferred_element_type=F32) + jnp.dot(t_lo, bh, preferred_element_type=F32)
            + jnp.dot(t_hi, bl, preferred_element_type=F32) for (t_hi, t_lo), bh, bl in zip(tsp, bd_hi, bd_lo)]


def _unit_tri_inverse(mats, row, col):
    eye = (row == col).astype(F32)
    a8 = [jnp.where((row // 8) == (col // 8), a, 0.0) for a in mats]
    a8_2 = _mm_heads(a8, a8, CHUNK)
    a8_4 = _mm_heads(a8_2, a8_2, CHUNK)
    ts = [eye - a for a in a8]
    ts = [t + p for t, p in zip(ts, _mm_heads(ts, a8_2, CHUNK))]
    ts = [t + p for t, p in zip(ts, _mm_heads(ts, a8_4, CHUNK))]
    b = 8
    while b < CHUNK:
        level = ((row // (2 * b)) == (col // (2 * b))) & ((row // b) != (col // b))
        off = [jnp.where(level, a, 0.0) for a in mats]
        ts = [t - p for t, p in zip(ts, _mm_heads(_mm_heads(ts, off, CHUNK), ts, CHUNK))]
        b *= 2
    return ts


def _bcast_cols(x, first, width):
    rows = x.shape[0]
    if width == LANES:
        return jnp.concatenate([jnp.broadcast_to(x[:, first + h:first + h + 1], (rows, width)) for h in range(DN_HEADS)], axis=1)
    blk = lax.broadcasted_iota(jnp.int32, (rows, DN_HEADS * width), 1) // width
    out = jnp.broadcast_to(x[:, first:first + 1], (rows, DN_HEADS * width))
    for h in range(1, DN_HEADS):
        out = jnp.where(blk == h, jnp.broadcast_to(x[:, first + h:first + h + 1], (rows, DN_HEADS * width)), out)
    return out


def _gdn_kernel(*refs, seq_len, n_seq, has_s0, write_state, n_alias):
    qkvd_ref, z_ref, ab_ref, convw_ref, alog_ref, dtb_ref, ogain_ref = refs[:7]
    pos = 7
    s0_ref = None
    if has_s0:
        s0_ref = refs[pos]
        pos += 1
    pos += n_alias
    dn_ref = refs[pos]
    pos += 1
    sfin_ref = None
    if write_state:
        sfin_ref = refs[pos]
        pos += 1
    qn_s, kn_s, vn_s, gb_s, u0_s, w_s, qd_s, kd_s, qk_s, gt_s, o_s, st_s, xpad_s = refs[pos:]
    L = seq_len
    n_chunks = L // CHUNK
    n_pairs = DN_HEADS // 2
    pair_w = 2 * DK
    pad = (DN_CONV - 1) // 2

    halo = 8

    def finish(y, kind):
        y = _silu(y)
        if kind == 0:
            return y * lax.rsqrt(jnp.sum(y * y, axis=-1, keepdims=True) + EPS) * (DK ** -0.5)
        if kind == 1:
            return y * lax.rsqrt(jnp.sum(y * y, axis=-1, keepdims=True) + EPS)
        return y

    xpad_s[0:halo, :] = jnp.zeros((halo, LANES), F32)
    xpad_s[halo + L:2 * halo + L, :] = jnp.zeros((halo, LANES), F32)
    for kind, dst in enumerate((qn_s, kn_s, vn_s)):
        def conv_head(head, carry, kind=kind, dst=dst):
            cs = pl.ds(pl.multiple_of((kind * DN_HEADS + head) * LANES, LANES), LANES)
            hs = pl.ds(pl.multiple_of(head * DK, DK), DK)
            for s in range(n_seq):
                r0 = s * L
                for rb in range(0, L, SEQ):
                    xpad_s[halo + rb:halo + rb + SEQ, :] = qkvd_ref[r0 + rb:r0 + rb + SEQ, cs].astype(F32)
                for rb in range(0, L, SEQ):
                    acc = None
                    for i in range(DN_CONV):
                        lo = halo + rb + i - pad
                        term = xpad_s[lo:lo + SEQ, :] * convw_ref[i:i + 1, cs]
                        acc = term if acc is None else acc + term
                    dst[r0 + rb:r0 + rb + SEQ, hs] = finish(acc, kind)
            return carry

        lax.fori_loop(0, DN_HEADS, conv_head, 0)

    ab = ab_ref[...]
    glane = lax.broadcasted_iota(jnp.int32, ab.shape, 1)
    g = -jnp.exp(alog_ref[...]) * _softplus(ab + dtb_ref[...])
    gb_s[...] = jnp.where(glane < N_DIR * DN_HEADS, g, _sigmoid(ab))

    for s in range(n_seq):
        for d in range(N_DIR):
            for p in range(n_pairs):
                idx = (s * N_DIR + d) * n_pairs + p
                st_s[idx] = jnp.zeros((pair_w, pair_w), F32)
                if has_s0:
                    st_s[idx, 0:DK, 0:DV] = s0_ref[s, d, 2 * p]
                    st_s[idx, DK:2 * DK, DV:2 * DV] = s0_ref[s, d, 2 * p + 1]

    row = lax.broadcasted_iota(jnp.int32, (CHUNK, DN_HEADS * CHUNK), 0)
    col = lax.broadcasted_iota(jnp.int32, (CHUNK, DN_HEADS * CHUNK), 1) % CHUNK
    before_incl = [col <= row, col >= row]
    before_strict = [col < row, col > row]
    after_strict01 = [m.astype(F32) for m in before_strict]
    r64 = lax.broadcasted_iota(jnp.int32, (CHUNK, CHUNK), 0)
    c64 = lax.broadcasted_iota(jnp.int32, (CHUNK, CHUNK), 1)
    tri01 = [(c64 <= r64).astype(BF16), (c64 >= r64).astype(BF16)]

    def prep_step(n, carry):
        a_mats, v_betas, kb_egcs, slots = [], [], [], []
        for j in range(GDN_UNROLL):
            c = n * GDN_UNROLL + j
            rs = pl.ds(pl.multiple_of(c * CHUNK, CHUNK), CHUNK)
            gbc = gb_s[rs, :]
            q = qn_s[rs, :]
            k = kn_s[rs, :]
            v = vn_s[rs, :]
            k_bd = _block_diag(k.astype(BF16), DK)
            betas = [_bcast_cols(gbc, N_DIR * DN_HEADS + d * DN_HEADS, DK) for d in range(N_DIR)]
            kbs = [k * beta for beta in betas]
            kk_qk = _dot_nt(jnp.concatenate(kbs + [q], axis=0), k_bd)
            qk_raw = kk_qk[N_DIR * CHUNK:(N_DIR + 1) * CHUNK]
            for d in range(N_DIR):
                first = d * DN_HEADS
                gc_all = _dot_split(tri01[d], gbc)
                last = CHUNK - 1 if d == 0 else 0
                gc_last = gc_all[last:last + 1, :]
                gt_s[d, pl.ds(c, 1), :] = jnp.exp(gc_last)
                e_gc = _bcast_cols(jnp.exp(gc_all), first, DK)
                e_kd = _bcast_cols(jnp.exp(gc_last - gc_all), first, DK)
                gdiff = _dot_split(tri01[d], _bcast_cols(gbc, first, CHUNK) * after_strict01[d])
                decay = jnp.where(before_incl[d], jnp.exp(gdiff), 0.0)
                a_mats.append(jnp.where(before_strict[d], kk_qk[d * CHUNK:(d + 1) * CHUNK] * decay, 0.0))
                qk_s[d, rs, :] = jnp.where(before_incl[d], qk_raw * decay, 0.0).astype(BF16)
                qd_s[d, rs, :] = (q * e_gc).astype(BF16)
                kd_s[d, rs, :] = (k * e_kd).astype(BF16)
                v_betas.append(v * betas[d])
                kb_egcs.append(kbs[d] * e_gc)
                slots.append((d, rs))
        t_invs = _unit_tri_inverse(a_mats, row, col)
        for (d, rs), u0, w in zip(slots, _mm_heads(t_invs, v_betas, DV), _mm_heads(t_invs, kb_egcs, DK)):
            u0_s[d, rs, :] = u0
            w_s[d, rs, :] = w.astype(BF16)
        return carry

    lax.fori_loop(0, n_seq * n_chunks // GDN_UNROLL, prep_step, 0)

    plane = lax.broadcasted_iota(jnp.int32, (1, pair_w), 1)
    srow = lax.broadcasted_iota(jnp.int32, (pair_w, pair_w), 0) // DK
    scol = lax.broadcasted_iota(jnp.int32, (pair_w, pair_w), 1) // DV
    same_head = srow == scol

    def scan_step(n, carry):
        probs = []
        for s in range(n_seq):
            for d in range(N_DIR):
                c = s * n_chunks + (n if d == 0 else n_chunks - 1 - n)
                rs = pl.ds(pl.multiple_of(c * CHUNK, CHUNK), CHUNK)
                gt = gt_s[d, pl.ds(c, 1), :]
                for p in range(n_pairs):
                    c0 = d * DN_HEADS + 2 * p
                    g_tot = jnp.where(plane < DV, gt[:, c0:c0 + 1], gt[:, c0 + 1:c0 + 2])
                    probs.append((d, rs, p, (s * N_DIR + d) * n_pairs + p, slice(p * pair_w, (p + 1) * pair_w), g_tot))
        s_prev = [st_s[idx] for _, _, _, idx, _, _ in probs]
        s_b = [s.astype(BF16) for s in s_prev]
        ws = [jnp.dot(w_s[d, rs, ps], sb, preferred_element_type=F32) for (d, rs, _, _, ps, _), sb in zip(probs, s_b)]
        u_b = [(u0_s[d, rs, ps] - w).astype(BF16) for (d, rs, _, _, ps, _), w in zip(probs, ws)]
        upd = [lax.dot_general(kd_s[d, rs, ps], u, (((0,), (0,)), ((), ())), preferred_element_type=F32)
               for (d, rs, _, _, ps, _), u in zip(probs, u_b)]
        for (_, _, _, idx, _, g_tot), s, up in zip(probs, s_prev, upd):
            st_s[idx] = s * g_tot + jnp.where(same_head, up, 0.0)
        for (d, rs, p, _, ps, _), sb, u in zip(probs, s_b, u_b):
            o_s[d, rs, ps] = (jnp.dot(qd_s[d, rs, ps], sb, preferred_element_type=F32)
                              + jnp.dot(qk_s[d, rs, p * 2 * CHUNK:(p + 1) * 2 * CHUNK], _block_diag(u, DV),
                                        preferred_element_type=F32))
        return carry

    lax.fori_loop(0, n_chunks, scan_step, 0)

    for h in range(DN_HEADS):
        hs = slice(h * DV, (h + 1) * DV)
        o = o_s[0, :, hs] + o_s[1, :, hs]
        y = o * lax.rsqrt(jnp.mean(o * o, axis=-1, keepdims=True) + EPS) * ogain_ref[...]
        dn_ref[:, hs] = (y * _silu(z_ref[:, hs].astype(F32))).astype(dn_ref.dtype)
    if write_state:
        for s in range(n_seq):
            for d in range(N_DIR):
                for p in range(n_pairs):
                    idx = (s * N_DIR + d) * n_pairs + p
                    sfin_ref[s, d, 2 * p] = st_s[idx, 0:DK, 0:DV]
                    sfin_ref[s, d, 2 * p + 1] = st_s[idx, DK:2 * DK, DV:2 * DV]


def _gdn_call(l, qkvd, z, ab, conv_w, alog16, dtb16, o_gain, state_dn, mix, new_state=None, *, latent):
    L = DEC_SEQ if latent else SEQ
    n_seq = 1 if latent else GDN_CTX_SEQS
    n_steps = (DEC_BATCH if latent else BATCH) // n_seq
    rows = n_seq * L
    base = N_CTX_TOK // rows if latent else 0
    seq = lambda w: pl.BlockSpec((rows, w), lambda b: (base + b, 0))
    in_specs = [
        seq(DN_CONV_CH), seq(DN_V), seq(N_GATE_COLS),
        pl.BlockSpec((None, DN_CONV, DN_CONV_CH), lambda b: (l, 0, 0)),
        pl.BlockSpec((None, 1, N_GATE_COLS), lambda b: (l, 0, 0)),
        pl.BlockSpec((None, 1, N_GATE_COLS), lambda b: (l, 0, 0)),
        pl.BlockSpec((None, 1, DV), lambda b: (l, 0, 0)),
    ]
    args = [qkvd, z, ab, conv_w, alog16, dtb16, o_gain]
    out_specs = [pl.BlockSpec((rows, DN_V), lambda b: (base + b, 1))]
    out_shape = [jax.ShapeDtypeStruct((N_TOK, ATT_Q + DN_V), BF16)]
    if latent:
        in_specs.append(pl.BlockSpec((n_seq, None, N_DIR, DN_HEADS, DK, DV), lambda b: (b, l, 0, 0, 0, 0)))
        args.append(state_dn)
    in_specs.append(pl.BlockSpec(memory_space=pl.ANY))
    args.append(mix)
    aliases = {len(args) - 1: 0}
    if not latent:
        out_specs.append(pl.BlockSpec((n_seq, None, N_DIR, DN_HEADS, DK, DV), lambda b: (b, l, 0, 0, 0, 0)))
        out_shape.append(jax.ShapeDtypeStruct((BATCH, DEPTH, N_DIR, DN_HEADS, DK, DV), F32))
        if new_state is not None:
            in_specs.append(pl.BlockSpec(memory_space=pl.ANY))
            args.append(new_state)
            aliases[len(args) - 1] = 1
    n_chunks = rows // CHUNK
    return pl.pallas_call(
        functools.partial(_gdn_kernel, seq_len=L, n_seq=n_seq, has_s0=latent, write_state=not latent,
                          n_alias=len(aliases)),
        grid=(n_steps,),
        in_specs=in_specs,
        out_specs=out_specs,
        out_shape=out_shape,
        scratch_shapes=[
            pltpu.VMEM((rows, DN_QK), F32), pltpu.VMEM((rows, DN_QK), F32), pltpu.VMEM((rows, DN_V), F32),
            pltpu.VMEM((rows, N_GATE_COLS), F32),
            pltpu.VMEM((N_DIR, rows, DN_V), F32),
            pltpu.VMEM((N_DIR, rows, DN_QK), BF16),
            pltpu.VMEM((N_DIR, rows, DN_QK), BF16),
            pltpu.VMEM((N_DIR, rows, DN_QK), BF16),
            pltpu.VMEM((N_DIR, rows, DN_HEADS * CHUNK), BF16),
            pltpu.VMEM((N_DIR, n_chunks, N_GATE_COLS), F32),
            pltpu.VMEM((N_DIR, rows, DN_V), F32),
            pltpu.VMEM((n_seq * N_DIR * (DN_HEADS // 2), 2 * DK, 2 * DV), F32),
            pltpu.VMEM((L + 16, LANES), F32),
        ],
        input_output_aliases=aliases,
        compiler_params=_cparams("arbitrary"),
        name="gdn_lat" if latent else "gdn_ctx",
    )(*args)


def _top2_of4(vals):
    m1 = jnp.maximum(jnp.maximum(vals[0], vals[1]), jnp.maximum(vals[2], vals[3]))
    i1 = jnp.where(vals[0] == m1, 0, jnp.where(vals[1] == m1, 1, jnp.where(vals[2] == m1, 2, 3)))
    rest = [jnp.where(i1 == j, -jnp.inf, vals[j]) for j in range(4)]
    m2 = jnp.maximum(jnp.maximum(rest[0], rest[1]), jnp.maximum(rest[2], rest[3]))
    i2 = jnp.where(rest[0] == m2, 0, jnp.where(rest[1] == m2, 1, jnp.where(rest[2] == m2, 2, 3)))
    return m1, i1, m2, i2


def _outproj_kernel(x_ref, mix_ref, mod_ref, gain_ref, wo_ref, rwt_ref, rb_ref, ut_ref,
                    x1_ref, h2_ref, ri_ref, rg_ref, cnt_ref, carry_s):
    i = pl.program_id(0)

    @pl.when(i == 0)
    def _():
        carry_s[...] = jnp.zeros_like(carry_s)

    gate1 = mod_ref[:, 2 * D_MODEL:3 * D_MODEL]
    shift2 = mod_ref[:, 3 * D_MODEL:4 * D_MODEL]
    scale2 = mod_ref[:, 4 * D_MODEL:5 * D_MODEL]
    x1 = x_ref[...] + gate1 * jnp.dot(mix_ref[...], wo_ref[...], preferred_element_type=F32)
    x1_ref[...] = x1
    h2 = _adaln(x1, gain_ref[...], shift2, scale2)
    h2_ref[...] = h2

    h_hi, h_lo = _split_bf16(h2)
    r_hi, r_lo = _split_bf16(rwt_ref[...])
    nt = (((1,), (1,)), ((), ()))
    logits = (lax.dot_general(r_hi, h_hi, nt, preferred_element_type=F32)
              + lax.dot_general(r_hi, h_lo, nt, preferred_element_type=F32)
              + lax.dot_general(r_lo, h_hi, nt, preferred_element_type=F32))
    scores = _sigmoid(logits)
    biased = scores + rb_ref[...]
    rows = [biased[e:e + 1, :] for e in range(N_EXPERTS)]
    per_group = [_top2_of4(rows[g * 4:(g + 1) * 4]) for g in range(N_GROUPS)]
    gs = [pg[0] + pg[2] for pg in per_group]
    _, gsel, _, _ = _top2_of4(gs)
    pick = lambda k: jnp.where(gsel == 0, per_group[0][k], jnp.where(gsel == 1, per_group[1][k],
                               jnp.where(gsel == 2, per_group[2][k], per_group[3][k])))
    e1 = gsel * EXPERTS_PER_GROUP + pick(1)
    e2 = gsel * EXPERTS_PER_GROUP + pick(3)
    erow = lax.broadcasted_iota(jnp.int32, scores.shape, 0)
    oh1 = erow == e1
    oh2 = erow == e2
    s1 = jnp.sum(jnp.where(oh1, scores, 0.0), axis=0, keepdims=True)
    s2 = jnp.sum(jnp.where(oh2, scores, 0.0), axis=0, keepdims=True)
    tot = s1 + s2
    grow = lax.broadcasted_iota(jnp.int32, rg_ref.shape, 0)
    rg_ref[...] = jnp.where(grow == 0, s1 / tot, s2 / tot)

    sel = (oh1 | oh2).astype(BF16)
    before = jnp.dot(sel, ut_ref[...], preferred_element_type=F32) + carry_s[...]
    r1 = jnp.sum(jnp.where(oh1, before, 0.0), axis=0, keepdims=True)
    r2 = jnp.sum(jnp.where(oh2, before, 0.0), axis=0, keepdims=True)
    carry_s[...] = carry_s[...] + jnp.sum(sel.astype(F32), axis=1, keepdims=True)
    cnt_ref[...] = carry_s[...].astype(jnp.int32)
    irow = lax.broadcasted_iota(jnp.int32, ri_ref.shape, 0)
    ri_ref[...] = jnp.where(irow == 0, e1, jnp.where(irow == 1, e2, jnp.where(
        irow == 2, r1.astype(jnp.int32), r2.astype(jnp.int32))))


def _outproj_call(l, x, mix, mod, gain, w_out, router_wt, router_b, ut):
    tok = lambda w: pl.BlockSpec((TOK_TILE, w), lambda i: (i, 0))
    per_tok = lambda r: pl.BlockSpec((r, TOK_TILE), lambda i: (0, i))
    full = lambda a: pl.BlockSpec(a.shape, lambda i: (0,) * a.ndim)
    return pl.pallas_call(
        _outproj_kernel,
        grid=(N_TOK // TOK_TILE,),
        in_specs=[
            tok(D_MODEL), tok(ATT_Q + DN_V),
            pl.BlockSpec((None, None, 1, 6 * D_MODEL), lambda i: (l, _mod_row(i), 0, 0)),
            pl.BlockSpec((None, 1, D_MODEL), lambda i: (l, 0, 0)),
            pl.BlockSpec((None, ATT_Q + DN_V, D_MODEL), lambda i: (l, 0, 0)),
            full(router_wt), full(router_b), full(ut),
        ],
        out_specs=[tok(D_MODEL), tok(D_MODEL), per_tok(4), per_tok(2), pl.BlockSpec((N_EXPERTS, 1), lambda i: (0, 0))],
        out_shape=[
            jax.ShapeDtypeStruct((N_TOK, D_MODEL), F32),
            jax.ShapeDtypeStruct((N_TOK, D_MODEL), F32),
            jax.ShapeDtypeStruct((4, N_TOK), jnp.int32),
            jax.ShapeDtypeStruct((2, N_TOK), F32),
            jax.ShapeDtypeStruct((N_EXPERTS, 1), jnp.int32),
        ],
        scratch_shapes=[pltpu.VMEM((N_EXPERTS, 1), F32)],
        compiler_params=_cparams("arbitrary"),
        name="outproj_router",
    )(x, mix, mod, gain, w_out, router_wt, router_b, ut)


def _row_dma_loops(row_copy):
    def start(g, c):
        for j in range(SUBLANES):
            row_copy(g, j, 0).start()
            row_copy(g, j, 1).start()
        return c

    def wait(g, c):
        for j in range(SUBLANES):
            row_copy(g, j, 0).wait()
            row_copy(g, j, 1).wait()
        return c

    lax.fori_loop(0, TOK_TILE // SUBLANES, start, 0)
    lax.fori_loop(0, TOK_TILE // SUBLANES, wait, 0)


def _dispatch_kernel(pos_ref, h2_ref, buf_in_ref, sorted_ref, sem):
    del buf_in_ref

    def row_copy(g, j, k):
        p = pos_ref[0, k * TOK_TILE + g * SUBLANES + j]
        return pltpu.make_async_copy(h2_ref.at[g, pl.ds(j, 1), :], sorted_ref.at[pl.ds(p, 1), :], sem)

    _row_dma_loops(row_copy)


def _dispatch_call(pos_tiles, h2, zeros_sorted):
    return pl.pallas_call(
        _dispatch_kernel,
        grid=(N_TOK // TOK_TILE,),
        in_specs=[
            pl.BlockSpec((None, 1, TOP_K * TOK_TILE), lambda i: (i, 0, 0), memory_space=pltpu.SMEM),
            pl.BlockSpec((TOK_TILE // SUBLANES, SUBLANES, D_MODEL), lambda i: (i, 0, 0)),
            pl.BlockSpec(memory_space=pl.ANY),
        ],
        out_specs=pl.BlockSpec(memory_space=pl.ANY),
        out_shape=jax.ShapeDtypeStruct((N_SORTED, D_MODEL), F32),
        scratch_shapes=[pltpu.SemaphoreType.DMA(())],
        input_output_aliases={2: 0},
        compiler_params=_cparams("arbitrary"),
        name="moe_dispatch",
    )(pos_tiles, h2, zeros_sorted)


def _experts_kernel(te_ref, nu_ref, x_ref, wg_ref, wu_ref, wd_ref, y_ref, wg_s, wu_s, wd_s):
    j = pl.program_id(0)
    used = j < nu_ref[0]

    @pl.when((j == 0) | (te_ref[j] != te_ref[jnp.maximum(j - 1, 0)]))
    def _():
        wg_s[...] = wg_ref[...].astype(BF16)
        wu_s[...] = wu_ref[...].astype(BF16)
        wd_s[...] = wd_ref[...].astype(BF16)

    @pl.when(used)
    def _():
        x = x_ref[...].astype(BF16)
        hid = _silu(jnp.dot(x, wg_s[...], preferred_element_type=F32)) * jnp.dot(x, wu_s[...], preferred_element_type=F32)
        y_ref[...] = jnp.dot(hid.astype(BF16), wd_s[...], preferred_element_type=F32)

    @pl.when(jnp.logical_not(used))
    def _():
        y_ref[...] = jnp.zeros_like(y_ref)


def _experts_call(l, tile_expert, n_used, xs, w_gate, w_up, w_down):
    row = lambda j, te, nu: (jnp.maximum(jnp.minimum(j, nu[0] - 1), 0), 0)
    grid_spec = pltpu.PrefetchScalarGridSpec(
        num_scalar_prefetch=2,
        grid=(N_EXP_TILES,),
        in_specs=[
            pl.BlockSpec((EXP_TILE, D_MODEL), row),
            pl.BlockSpec((None, None, D_MODEL, D_FF), lambda j, te, nu: (l, te[j], 0, 0)),
            pl.BlockSpec((None, None, D_MODEL, D_FF), lambda j, te, nu: (l, te[j], 0, 0)),
            pl.BlockSpec((None, None, D_FF, D_MODEL), lambda j, te, nu: (l, te[j], 0, 0)),
        ],
        out_specs=pl.BlockSpec((EXP_TILE, D_MODEL), lambda j, te, nu: (j, 0)),
        scratch_shapes=[pltpu.VMEM((D_MODEL, D_FF), BF16), pltpu.VMEM((D_MODEL, D_FF), BF16),
                        pltpu.VMEM((D_FF, D_MODEL), BF16)],
    )
    return pl.pallas_call(
        _experts_kernel,
        grid_spec=grid_spec,
        out_shape=jax.ShapeDtypeStruct((N_SORTED, D_MODEL), F32),
        compiler_params=_cparams("arbitrary"),
        name="moe_experts",
    )(tile_expert, n_used, xs, w_gate, w_up, w_down)


def _combine_kernel(pos_ref, x1_ref, rg_ref, mod_ref, y_ref, out_ref, buf_s, sem):
    def row_copy(g, j, k):
        p = pos_ref[0, k * TOK_TILE + g * SUBLANES + j]
        return pltpu.make_async_copy(y_ref.at[pl.ds(p, 1), :], buf_s.at[k, g, pl.ds(j, 1), :], sem)

    _row_dma_loops(row_copy)
    gate2 = mod_ref[:, 5 * D_MODEL:6 * D_MODEL]
    moe = (buf_s[0].reshape(TOK_TILE, D_MODEL) * rg_ref[:, 0:1] + buf_s[1].reshape(TOK_TILE, D_MODEL) * rg_ref[:, 1:2])
    out_ref[...] = x1_ref[...] + gate2 * moe


def _combine_call(l, pos_tiles, x1, rg, mod, y_sorted):
    tok = lambda w: pl.BlockSpec((TOK_TILE, w), lambda i: (i, 0))
    return pl.pallas_call(
        _combine_kernel,
        grid=(N_TOK // TOK_TILE,),
        in_specs=[
            pl.BlockSpec((None, 1, TOP_K * TOK_TILE), lambda i: (i, 0, 0), memory_space=pltpu.SMEM),
            tok(D_MODEL), tok(2),
            pl.BlockSpec((None, None, 1, 6 * D_MODEL), lambda i: (l, _mod_row(i), 0, 0)),
            pl.BlockSpec(memory_space=pl.ANY),
        ],
        out_specs=tok(D_MODEL),
        out_shape=jax.ShapeDtypeStruct((N_TOK, D_MODEL), F32),
        scratch_shapes=[pltpu.VMEM((TOP_K, TOK_TILE // SUBLANES, SUBLANES, D_MODEL), F32), pltpu.SemaphoreType.DMA(())],
        compiler_params=_cparams("arbitrary"),
        name="moe_combine",
    )(pos_tiles, x1, rg, mod, y_sorted)


def _rope_tables():
    pos = jnp.arange(DEC_SEQ)
    r = (pos // GRID_W).astype(F32)
    c = (pos % GRID_W).astype(F32)
    inv = ROPE_BASE ** (-jnp.arange(ROPE_PAIRS, dtype=F32) / ROPE_PAIRS)
    ar, ac = r[:, None] * inv, c[:, None] * inv
    cos = jnp.concatenate([jnp.cos(ar), jnp.cos(ar), jnp.cos(ac), jnp.cos(ac)], axis=-1)
    sin = jnp.concatenate([-jnp.sin(ar), jnp.sin(ar), -jnp.sin(ac), jnp.sin(ac)], axis=-1)
    return jnp.tile(cos, (1, N_HEADS)), jnp.tile(sin, (1, N_HEADS))


def _routing_layout(ri, counts):
    counts = counts.reshape(N_EXPERTS)
    padded = ((counts + EXP_TILE - 1) // EXP_TILE) * EXP_TILE
    ends = jnp.cumsum(padded)
    offs = ends - padded
    eids = jnp.arange(N_EXPERTS, dtype=jnp.int32)[:, None, None]
    pos = jnp.sum(jnp.where(ri[None, 0:2] == eids, offs[:, None, None], 0), axis=0) + ri[2:4]
    n_used = (ends[-1] // EXP_TILE).astype(jnp.int32)
    tile_start = jnp.arange(N_EXP_TILES, dtype=jnp.int32) * EXP_TILE
    tile_expert = jnp.sum((tile_start[:, None] >= ends[None, :]).astype(jnp.int32), axis=1)
    last = jnp.sum((jnp.maximum(ends[-1] - EXP_TILE, 0) >= ends).astype(jnp.int32))
    tile_expert = jnp.minimum(jnp.where(tile_start < ends[-1], tile_expert, last), N_EXPERTS - 1).astype(jnp.int32)
    pos_tiles = pos.reshape(TOP_K, N_TOK // TOK_TILE, TOK_TILE).transpose(1, 0, 2).reshape(
        N_TOK // TOK_TILE, 1, TOP_K * TOK_TILE).astype(jnp.int32)
    return pos_tiles, tile_expert, n_used.reshape(1)


def kernel(x_prompt, x_sample, cache_k, cache_v, state_dn, c, c_ctx, w_ada, b_ada, norm_attn, norm_ffn,
           w_in, conv_w, a_log, dt_bias, q_norm, k_norm, sink, o_norm, w_out, router_w, router_bias,
           w_gate, w_up, w_down):
    x = jnp.concatenate([x_prompt.reshape(N_CTX_TOK, D_MODEL), x_sample.reshape(N_LAT_TOK, D_MODEL)], axis=0)
    cond = jnp.concatenate([c_ctx[None, :], c, jnp.zeros((N_COND - 1 - DEC_BATCH, D_MODEL), F32)], axis=0)
    mod = _modulation_call(cond, w_ada, b_ada).reshape(DEPTH, N_COND, 1, 6 * D_MODEL)

    w_in_b = w_in.astype(BF16)
    w_out_b = w_out.astype(BF16)
    seg = jnp.arange(ATT_Q) // HEAD_DIM
    bd = jnp.where(seg[:, None] == seg[None, :], 1.0 / HEAD_DIM, 0.0).astype(BF16)
    qg = jnp.tile(q_norm, (1, N_HEADS)).reshape(DEPTH, 1, ATT_Q)
    kg = jnp.tile(k_norm, (1, N_KV)).reshape(DEPTH, 1, ATT_KV)
    cos, sin = _rope_tables()
    gain1 = norm_attn.reshape(DEPTH, 1, D_MODEL)
    gain2 = norm_ffn.reshape(DEPTH, 1, D_MODEL)
    pad8 = lambda a: jnp.pad(a.reshape(DEPTH, 1, N_DIR * DN_HEADS), ((0, 0), (0, 0), (0, N_GATE_COLS - N_DIR * DN_HEADS)))
    alog16, dtb16 = pad8(a_log), pad8(dt_bias)
    o_gain = o_norm.reshape(DEPTH, 1, DV)
    ck = cache_k.reshape(DEC_BATCH, DEPTH, PAST_LEN, ATT_KV)
    cv = cache_v.reshape(DEC_BATCH, DEPTH, PAST_LEN, ATT_KV)
    router_wt = router_w.T
    rb = router_bias.reshape(N_EXPERTS, 1)
    tri = jnp.arange(TOK_TILE)
    ut = (tri[:, None] < tri[None, :]).astype(BF16)
    sorted_buf = jnp.zeros((N_SORTED, D_MODEL), F32)

    k_list, v_list, new_state = [], [], None
    for l in range(DEPTH):
        q, kv, qkvd, z, ab = _inproj_call(l, x, mod, gain1, w_in_b, bd, qg, kg, cos, sin)
        k_list.append(kv[:N_CTX_TOK, 0:ATT_KV].reshape(BATCH, SEQ, N_KV, HEAD_DIM))
        v_list.append(kv[:N_CTX_TOK, ATT_KV:].reshape(BATCH, SEQ, N_KV, HEAD_DIM))
        mix = _attn_ctx_call(sink[l], q, kv)
        mix = _attn_lat_call(l, sink[l], q, kv, ck, cv, mix)
        mix, new_state = _gdn_call(l, qkvd, z, ab, conv_w, alog16, dtb16, o_gain, state_dn, mix, new_state, latent=False)
        (mix,) = _gdn_call(l, qkvd, z, ab, conv_w, alog16, dtb16, o_gain, state_dn, mix, latent=True)
        x1, h2, ri, rg, counts = _outproj_call(l, x, mix, mod, gain2, w_out_b, router_wt, rb, ut)
        pos_tiles, tile_expert, n_used = _routing_layout(ri, counts)
        xs_sorted = _dispatch_call(pos_tiles, h2.reshape(N_TOK // SUBLANES, SUBLANES, D_MODEL), sorted_buf)
        sorted_buf = xs_sorted
        y_sorted = _experts_call(l, tile_expert, n_used, xs_sorted, w_gate, w_up, w_down)
        x = _combine_call(l, pos_tiles, x1, rg.T, mod, y_sorted)

    y_prompt = x[:N_CTX_TOK].reshape(BATCH, SEQ, D_MODEL)
    y_sample = x[N_CTX_TOK:].reshape(DEC_BATCH, DEC_SEQ, D_MODEL)
    return (y_prompt, y_sample, jnp.stack(k_list, axis=1), jnp.stack(v_list, axis=1), new_state)
```
